```python
import jax
import jax.numpy as jnp
from jax import lax
import numpy as np

D_MODEL = 1024
BATCH = 16
SEQ = 4096
DEPTH = 4

POOL_WINDOWS = (2, 4, 8, 16)
POOL_GROUP = D_MODEL // 8
D_POOL = POOL_GROUP * len(POOL_WINDOWS)
D_REC = D_MODEL // 2
N_REC_HEADS = 8
REC_HEAD = D_REC // N_REC_HEADS
CONV_WIDTH = 4
LRU_C = 8.0
D_IN_AB = D_POOL + 2 * D_REC
D_MIX_AB = D_POOL + D_REC
CHUNK = 128
D_SGU = D_MODEL
N_SGU_HEADS = 8
SGU_HEAD = D_SGU // N_SGU_HEADS
N_EXPERTS = 32
TOP_K = 4
D_FF = D_MODEL
SWIGLU_LIMIT = 7.0
SWIGLU_ALPHA = 1.702
EXPERT_BLOCK = 512
LN_EPS = 1e-5
DEEPNORM_ALPHA = (2 * DEPTH) ** 0.25
DEEPNORM_BETA = (8 * DEPTH) ** -0.25
N_EVEN = (DEPTH + 1) // 2
N_ODD = DEPTH // 2

kernel_name = "hybrid_pool_rglru_sgu_moe_trunk"


def _layer_norm(x, g, b):
    xf = x.astype(jnp.float32)
    mu = jnp.mean(xf, axis=-1, keepdims=True)
    var = jnp.mean(jnp.square(xf - mu), axis=-1, keepdims=True)
    return ((xf - mu) * lax.rsqrt(var + LN_EPS) * g + b).astype(x.dtype)


def _modulation(c_act, w_mod, b_mod):
    m = c_act @ w_mod + b_mod
    shift, scale, gate = jnp.split(m, 3, axis=-1)
    return shift[:, None], scale[:, None], 1.0 + gate[:, None]


def _pool_mixer(xp, pool_w, pool_scale):
    B_, S_, _ = xp.shape
    xf = xp.astype(jnp.float32)
    cs = jnp.pad(jnp.cumsum(xf, axis=1), ((0, 0), (1, 0), (0, 0)))
    t = jnp.arange(S_)
    outs = []
    for g, w in enumerate(POOL_WINDOWS):
        lo, hi = g * POOL_GROUP, (g + 1) * POOL_GROUP
        start = jnp.maximum(t + 1 - w, 0)
        window_sum = cs[:, 1:, lo:hi] - cs[:, start, lo:hi]
        count = jnp.minimum(t + 1, w).astype(jnp.float32)
        outs.append(window_sum / count[None, :, None] - xf[..., lo:hi])
    pooled = jnp.stack(outs, axis=2).astype(xp.dtype)
    mixed = jnp.einsum('bsgc,gcd->bsgd', pooled, pool_w)
    return mixed.reshape(B_, S_, D_POOL) * pool_scale


def _combine_linear(p, q):
    a1, b1 = p
    a2, b2 = q
    return a1 * a2, a2 * b1 + b2


def _rg_lru(xr, gr, conv_w, conv_b, w_a, b_a, w_x, b_x, lam):
    B_, S_, _ = xr.shape
    xpad = jnp.pad(xr, ((0, 0), (CONV_WIDTH - 1, 0), (0, 0)))
    xc = conv_b
    for k in range(CONV_WIDTH):
        xc = xc + xpad[:, k:k + S_] * conv_w[k]
    xh = xc.reshape(B_, S_, N_REC_HEADS, REC_HEAD)
    r = jax.nn.sigmoid(jnp.einsum('bshi,hij->bshj', xh, w_a).reshape(B_, S_, D_REC) + b_a)
    i = jax.nn.sigmoid(jnp.einsum('bshi,hij->bshj', xh, w_x).reshape(B_, S_, D_REC) + b_x)
    log_a = -LRU_C * r.astype(jnp.float32) * jax.nn.softplus(-lam.astype(jnp.float32))
    a = jnp.exp(log_a)
    mult = jnp.sqrt(-jnp.expm1(2.0 * log_a))
    b_in = mult * (i * xc).astype(jnp.float32)
    _, h = lax.associative_scan(_combine_linear, (a, b_in), axis=1)
    return h.astype(xr.dtype) * jax.nn.gelu(gr)


def _mixer_pool_lru(h, w_in, pool_w, pool_scale, conv_w, conv_b, w_a, b_a, w_x, b_x, lam,
                    w_out, b_out):
    z = h @ w_in
    xp = z[..., :D_POOL]
    xr = z[..., D_POOL:D_POOL + D_REC]
    gr = z[..., D_POOL + D_REC:]
    y_pool = _pool_mixer(xp, pool_w, pool_scale)
    y_rec = _rg_lru(xr, gr, conv_w, conv_b, w_a, b_a, w_x, b_x, lam)
    return jnp.concatenate([y_pool, y_rec], axis=-1) @ w_out + b_out


def _mixer_sgu(h, w_in, b_in, ln_g, ln_b, w_s, b_s, w_out, b_out):
    B_, S_, _ = h.shape
    z = jax.nn.gelu(h @ w_in + b_in)
    u, v = jnp.split(z, 2, axis=-1)
    v = _layer_norm(v, ln_g, ln_b)
    vc = v.reshape(B_, S_ // CHUNK, CHUNK, N_SGU_HEADS, SGU_HEAD)
    mask = jnp.tril(jnp.ones((CHUNK, CHUNK), dtype=bool))
    ws = jnp.where(mask[None], w_s, jnp.zeros_like(w_s))
    sv = jnp.einsum('hts,bnshc->bnthc', ws, vc) + b_s.T[None, None, :, :, None]
    return (u * sv.reshape(B_, S_, D_SGU)) @ w_out + b_out


def _moe_ffn(h, w_router, b_router, w1, b1, w2, b2):
    B_, S_, D = h.shape
    T = B_ * S_
    TK = T * TOP_K
    xt = h.reshape(T, D)
    logits = (xt @ w_router + b_router).astype(jnp.float32)
    top_vals, top_idx = lax.top_k(logits, TOP_K)
    gates = jax.nn.softmax(top_vals, axis=-1).astype(h.dtype)
    flat_e = top_idx.reshape(-1)
    flat_tok = jnp.repeat(jnp.arange(T, dtype=jnp.int32), TOP_K)
    flat_g = gates.reshape(-1)
    order = jnp.argsort(flat_e)
    se, stok, sg = flat_e[order], flat_tok[order], flat_g[order]
    counts = jnp.zeros((N_EXPERTS,), jnp.int32).at[flat_e].add(1)
    padded = ((counts + EXPERT_BLOCK - 1) // EXPERT_BLOCK) * EXPERT_BLOCK
    starts = jnp.cumsum(counts) - counts
    pends = jnp.cumsum(padded)
    pstarts = pends - padded
    dest = pstarts[se] + (jnp.arange(TK, dtype=jnp.int32) - starts[se])
    n_blocks = -(-TK // EXPERT_BLOCK) + N_EXPERTS
    cap = n_blocks * EXPERT_BLOCK
    buf_tok = jnp.full((cap,), T, jnp.int32).at[dest].set(stok)
    buf_g = jnp.zeros((cap,), h.dtype).at[dest].set(sg)
    block_start = jnp.arange(n_blocks, dtype=jnp.int32) * EXPERT_BLOCK
    block_e = jnp.clip(jnp.searchsorted(pends, block_start, side='right'), 0, N_EXPERTS - 1)
    x_pad = jnp.concatenate([xt, jnp.zeros((1, D), xt.dtype)], axis=0)

    def expert_block(args):
        tok_blk, g_blk, e = args
        hb = x_pad[tok_blk] @ w1[e] + b1[e]
        x_glu = jnp.minimum(hb[:, 0::2], SWIGLU_LIMIT)
        x_lin = jnp.clip(hb[:, 1::2], -SWIGLU_LIMIT, SWIGLU_LIMIT)
        act = x_glu * jax.nn.sigmoid(SWIGLU_ALPHA * x_glu) * (x_lin + 1.0)
        return (act @ w2[e] + b2[e]) * g_blk[:, None]

    yb = lax.map(expert_block, (buf_tok.reshape(n_blocks, EXPERT_BLOCK),
                                buf_g.reshape(n_blocks, EXPERT_BLOCK), block_e))
    out = jnp.zeros((T + 1, D), yb.dtype).at[buf_tok].add(yb.reshape(cap, D))[:T]
    return out.reshape(B_, S_, D)


def setup_inputs(seed: int = 0) -> dict:
    key = jax.random.key(seed)
    ks = iter(jax.random.split(key, 40))

    def nrm(shape, std):
        return jax.random.normal(next(ks), shape, jnp.float32) * std

    d = D_MODEL
    u = jax.random.uniform(next(ks), (N_EVEN, D_REC), jnp.float32, minval=0.9, maxval=0.999)
    a0 = u ** (1.0 / LRU_C)
    lru_lambda = jnp.log(a0) - jnp.log1p(-a0)
    return {
        'x': nrm((BATCH, SEQ, d), 1.0),
        'c': nrm((BATCH, d), 1.0),
        'mod_w': nrm((DEPTH, 2, d, 3 * d), 0.5 * d ** -0.5),
        'mod_b': nrm((DEPTH, 2, 3 * d), 0.01),
        'ln_g': 1.0 + nrm((DEPTH, 2, d), 0.05),
        'ln_b': nrm((DEPTH, 2, d), 0.01),
        'ab_w_in': nrm((N_EVEN, d, D_IN_AB), d ** -0.5),
        'pool_w': nrm((N_EVEN, len(POOL_WINDOWS), POOL_GROUP, POOL_GROUP), POOL_GROUP ** -0.5),
        'pool_scale': 1.0 + nrm((N_EVEN, D_POOL), 0.05),
        'conv_w': nrm((N_EVEN, CONV_WIDTH, D_REC), CONV_WIDTH ** -0.5),
        'conv_b': nrm((N_EVEN, D_REC), 0.01),
        'lru_w_a': nrm((N_EVEN, N_REC_HEADS, REC_HEAD, REC_HEAD), REC_HEAD ** -0.5),
        'lru_b_a': nrm((N_EVEN, D_REC), 0.01),
        'lru_w_x': nrm((N_EVEN, N_REC_HEADS, REC_HEAD, REC_HEAD), REC_HEAD ** -0.5),
        'lru_b_x': nrm((N_EVEN, D_REC), 0.01),
        'lru_lambda': lru_lambda,
        'ab_w_out': nrm((N_EVEN, D_MIX_AB, d), DEEPNORM_BETA * D_MIX_AB ** -0.5),
        'ab_b_out': nrm((N_EVEN, d), 0.01),
        'sgu_w_in': nrm((N_ODD, d, 2 * D_SGU), d ** -0.5),
        'sgu_b_in': nrm((N_ODD, 2 * D_SGU), 0.01),
        'sgu_ln_g': 1.0 + nrm((N_ODD, D_SGU), 0.05),
        'sgu_ln_b': nrm((N_ODD, D_SGU), 0.01),
        'sgu_w_s': nrm((N_ODD, N_SGU_HEADS, CHUNK, CHUNK), CHUNK ** -0.5),
        'sgu_b_s': 1.0 + nrm((N_ODD, N_SGU_HEADS, CHUNK), 0.01),
        'sgu_w_out': nrm((N_ODD, D_SGU, d), DEEPNORM_BETA * D_SGU ** -0.5),
        'sgu_b_out': nrm((N_ODD, d), 0.01),
        'router_w': nrm((DEPTH, d, N_EXPERTS), d ** -0.5),
        'router_b': nrm((DEPTH, N_EXPERTS), 0.01),
        'moe_w1': nrm((DEPTH, N_EXPERTS, d, 2 * D_FF), d ** -0.5),
        'moe_b1': nrm((DEPTH, N_EXPERTS, 2 * D_FF), 0.01),
        'moe_w2': nrm((DEPTH, N_EXPERTS, D_FF, d), DEEPNORM_BETA * D_FF ** -0.5),
        'moe_b2': nrm((DEPTH, N_EXPERTS, d), 0.01),
    }


def reference(x, c, mod_w, mod_b, ln_g, ln_b,
              ab_w_in, pool_w, pool_scale, conv_w, conv_b, lru_w_a, lru_b_a, lru_w_x, lru_b_x,
              lru_lambda, ab_w_out, ab_b_out,
              sgu_w_in, sgu_b_in, sgu_ln_g, sgu_ln_b, sgu_w_s, sgu_b_s, sgu_w_out, sgu_b_out,
              router_w, router_b, moe_w1, moe_b1, moe_w2, moe_b2):
    c_act = jax.nn.silu(c)
    for layer in range(DEPTH):
        j = layer // 2
        shift, scale, gate = _modulation(c_act, mod_w[layer, 0], mod_b[layer, 0])
        h = x * (1.0 + scale) + shift
        if layer % 2 == 0:
            y = _mixer_pool_lru(h, ab_w_in[j], pool_w[j], pool_scale[j], conv_w[j], conv_b[j],
                                lru_w_a[j], lru_b_a[j], lru_w_x[j], lru_b_x[j], lru_lambda[j],
                                ab_w_out[j], ab_b_out[j])
        else:
            y = _mixer_sgu(h, sgu_w_in[j], sgu_b_in[j], sgu_ln_g[j], sgu_ln_b[j], sgu_w_s[j],
                           sgu_b_s[j], sgu_w_out[j], sgu_b_out[j])
        x = _layer_norm(DEEPNORM_ALPHA * x + gate * y, ln_g[layer, 0], ln_b[layer, 0])
        shift, scale, gate = _modulation(c_act, mod_w[layer, 1], mod_b[layer, 1])
        h = x * (1.0 + scale) + shift
        y = _moe_ffn(h, router_w[layer], router_b[layer], moe_w1[layer], moe_b1[layer],
                     moe_w2[layer], moe_b2[layer])
        x = _layer_norm(DEEPNORM_ALPHA * x + gate * y, ln_g[layer, 1], ln_b[layer, 1])
    return x
```

```python
import functools

import jax
import jax.numpy as jnp
from jax import lax
from jax.experimental import pallas as pl
from jax.experimental.pallas import tpu as pltpu
from jax.experimental.pallas import tpu_sc as plsc

D = 1024
DEPTH = 4
POOL_WINDOWS = (2, 4, 8, 16)
POOL_GROUP = 128
D_POOL = 512
D_REC = 512
N_REC_HEADS = 8
REC_HEAD = 64
CONV_WIDTH = 4
LRU_C = 8.0
D_IN_AB = 1536
CHUNK = 128
N_SGU_HEADS = 8
N_EXPERTS = 32
TOP_K = 4
D_FF = 1024
SWIGLU_LIMIT = 7.0
SWIGLU_ALPHA = 1.702
EXPERT_BLOCK = 512
LN_EPS = 1e-5
ALPHA = (2 * DEPTH) ** 0.25

TS = 512
LANES = 128
ROW_TILES = D // LANES
POOL_PAD = 32
CONV_PAD = 8
SCAN_PAD = TS // 2
SC_CORES = 2
SC_SUBCORES = 16
SC_CHUNK = 64
VMEM_LIMIT = 56 * 1024 * 1024

_F32 = jnp.float32
_BF16 = jnp.bfloat16
_NT = (((1,), (1,)), ((), ()))


def _split(a):
    hi = a.astype(_BF16)
    lo = (a - hi.astype(_F32)).astype(_BF16)
    return hi, lo


def _dot(a, b):
    return jnp.dot(a, b, preferred_element_type=_F32)


def _layer_norm(x, g, b):
    mu = jnp.mean(x, axis=-1, keepdims=True)
    xc = x - mu
    var = jnp.mean(xc * xc, axis=-1, keepdims=True)
    return xc * lax.rsqrt(var + LN_EPS) * g + b


def _gelu(x):
    return jax.nn.gelu(x)


def _mod_kernel(c_ref, w_ref, b_ref, o_ref):
    c = c_ref[...]
    ca = c * jax.nn.sigmoid(c)
    a_hi, a_lo = _split(ca)
    w_hi, w_lo = _split(w_ref[...])
    o_ref[...] = _dot(a_hi, w_hi) + _dot(a_lo, w_hi) + _dot(a_hi, w_lo) + b_ref[...]


def _modulation(c, mod_w, mod_b):
    nb = c.shape[0]
    n = DEPTH * 2
    w = mod_w.reshape(n, D, 3 * D)
    b = mod_b.reshape(n, 1, 3 * D)
    out = pl.pallas_call(
        _mod_kernel,
        grid=(n, 3),
        in_specs=[
            pl.BlockSpec((nb, D), lambda i, j: (0, 0)),
            pl.BlockSpec((None, D, D), lambda i, j: (i, 0, j)),
            pl.BlockSpec((None, 1, D), lambda i, j: (i, 0, j)),
        ],
        out_specs=pl.BlockSpec((None, nb, D), lambda i, j: (i, 0, j)),
        out_shape=jax.ShapeDtypeStruct((n, nb, 3 * D), _F32),
        compiler_params=pltpu.CompilerParams(
            dimension_semantics=("arbitrary", "arbitrary"), vmem_limit_bytes=VMEM_LIMIT),
        name="modulation",
    )(c, w, b)
    return out.reshape(DEPTH, 2, nb, 3, D)


def _store_rows(ref, val):
    rows = val.shape[0]
    for j in range(ROW_TILES):
        ref[pl.ds(j, rows, stride=ROW_TILES), :] = val[:, j * LANES:(j + 1) * LANES]


def _load_rows(ref, rows):
    return jnp.concatenate(
        [ref[pl.ds(j, rows, stride=ROW_TILES), :] for j in range(ROW_TILES)], axis=1)


def _finish_sublayer(x, y, m1_ref, ln_ref, m2_ref, wr_hi_ref, wr_lo_ref, br_ref, tri_ref, first,
                     x1_ref, h2_ref, idx_ref, gate_ref, rank_ref, cnt_ref):
    gate1 = 1.0 + m1_ref[2:3, :]
    x1 = _layer_norm(ALPHA * x + gate1 * y, ln_ref[0:1, :], ln_ref[1:2, :])
    x1_ref[...] = x1
    h2 = x1 * (1.0 + m2_ref[1:2, :]) + m2_ref[0:1, :]
    _store_rows(h2_ref, h2)

    h_hi, h_lo = _split(h2)
    w_hi = wr_hi_ref[...]
    logits = (lax.dot_general(w_hi, h_hi, _NT, preferred_element_type=_F32)
              + lax.dot_general(wr_lo_ref[...], h_hi, _NT, preferred_element_type=_F32)
              + lax.dot_general(w_hi, h_lo, _NT, preferred_element_type=_F32)
              + br_ref[...])
    n_tok = logits.shape[1]
    iota_e = lax.broadcasted_iota(jnp.int32, (N_EXPERTS, n_tok), 0)
    work = logits
    vals, idxs, hots = [], [], []
    for _ in range(TOP_K):
        m = jnp.max(work, axis=0, keepdims=True)
        idx = jnp.min(jnp.where(work == m, iota_e, N_EXPERTS), axis=0, keepdims=True)
        hot = iota_e == idx
        work = jnp.where(hot, -jnp.inf, work)
        vals.append(m)
        idxs.append(idx)
        hots.append(hot)
    exps = [jnp.exp(v - vals[0]) for v in vals]
    denom = exps[0] + exps[1] + exps[2] + exps[3]
    gate_ref[...] = jnp.concatenate([e / denom for e in exps], axis=0)
    idx_ref[...] = jnp.concatenate(idxs, axis=0)

    sel = jnp.zeros((N_EXPERTS, n_tok), _F32)
    for hot in hots:
        sel = sel + jnp.where(hot, 1.0, 0.0)

    @pl.when(first)
    def _():
        cnt_ref[...] = jnp.zeros_like(cnt_ref)

    base = cnt_ref[:, 0:1]
    before = _dot(sel.astype(_BF16), tri_ref[...]) + base
    ranks = [jnp.sum(jnp.where(hot, before, 0.0), axis=0, keepdims=True) for hot in hots]
    rank_ref[...] = jnp.concatenate(ranks, axis=0).astype(jnp.int32)
    cnt_ref[...] = cnt_ref[...] + jnp.sum(sel, axis=1, keepdims=True)


def _epilogue_specs(n_seq_tiles):
    tile = lambda b, s: b * n_seq_tiles + s
    in_specs = [
        pl.BlockSpec((None, 3, D), lambda b, s: (b, 0, 0)),
        pl.BlockSpec((2, D), lambda b, s: (0, 0)),
        pl.BlockSpec((N_EXPERTS, D), lambda b, s: (0, 0)),
        pl.BlockSpec((N_EXPERTS, D), lambda b, s: (0, 0)),
        pl.BlockSpec((N_EXPERTS, 1), lambda b, s: (0, 0)),
        pl.BlockSpec((TS, TS), lambda b, s: (0, 0)),
    ]
    out_specs = [
        pl.BlockSpec((None, TS, D), lambda b, s: (b, s, 0)),
        pl.BlockSpec((TS * ROW_TILES, LANES), lambda b, s: (tile(b, s), 0)),
        pl.BlockSpec((TOP_K, TS), lambda b, s: (0, tile(b, s))),
        pl.BlockSpec((TOP_K, TS), lambda b, s: (0, tile(b, s))),
        pl.BlockSpec((TOP_K, TS), lambda b, s: (0, tile(b, s))),
        pl.BlockSpec((N_EXPERTS, LANES), lambda b, s: (0, 0)),
    ]
    return in_specs, out_specs


def _epilogue_out_shapes(nb, seq):
    t = nb * seq
    return [
        jax.ShapeDtypeStruct((nb, seq, D), _F32),
        jax.ShapeDtypeStruct((t * ROW_TILES, LANES), _F32),
        jax.ShapeDtypeStruct((TOP_K, t), jnp.int32),
        jax.ShapeDtypeStruct((TOP_K, t), _F32),
        jax.ShapeDtypeStruct((TOP_K, t), jnp.int32),
        jax.ShapeDtypeStruct((N_EXPERTS, LANES), _F32),
    ]


def _even_kernel(x_ref, m1_ref, win_ref, poolw_ref, pscale_ref, convw_ref, convb_ref, wg_ref,
                 bg_ref, lam_ref, wout_ref, bout_ref,
                 m2_ref, ln_ref, wr_hi_ref, wr_lo_ref, br_ref, tri_ref,
                 x1_ref, h2_ref, idx_ref, gate_ref, rank_ref, cnt_ref,
                 p1, p2, p4, p8, cbuf, sa0, sa1, sb0, sb1, hstate):
    b = pl.program_id(0)
    s = pl.program_id(1)
    x = x_ref[...]
    h = x * (1.0 + m1_ref[1:2, :]) + m1_ref[0:1, :]
    z = _dot(h.astype(_BF16), win_ref[...])
    xp = z[:, :D_POOL]
    xr = z[:, D_POOL:D_POOL + D_REC]
    gr = z[:, D_POOL + D_REC:]

    @pl.when(s == 0)
    def _():
        p1[0:POOL_PAD, :] = jnp.zeros((POOL_PAD, D_POOL), _F32)
        p2[0:POOL_PAD, :] = jnp.zeros((POOL_PAD, D_POOL), _F32)
        p4[0:POOL_PAD, :] = jnp.zeros((POOL_PAD, 3 * POOL_GROUP), _F32)
        p8[0:POOL_PAD, :] = jnp.zeros((POOL_PAD, 2 * POOL_GROUP), _F32)
        cbuf[0:CONV_PAD, :] = jnp.zeros((CONV_PAD, D_REC), _F32)
        hstate[...] = jnp.zeros_like(hstate)
        for buf, fill in ((sa0, 1.0), (sa1, 1.0), (sb0, 0.0), (sb1, 0.0)):
            buf[0:SCAN_PAD, :] = jnp.full((SCAN_PAD, D_REC), fill, _F32)

    rr = TS + 16
    p1[POOL_PAD:, :] = xp
    p2[16:, :] = p1[16:16 + rr, :] + p1[15:15 + rr, :]
    p4[16:, :] = p2[16:16 + rr, POOL_GROUP:] + p2[14:14 + rr, POOL_GROUP:]
    p8[16:, :] = p4[16:16 + rr, POOL_GROUP:] + p4[12:12 + rr, POOL_GROUP:]
    s16 = p8[POOL_PAD:, POOL_GROUP:] + p8[POOL_PAD - 8:POOL_PAD - 8 + TS, POOL_GROUP:]
    sums = (p2[POOL_PAD:, 0:POOL_GROUP], p4[POOL_PAD:, 0:POOL_GROUP],
            p8[POOL_PAD:, 0:POOL_GROUP], s16)
    pos = (s * TS + lax.broadcasted_iota(jnp.int32, (TS, 1), 0) + 1).astype(_F32)
    pooled = []
    for g, w in enumerate(POOL_WINDOWS):
        cnt = jnp.minimum(pos, float(w))
        pooled.append(sums[g] / cnt - xp[:, g * POOL_GROUP:(g + 1) * POOL_GROUP])
    pooled = jnp.concatenate(pooled, axis=1).astype(_BF16)
    y_pool = _dot(pooled, poolw_ref[...]) * pscale_ref[...]
    p1[16:POOL_PAD, :] = p1[TS + 16:TS + POOL_PAD, :]

    cbuf[CONV_PAD:, :] = xr
    xc = convb_ref[...]
    for k in range(CONV_WIDTH):
        off = CONV_PAD - (CONV_WIDTH - 1) + k
        xc = xc + cbuf[off:off + TS, :] * convw_ref[k:k + 1, :]
    cbuf[0:CONV_PAD, :] = cbuf[TS:TS + CONV_PAD, :]
    gates = _dot(xc.astype(_BF16), wg_ref[...]) + bg_ref[...]
    r_gate = jax.nn.sigmoid(gates[:, :D_REC])
    i_gate = jax.nn.sigmoid(gates[:, D_REC:])
    log_a = -LRU_C * r_gate * jax.nn.softplus(-lam_ref[...])
    a = jnp.exp(log_a)
    mult = jnp.sqrt(-jnp.tanh(log_a) * (a * a + 1.0))
    b_in = mult * (i_gate * xc)

    sa0[SCAN_PAD:, :] = a
    sb0[SCAN_PAD:, :] = b_in
    bufs = ((sa0, sb0), (sa1, sb1))
    d = 1
    lvl = 0
    while d < TS:
        sa, sb = bufs[lvl % 2]
        da, db = bufs[(lvl + 1) % 2]
        a_cur = sa[SCAN_PAD:, :]
        a_prev = sa[SCAN_PAD - d:SCAN_PAD - d + TS, :]
        b_cur = sb[SCAN_PAD:, :]
        b_prev = sb[SCAN_PAD - d:SCAN_PAD - d + TS, :]
        da[SCAN_PAD:, :] = a_cur * a_prev
        db[SCAN_PAD:, :] = b_cur + a_cur * b_prev
        d *= 2
        lvl += 1
    fa, fb = bufs[lvl % 2]
    h_rec = fb[SCAN_PAD:, :] + fa[SCAN_PAD:, :] * hstate[0:1, :]
    hstate[0:1, :] = h_rec[TS - 1:TS, :]
    y_rec = h_rec * _gelu(gr)

    y_mix = jnp.concatenate([y_pool, y_rec], axis=1).astype(_BF16)
    y = _dot(y_mix, wout_ref[...]) + bout_ref[...]
    first = jnp.logical_and(b == 0, s == 0)
    _finish_sublayer(x, y, m1_ref, ln_ref, m2_ref, wr_hi_ref, wr_lo_ref, br_ref, tri_ref, first,
                     x1_ref, h2_ref, idx_ref, gate_ref, rank_ref, cnt_ref)


def _block_diag(w):
    n, k, _ = w.shape
    eye = jnp.eye(n, dtype=w.dtype)
    return (eye[:, None, :, None] * w[:, :, None, :]).reshape(n * k, n * k)


def _even_layer(x, m1, m2, ln, router, tri, w_in, pool_w, pool_scale, conv_w, conv_b, w_a, b_a,
                w_x, b_x, lam, w_out, b_out):
    nb, seq, _ = x.shape
    n_s = seq // TS
    wr_hi, wr_lo, br = router
    wg = jnp.concatenate([_block_diag(w_a), _block_diag(w_x)], axis=1).astype(_BF16)
    bg = jnp.concatenate([b_a, b_x])[None, :]
    ep_in, ep_out = _epilogue_specs(n_s)
    const = lambda shape: pl.BlockSpec(shape, lambda b, s: (0,) * len(shape))
    in_specs = [
        pl.BlockSpec((None, TS, D), lambda b, s: (b, s, 0)),
        pl.BlockSpec((None, 3, D), lambda b, s: (b, 0, 0)),
        const((D, D_IN_AB)), const((D_POOL, D_POOL)), const((1, D_POOL)),
        const((CONV_WIDTH, D_REC)), const((1, D_REC)), const((D_REC, 2 * D_REC)),
        const((1, 2 * D_REC)), const((1, D_REC)), const((D, D)), const((1, D)),
    ] + ep_in
    scratch = [
        pltpu.VMEM((TS + POOL_PAD, D_POOL), _F32),
        pltpu.VMEM((TS + POOL_PAD, D_POOL), _F32),
        pltpu.VMEM((TS + POOL_PAD, 3 * POOL_GROUP), _F32),
        pltpu.VMEM((TS + POOL_PAD, 2 * POOL_GROUP), _F32),
        pltpu.VMEM((TS + CONV_PAD, D_REC), _F32),
        pltpu.VMEM((TS + SCAN_PAD, D_REC), _F32),
        pltpu.VMEM((TS + SCAN_PAD, D_REC), _F32),
        pltpu.VMEM((TS + SCAN_PAD, D_REC), _F32),
        pltpu.VMEM((TS + SCAN_PAD, D_REC), _F32),
        pltpu.VMEM((8, D_REC), _F32),
    ]
    return pl.pallas_call(
        _even_kernel,
        grid=(nb, n_s),
        in_specs=in_specs,
        out_specs=ep_out,
        out_shape=_epilogue_out_shapes(nb, seq),
        scratch_shapes=scratch,
        compiler_params=pltpu.CompilerParams(
            dimension_semantics=("arbitrary", "arbitrary"), vmem_limit_bytes=VMEM_LIMIT),
        name="even_mixer",
    )(x, m1, w_in.astype(_BF16), _block_diag(pool_w).astype(_BF16), pool_scale[None, :],
      conv_w, conv_b[None, :], wg, bg, lam[None, :], w_out.astype(_BF16), b_out[None, :],
      m2, ln, wr_hi, wr_lo, br, tri)


def _odd_kernel(x_ref, m1_ref, win_ref, bin_ref, sln_ref, ws_ref, bs_ref, wout_ref, bout_ref,
                m2_ref, ln_ref, wr_hi_ref, wr_lo_ref, br_ref, tri_ref,
                x1_ref, h2_ref, idx_ref, gate_ref, rank_ref, cnt_ref,
                sv_buf):
    b = pl.program_id(0)
    s = pl.program_id(1)
    x = x_ref[...]
    h = x * (1.0 + m1_ref[1:2, :]) + m1_ref[0:1, :]
    z = _gelu(_dot(h.astype(_BF16), win_ref[...]) + bin_ref[...])
    u = z[:, :D]
    v = _layer_norm(z[:, D:], sln_ref[0:1, :], sln_ref[1:2, :]).astype(_BF16)
    n_chunks = TS // CHUNK
    row = lax.broadcasted_iota(jnp.int32, (CHUNK, CHUNK), 0)
    col = lax.broadcasted_iota(jnp.int32, (CHUNK, CHUNK), 1)
    causal = col <= row
    for hh in range(N_SGU_HEADS):
        lanes = slice(hh * CHUNK, (hh + 1) * CHUNK)
        w_h = jnp.where(causal, ws_ref[hh], 0.0).astype(_BF16)
        v_h = jnp.concatenate(
            [v[n * CHUNK:(n + 1) * CHUNK, lanes] for n in range(n_chunks)], axis=1)
        sv_h = _dot(w_h, v_h)
        for n in range(n_chunks):
            sv_buf[n * CHUNK:(n + 1) * CHUNK, lanes] = (
                sv_h[:, n * CHUNK:(n + 1) * CHUNK] + bs_ref[hh])
    y = _dot((u * sv_buf[...]).astype(_BF16), wout_ref[...]) + bout_ref[...]
    first = jnp.logical_and(b == 0, s == 0)
    _finish_sublayer(x, y, m1_ref, ln_ref, m2_ref, wr_hi_ref, wr_lo_ref, br_ref, tri_ref, first,
                     x1_ref, h2_ref, idx_ref, gate_ref, rank_ref, cnt_ref)


def _odd_layer(x, m1, m2, ln, router, tri, w_in, b_in, ln_g, ln_b, w_s, b_s, w_out, b_out):
    nb, seq, _ = x.shape
    n_s = seq // TS
    wr_hi, wr_lo, br = router
    sln = jnp.stack([ln_g, ln_b])
    bs = jnp.broadcast_to(b_s[:, :, None], (N_SGU_HEADS, CHUNK, CHUNK))
    ep_in, ep_out = _epilogue_specs(n_s)
    const = lambda shape: pl.BlockSpec(shape, lambda b, s: (0,) * len(shape))
    in_specs = [
        pl.BlockSpec((None, TS, D), lambda b, s: (b, s, 0)),
        pl.BlockSpec((None, 3, D), lambda b, s: (b, 0, 0)),
        const((D, 2 * D)), const((1, 2 * D)), const((2, D)),
        const((N_SGU_HEADS, CHUNK, CHUNK)), const((N_SGU_HEADS, CHUNK, CHUNK)),
        const((D, D)), const((1, D)),
    ] + ep_in
    return pl.pallas_call(
        _odd_kernel,
        grid=(nb, n_s),
        in_specs=in_specs,
        out_specs=ep_out,
        out_shape=_epilogue_out_shapes(nb, seq),
        scratch_shapes=[pltpu.VMEM((TS, D), _F32)],
        compiler_params=pltpu.CompilerParams(
            dimension_semantics=("arbitrary", "arbitrary"), vmem_limit_bytes=VMEM_LIMIT),
        name="odd_mixer",
    )(x, m1, w_in.astype(_BF16), b_in[None, :], sln, w_s, bs, w_out.astype(_BF16),
      b_out[None, :], m2, ln, wr_hi, wr_lo, br, tri)


def _plan_kernel(pstart_ref, idx_ref, rank_ref, dest_ref):
    idx = idx_ref[...]
    dest = rank_ref[...]
    for e in range(N_EXPERTS):
        dest = dest + jnp.where(idx == e, pstart_ref[e], 0)
    dest_ref[...] = dest


def _plan(idx, rank, counts):
    t = idx.shape[1]
    n_blocks = (t * TOP_K) // EXPERT_BLOCK + N_EXPERTS
    cnt = counts[:, 0].astype(jnp.int32)
    padded = ((cnt + EXPERT_BLOCK - 1) // EXPERT_BLOCK) * EXPERT_BLOCK
    pends = jnp.cumsum(padded)
    pstart = pends - padded
    block_start = jnp.arange(n_blocks, dtype=jnp.int32) * EXPERT_BLOCK
    block_e = jnp.clip(jnp.searchsorted(pends, block_start, side='right'), 0,
                       N_EXPERTS - 1).astype(jnp.int32)
    n_used = (pends[-1] // EXPERT_BLOCK).astype(jnp.int32)
    n_valid = jnp.clip(pstart[block_e] + cnt[block_e] - block_start, 0, EXPERT_BLOCK)
    n_valid = jnp.where(block_start < pends[-1], n_valid, 0).astype(jnp.int32)
    tl = min(t, 8192)
    dest = pl.pallas_call(
        _plan_kernel,
        grid_spec=pltpu.PrefetchScalarGridSpec(
            num_scalar_prefetch=1,
            grid=(t // tl,),
            in_specs=[pl.BlockSpec((TOP_K, tl), lambda i, p: (0, i)),
                      pl.BlockSpec((TOP_K, tl), lambda i, p: (0, i))],
            out_specs=pl.BlockSpec((TOP_K, tl), lambda i, p: (0, i)),
        ),
        out_shape=jax.ShapeDtypeStruct((TOP_K, t), jnp.int32),
        compiler_params=pltpu.CompilerParams(dimension_semantics=("arbitrary",)),
        name="moe_plan",
    )(pstart.astype(jnp.int32), idx, rank)
    dest3 = dest.reshape(TOP_K, t // SC_CHUNK, SC_CHUNK).transpose(1, 0, 2)
    return dest3, block_e, n_valid, n_used.reshape(1)


def _sc_mesh():
    return plsc.VectorSubcoreMesh(core_axis_name="c", subcore_axis_name="s")


def _sc_dispatch(h2_rows, dest3, cap):
    t = h2_rows.shape[0]
    per_w = (t // SC_CHUNK) // (SC_CORES * SC_SUBCORES)

    @functools.partial(
        pl.kernel, mesh=_sc_mesh(),
        out_type=jax.ShapeDtypeStruct((cap, ROW_TILES, LANES), _F32),
        scratch_types=[pltpu.VMEM((TOP_K, SC_CHUNK), jnp.int32),
                       pltpu.VMEM((SC_CHUNK, ROW_TILES, LANES), _F32)],
        name="moe_dispatch",
    )
    def run(x_hbm, dest_hbm, out_hbm, idx_v, rows_v):
        wid = lax.axis_index("s") * SC_CORES + lax.axis_index("c")

        @pl.loop(0, per_w)
        def _(i):
            c = wid * per_w + i
            pltpu.sync_copy(dest_hbm.at[c], idx_v)
            pltpu.sync_copy(x_hbm.at[pl.ds(c * SC_CHUNK, SC_CHUNK)], rows_v)
            for k in range(TOP_K):
                pltpu.sync_copy(rows_v, out_hbm.at[idx_v.at[k]])

    return run(h2_rows, dest3)


def _sc_combine(ys_rows, dest3, t):
    per_w = (t // SC_CHUNK) // (SC_CORES * SC_SUBCORES)

    @functools.partial(
        pl.kernel, mesh=_sc_mesh(),
        out_type=jax.ShapeDtypeStruct((TOP_K, t, ROW_TILES, LANES), _F32),
        scratch_types=[pltpu.VMEM((TOP_K, SC_CHUNK), jnp.int32),
                       pltpu.VMEM((SC_CHUNK, ROW_TILES, LANES), _F32)],
        name="moe_combine",
    )
    def run(ys_hbm, dest_hbm, out_hbm, idx_v, rows_v):
        wid = lax.axis_index("s") * SC_CORES + lax.axis_index("c")

        @pl.loop(0, per_w)
        def _(i):
            c = wid * per_w + i
            pltpu.sync_copy(dest_hbm.at[c], idx_v)
            for k in range(TOP_K):
                pltpu.sync_copy(ys_hbm.at[idx_v.at[k]], rows_v)
                pltpu.sync_copy(rows_v, out_hbm.at[k, pl.ds(c * SC_CHUNK, SC_CHUNK)])

    return run(ys_rows, dest3)


def _ffn_kernel(be_ref, nv_ref, nu_ref, xs_ref, w1_ref, b1_ref, w2_ref, b2_ref, ys_ref):
    i = pl.program_id(0)

    @pl.when(i < nu_ref[0])
    def _():
        x = _load_rows(xs_ref, EXPERT_BLOCK)
        rows = lax.broadcasted_iota(jnp.int32, (EXPERT_BLOCK, 1), 0)
        x = jnp.where(rows < nv_ref[i], x, 0.0).astype(_BF16)
        hb = _dot(x, w1_ref[...]) + b1_ref[...]
        x_glu = jnp.minimum(hb[:, :D_FF], SWIGLU_LIMIT)
        x_lin = jnp.clip(hb[:, D_FF:], -SWIGLU_LIMIT, SWIGLU_LIMIT)
        act = x_glu * jax.nn.sigmoid(SWIGLU_ALPHA * x_glu) * (x_lin + 1.0)
        y = _dot(act.astype(_BF16), w2_ref[...]) + b2_ref[...]
        _store_rows(ys_ref, y)


def _expert_ffn(xs_rows, block_e, n_valid, n_used, w1, b1, w2, b2):
    n_blocks = block_e.shape[0]
    blk = lambda i, be, nv, nu: (jnp.minimum(i, nu[0] - 1), 0)
    wsel = lambda i, be, nv, nu: (be[jnp.minimum(i, nu[0] - 1)], 0, 0)
    return pl.pallas_call(
        _ffn_kernel,
        grid_spec=pltpu.PrefetchScalarGridSpec(
            num_scalar_prefetch=3,
            grid=(n_blocks,),
            in_specs=[
                pl.BlockSpec((EXPERT_BLOCK * ROW_TILES, LANES), blk),
                pl.BlockSpec((None, D, 2 * D_FF), wsel),
                pl.BlockSpec((None, 1, 2 * D_FF), wsel),
                pl.BlockSpec((None, D_FF, D), wsel),
                pl.BlockSpec((None, 1, D), wsel),
            ],
            out_specs=pl.BlockSpec((EXPERT_BLOCK * ROW_TILES, LANES), blk),
        ),
        out_shape=jax.ShapeDtypeStruct(xs_rows.shape, _F32),
        compiler_params=pltpu.CompilerParams(
            dimension_semantics=("arbitrary",), vmem_limit_bytes=VMEM_LIMIT),
        name="expert_ffn",
    )(block_e, n_valid, n_used, xs_rows, w1, b1, w2, b2)


def _post_kernel(x1_ref, yg_ref, gate_ref, m2_ref, ln_ref, o_ref):
    gt = gate_ref[...]
    pad = jnp.zeros((LANES - TOP_K, TS), _F32)
    g_rows = jnp.concatenate([gt, pad], axis=0).T
    y = jnp.zeros((TS, D), _F32)
    for k in range(TOP_K):
        y = y + g_rows[:, k:k + 1] * _load_rows(yg_ref.at[k], TS)
    gate2 = 1.0 + m2_ref[2:3, :]
    o_ref[...] = _layer_norm(ALPHA * x1_ref[...] + gate2 * y, ln_ref[0:1, :], ln_ref[1:2, :])


def _moe_post(x1, yg, gates, m2, ln):
    nb, seq, _ = x1.shape
    n_s = seq // TS
    tile = lambda b, s: b * n_s + s
    return pl.pallas_call(
        _post_kernel,
        grid=(nb, n_s),
        in_specs=[
            pl.BlockSpec((None, TS, D), lambda b, s: (b, s, 0)),
            pl.BlockSpec((TOP_K, TS * ROW_TILES, LANES), lambda b, s: (0, tile(b, s), 0)),
            pl.BlockSpec((TOP_K, TS), lambda b, s: (0, tile(b, s))),
            pl.BlockSpec((None, 3, D), lambda b, s: (b, 0, 0)),
            pl.BlockSpec((2, D), lambda b, s: (0, 0)),
        ],
        out_specs=pl.BlockSpec((None, TS, D), lambda b, s: (b, s, 0)),
        out_shape=jax.ShapeDtypeStruct((nb, seq, D), _F32),
        compiler_params=pltpu.CompilerParams(
            dimension_semantics=("arbitrary", "arbitrary"), vmem_limit_bytes=VMEM_LIMIT),
        name="moe_post",
    )(x1, yg, gates, m2, ln)


def _moe(x1, h2_rows, idx, gates, rank, counts, m2, ln, w1, b1, w2, b2):
    nb, seq, _ = x1.shape
    t = nb * seq
    cap = (t * TOP_K // EXPERT_BLOCK + N_EXPERTS) * EXPERT_BLOCK
    dest3, block_e, n_valid, n_used = _plan(idx, rank, counts)
    xs = _sc_dispatch(h2_rows.reshape(t, ROW_TILES, LANES), dest3, cap)
    ys = _expert_ffn(xs.reshape(cap * ROW_TILES, LANES), block_e, n_valid, n_used, w1, b1, w2, b2)
    yg = _sc_combine(ys.reshape(cap, ROW_TILES, LANES), dest3, t)
    return _moe_post(x1, yg.reshape(TOP_K, t * ROW_TILES, LANES), gates, m2, ln)


def kernel(x, c, mod_w, mod_b, ln_g, ln_b, ab_w_in, pool_w, pool_scale, conv_w, conv_b, lru_w_a, lru_b_a, lru_w_x, lru_b_x, lru_lambda, ab_w_out, ab_b_out, sgu_w_in, sgu_b_in, sgu_ln_g, sgu_ln_b, sgu_w_s, sgu_b_s, sgu_w_out, sgu_b_out, router_w, router_b, moe_w1, moe_b1, moe_w2, moe_b2):
    assert x.shape[1] % TS == 0 and x.shape[2] == D
    mods = _modulation(c, mod_w, mod_b)
    tri = (lax.broadcasted_iota(jnp.int32, (TS, TS), 0)
           < lax.broadcasted_iota(jnp.int32, (TS, TS), 1)).astype(_BF16)
    w1 = jnp.concatenate([moe_w1[..., 0::2], moe_w1[..., 1::2]], axis=-1).astype(_BF16)
    b1 = jnp.concatenate([moe_b1[..., 0::2], moe_b1[..., 1::2]], axis=-1)[:, :, None, :]
    w2 = moe_w2.astype(_BF16)
    b2 = moe_b2[:, :, None, :]
    for layer in range(DEPTH):
        j = layer // 2
        m1, m2 = mods[layer, 0], mods[layer, 1]
        ln1 = jnp.stack([ln_g[layer, 0], ln_b[layer, 0]])
        ln2 = jnp.stack([ln_g[layer, 1], ln_b[layer, 1]])
        wr_hi, wr_lo = _split(router_w[layer].T)
        router = (wr_hi, wr_lo, router_b[layer][:, None])
        if layer % 2 == 0:
            outs = _even_layer(x, m1, m2, ln1, router, tri, ab_w_in[j], pool_w[j], pool_scale[j],
                               conv_w[j], conv_b[j], lru_w_a[j], lru_b_a[j], lru_w_x[j],
                               lru_b_x[j], lru_lambda[j], ab_w_out[j], ab_b_out[j])
        else:
            outs = _odd_layer(x, m1, m2, ln1, router, tri, sgu_w_in[j], sgu_b_in[j], sgu_ln_g[j],
                              sgu_ln_b[j], sgu_w_s[j], sgu_b_s[j], sgu_w_out[j], sgu_b_out[j])
        x1, h2_rows, idx, gates, rank, counts = outs
        x = _moe(x1, h2_rows, idx, gates, rank, counts, m2, ln2,
                 w1[layer], b1[layer], w2[layer], b2[layer])
    return x
```

```python
import functools

import jax
import jax.numpy as jnp
from jax import lax
from jax.experimental import pallas as pl
from jax.experimental.pallas import tpu as pltpu
from jax.experimental.pallas import tpu_sc as plsc

D = 1024
DEPTH = 4
POOL_WINDOWS = (2, 4, 8, 16)
POOL_GROUP = 128
D_POOL = 512
D_REC = 512
N_REC_HEADS = 8
REC_HEAD = 64
CONV_WIDTH = 4
LRU_C = 8.0
D_IN_AB = 1536
CHUNK = 128
N_SGU_HEADS = 8
N_EXPERTS = 32
TOP_K = 4
D_FF = 1024
SWIGLU_LIMIT = 7.0
SWIGLU_ALPHA = 1.702
EXPERT_BLOCK = 512
LN_EPS = 1e-5
ALPHA = (2 * DEPTH) ** 0.25

TS = 512
LANES = 128
ROW_TILES = D // LANES
POOL_PAD = 32
CONV_PAD = 8
SCAN_PAD = TS // 2
SC_CORES = 2
SC_SUBCORES = 16
SC_CHUNK = 64
PERM = 256
VMEM_LIMIT = 56 * 1024 * 1024

_F32 = jnp.float32
_BF16 = jnp.bfloat16
_NT = (((1,), (1,)), ((), ()))


def _split(a):
    hi = a.astype(_BF16)
    lo = (a - hi.astype(_F32)).astype(_BF16)
    return hi, lo


def _dot(a, b):
    return jnp.dot(a, b, preferred_element_type=_F32)


def _layer_norm(x, g, b):
    mu = jnp.mean(x, axis=-1, keepdims=True)
    xc = x - mu
    var = jnp.mean(xc * xc, axis=-1, keepdims=True)
    return xc * lax.rsqrt(var + LN_EPS) * g + b


def _gelu(x):
    return jax.nn.gelu(x)


def _mod_kernel(c_ref, w_ref, b_ref, o_ref):
    c = c_ref[...]
    ca = c * jax.nn.sigmoid(c)
    a_hi, a_lo = _split(ca)
    w_hi, w_lo = _split(w_ref[...])
    o_ref[...] = _dot(a_hi, w_hi) + _dot(a_lo, w_hi) + _dot(a_hi, w_lo) + b_ref[...]


def _modulation(c, mod_w, mod_b):
    nb = c.shape[0]
    n = DEPTH * 2
    w = mod_w.reshape(n, D, 3 * D)
    b = mod_b.reshape(n, 1, 3 * D)
    out = pl.pallas_call(
        _mod_kernel,
        grid=(n, 3),
        in_specs=[
            pl.BlockSpec((nb, D), lambda i, j: (0, 0)),
            pl.BlockSpec((None, D, D), lambda i, j: (i, 0, j)),
            pl.BlockSpec((None, 1, D), lambda i, j: (i, 0, j)),
        ],
        out_specs=pl.BlockSpec((None, nb, D), lambda i, j: (i, 0, j)),
        out_shape=jax.ShapeDtypeStruct((n, nb, 3 * D), _F32),
        compiler_params=pltpu.CompilerParams(
            dimension_semantics=("arbitrary", "arbitrary"), vmem_limit_bytes=VMEM_LIMIT),
        name="modulation",
    )(c, w, b)
    return out.reshape(DEPTH, 2, nb, 3, D)


def _store_rows(ref, val):
    rows = val.shape[0]
    for j in range(ROW_TILES):
        ref[pl.ds(j, rows, stride=ROW_TILES), :] = val[:, j * LANES:(j + 1) * LANES]


def _load_rows(ref, rows):
    return jnp.concatenate(
        [ref[pl.ds(j, rows, stride=ROW_TILES), :] for j in range(ROW_TILES)], axis=1)


def _finish_sublayer(x, y, m1_ref, ln_ref, m2_ref, wr_hi_ref, wr_lo_ref, br_ref, tri_ref, first,
                     x1_ref, h2_ref, idx_ref, gate_ref, rank_ref, cnt_ref):
    gate1 = 1.0 + m1_ref[2:3, :]
    x1 = _layer_norm(ALPHA * x + gate1 * y, ln_ref[0:1, :], ln_ref[1:2, :])
    x1_ref[...] = x1
    h2 = x1 * (1.0 + m2_ref[1:2, :]) + m2_ref[0:1, :]
    _store_rows(h2_ref, h2)

    h_hi, h_lo = _split(h2)
    w_hi = wr_hi_ref[...]
    logits = (lax.dot_general(w_hi, h_hi, _NT, preferred_element_type=_F32)
              + lax.dot_general(wr_lo_ref[...], h_hi, _NT, preferred_element_type=_F32)
              + lax.dot_general(w_hi, h_lo, _NT, preferred_element_type=_F32)
              + br_ref[...])
    n_tok = logits.shape[1]
    iota_e = lax.broadcasted_iota(jnp.int32, (N_EXPERTS, n_tok), 0)
    work = logits
    vals, idxs, hots = [], [], []
    for _ in range(TOP_K):
        m = jnp.max(work, axis=0, keepdims=True)
        idx = jnp.min(jnp.where(work == m, iota_e, N_EXPERTS), axis=0, keepdims=True)
        hot = iota_e == idx
        work = jnp.where(hot, -jnp.inf, work)
        vals.append(m)
        idxs.append(idx)
        hots.append(hot)
    exps = [jnp.exp(v - vals[0]) for v in vals]
    denom = exps[0] + exps[1] + exps[2] + exps[3]
    gate_ref[...] = jnp.concatenate([e / denom for e in exps], axis=0)
    idx_ref[...] = jnp.concatenate(idxs, axis=0)

    sel = jnp.zeros((N_EXPERTS, n_tok), _F32)
    for hot in hots:
        sel = sel + jnp.where(hot, 1.0, 0.0)

    @pl.when(first)
    def _():
        cnt_ref[...] = jnp.zeros_like(cnt_ref)

    base = cnt_ref[:, 0:1]
    before = _dot(sel.astype(_BF16), tri_ref[...]) + base
    ranks = [jnp.sum(jnp.where(hot, before, 0.0), axis=0, keepdims=True) for hot in hots]
    rank_ref[...] = jnp.concatenate(ranks, axis=0).astype(jnp.int32)
    cnt_ref[...] = cnt_ref[...] + jnp.sum(sel, axis=1, keepdims=True)


def _epilogue_specs(n_seq_tiles, mb0):
    tile = lambda b, s: b * n_seq_tiles + s
    in_specs = [
        pl.BlockSpec((None, 3, D), lambda b, s: (b + mb0, 0, 0)),
        pl.BlockSpec((2, D), lambda b, s: (0, 0)),
        pl.BlockSpec((N_EXPERTS, D), lambda b, s: (0, 0)),
        pl.BlockSpec((N_EXPERTS, D), lambda b, s: (0, 0)),
        pl.BlockSpec((N_EXPERTS, 1), lambda b, s: (0, 0)),
        pl.BlockSpec((TS, TS), lambda b, s: (0, 0)),
    ]
    out_specs = [
        pl.BlockSpec((None, TS, D), lambda b, s: (b, s, 0)),
        pl.BlockSpec((TS * ROW_TILES, LANES), lambda b, s: (tile(b, s), 0)),
        pl.BlockSpec((TOP_K, TS), lambda b, s: (0, tile(b, s))),
        pl.BlockSpec((TOP_K, TS), lambda b, s: (0, tile(b, s))),
        pl.BlockSpec((TOP_K, TS), lambda b, s: (0, tile(b, s))),
        pl.BlockSpec((N_EXPERTS, LANES), lambda b, s: (0, 0)),
    ]
    return in_specs, out_specs


def _epilogue_out_shapes(nb, seq):
    t = nb * seq
    return [
        jax.ShapeDtypeStruct((nb, seq, D), _F32),
        jax.ShapeDtypeStruct((t * ROW_TILES, LANES), _F32),
        jax.ShapeDtypeStruct((TOP_K, t), jnp.int32),
        jax.ShapeDtypeStruct((TOP_K, t), _F32),
        jax.ShapeDtypeStruct((TOP_K, t), jnp.int32),
        jax.ShapeDtypeStruct((N_EXPERTS, LANES), _F32),
    ]


def _even_kernel(x_ref, m1_ref, win_ref, poolw_ref, pscale_ref, convw_ref, convb_ref, wg_ref,
                 bg_ref, lam_ref, wout_ref, bout_ref,
                 m2_ref, ln_ref, wr_hi_ref, wr_lo_ref, br_ref, tri_ref,
                 x1_ref, h2_ref, idx_ref, gate_ref, rank_ref, cnt_ref,
                 p1, p2, p4, p8, cbuf, sa0, sa1, sb0, sb1, hstate):
    b = pl.program_id(0)
    s = pl.program_id(1)
    x = x_ref[...]
    h = x * (1.0 + m1_ref[1:2, :]) + m1_ref[0:1, :]
    z = _dot(h.astype(_BF16), win_ref[...])
    xp = z[:, :D_POOL]
    xr = z[:, D_POOL:D_POOL + D_REC]
    gr = z[:, D_POOL + D_REC:]

    @pl.when(s == 0)
    def _():
        p1[0:POOL_PAD, :] = jnp.zeros((POOL_PAD, D_POOL), _F32)
        p2[0:POOL_PAD, :] = jnp.zeros((POOL_PAD, D_POOL), _F32)
        p4[0:POOL_PAD, :] = jnp.zeros((POOL_PAD, 3 * POOL_GROUP), _F32)
        p8[0:POOL_PAD, :] = jnp.zeros((POOL_PAD, 2 * POOL_GROUP), _F32)
        cbuf[0:CONV_PAD, :] = jnp.zeros((CONV_PAD, D_REC), _F32)
        hstate[...] = jnp.zeros_like(hstate)
        for buf, fill in ((sa0, 1.0), (sa1, 1.0), (sb0, 0.0), (sb1, 0.0)):
            buf[0:SCAN_PAD, :] = jnp.full((SCAN_PAD, D_REC), fill, _F32)

    rr = TS + 16
    p1[POOL_PAD:, :] = xp
    p2[16:, :] = p1[16:16 + rr, :] + p1[15:15 + rr, :]
    p4[16:, :] = p2[16:16 + rr, POOL_GROUP:] + p2[14:14 + rr, POOL_GROUP:]
    p8[16:, :] = p4[16:16 + rr, POOL_GROUP:] + p4[12:12 + rr, POOL_GROUP:]
    s16 = p8[POOL_PAD:, POOL_GROUP:] + p8[POOL_PAD - 8:POOL_PAD - 8 + TS, POOL_GROUP:]
    sums = (p2[POOL_PAD:, 0:POOL_GROUP], p4[POOL_PAD:, 0:POOL_GROUP],
            p8[POOL_PAD:, 0:POOL_GROUP], s16)
    pos = (s * TS + lax.broadcasted_iota(jnp.int32, (TS, 1), 0) + 1).astype(_F32)
    pooled = []
    for g, w in enumerate(POOL_WINDOWS):
        cnt = jnp.minimum(pos, float(w))
        pooled.append(sums[g] / cnt - xp[:, g * POOL_GROUP:(g + 1) * POOL_GROUP])
    pooled = jnp.concatenate(pooled, axis=1).astype(_BF16)
    y_pool = _dot(pooled, poolw_ref[...]) * pscale_ref[...]
    p1[16:POOL_PAD, :] = p1[TS + 16:TS + POOL_PAD, :]

    cbuf[CONV_PAD:, :] = xr
    xc = convb_ref[...]
    for k in range(CONV_WIDTH):
        off = CONV_PAD - (CONV_WIDTH - 1) + k
        xc = xc + cbuf[off:off + TS, :] * convw_ref[k:k + 1, :]
    cbuf[0:CONV_PAD, :] = cbuf[TS:TS + CONV_PAD, :]
    gates = _dot(xc.astype(_BF16), wg_ref[...]) + bg_ref[...]
    r_gate = jax.nn.sigmoid(gates[:, :D_REC])
    i_gate = jax.nn.sigmoid(gates[:, D_REC:])
    log_a = -LRU_C * r_gate * jax.nn.softplus(-lam_ref[...])
    a = jnp.exp(log_a)
    mult = jnp.sqrt(-jnp.tanh(log_a) * (a * a + 1.0))
    b_in = mult * (i_gate * xc)

    sa0[SCAN_PAD:, :] = a
    sb0[SCAN_PAD:, :] = b_in
    bufs = ((sa0, sb0), (sa1, sb1))
    d = 1
    lvl = 0
    while d < TS:
        sa, sb = bufs[lvl % 2]
        da, db = bufs[(lvl + 1) % 2]
        a_cur = sa[SCAN_PAD:, :]
        a_prev = sa[SCAN_PAD - d:SCAN_PAD - d + TS, :]
        b_cur = sb[SCAN_PAD:, :]
        b_prev = sb[SCAN_PAD - d:SCAN_PAD - d + TS, :]
        da[SCAN_PAD:, :] = a_cur * a_prev
        db[SCAN_PAD:, :] = b_cur + a_cur * b_prev
        d *= 2
        lvl += 1
    fa, fb = bufs[lvl % 2]
    h_rec = fb[SCAN_PAD:, :] + fa[SCAN_PAD:, :] * hstate[0:1, :]
    hstate[0:1, :] = h_rec[TS - 1:TS, :]
    y_rec = h_rec * _gelu(gr)

    y_mix = jnp.concatenate([y_pool, y_rec], axis=1).astype(_BF16)
    y = _dot(y_mix, wout_ref[...]) + bout_ref[...]
    first = jnp.logical_and(b == 0, s == 0)
    _finish_sublayer(x, y, m1_ref, ln_ref, m2_ref, wr_hi_ref, wr_lo_ref, br_ref, tri_ref, first,
                     x1_ref, h2_ref, idx_ref, gate_ref, rank_ref, cnt_ref)


def _block_diag(w):
    n, k, _ = w.shape
    eye = jnp.eye(n, dtype=w.dtype)
    return (eye[:, None, :, None] * w[:, :, None, :]).reshape(n * k, n * k)


def _even_layer(x, span, m1, m2, ln, router, tri, w_in, pool_w, pool_scale, conv_w, conv_b, w_a,
                b_a, w_x, b_x, lam, w_out, b_out):
    xb0, mb0, nb = span
    seq = x.shape[1]
    n_s = seq // TS
    wr_hi, wr_lo, br = router
    wg = jnp.concatenate([_block_diag(w_a), _block_diag(w_x)], axis=1).astype(_BF16)
    bg = jnp.concatenate([b_a, b_x])[None, :]
    ep_in, ep_out = _epilogue_specs(n_s, mb0)
    const = lambda shape: pl.BlockSpec(shape, lambda b, s: (0,) * len(shape))
    in_specs = [
        pl.BlockSpec((None, TS, D), lambda b, s: (b + xb0, s, 0)),
        pl.BlockSpec((None, 3, D), lambda b, s: (b + mb0, 0, 0)),
        const((D, D_IN_AB)), const((D_POOL, D_POOL)), const((1, D_POOL)),
        const((CONV_WIDTH, D_REC)), const((1, D_REC)), const((D_REC, 2 * D_REC)),
        const((1, 2 * D_REC)), const((1, D_REC)), const((D, D)), const((1, D)),
    ] + ep_in
    scratch = [
        pltpu.VMEM((TS + POOL_PAD, D_POOL), _F32),
        pltpu.VMEM((TS + POOL_PAD, D_POOL), _F32),
        pltpu.VMEM((TS + POOL_PAD, 3 * POOL_GROUP), _F32),
        pltpu.VMEM((TS + POOL_PAD, 2 * POOL_GROUP), _F32),
        pltpu.VMEM((TS + CONV_PAD, D_REC), _F32),
        pltpu.VMEM((TS + SCAN_PAD, D_REC), _F32),
        pltpu.VMEM((TS + SCAN_PAD, D_REC), _F32),
        pltpu.VMEM((TS + SCAN_PAD, D_REC), _F32),
        pltpu.VMEM((TS + SCAN_PAD, D_REC), _F32),
        pltpu.VMEM((8, D_REC), _F32),
    ]
    return pl.pallas_call(
        _even_kernel,
        grid=(nb, n_s),
        in_specs=in_specs,
        out_specs=ep_out,
        out_shape=_epilogue_out_shapes(nb, seq),
        scratch_shapes=scratch,
        compiler_params=pltpu.CompilerParams(
            dimension_semantics=("arbitrary", "arbitrary"), vmem_limit_bytes=VMEM_LIMIT),
        name="even_mixer",
    )(x, m1, w_in.astype(_BF16), _block_diag(pool_w).astype(_BF16), pool_scale[None, :],
      conv_w, conv_b[None, :], wg, bg, lam[None, :], w_out.astype(_BF16), b_out[None, :],
      m2, ln, wr_hi, wr_lo, br, tri)


def _odd_kernel(x_ref, m1_ref, win_ref, bin_ref, sln_ref, ws_ref, bs_ref, wout_ref, bout_ref,
                m2_ref, ln_ref, wr_hi_ref, wr_lo_ref, br_ref, tri_ref,
                x1_ref, h2_ref, idx_ref, gate_ref, rank_ref, cnt_ref,
                sv_buf):
    b = pl.program_id(0)
    s = pl.program_id(1)
    x = x_ref[...]
    h = x * (1.0 + m1_ref[1:2, :]) + m1_ref[0:1, :]
    z = _gelu(_dot(h.astype(_BF16), win_ref[...]) + bin_ref[...])
    u = z[:, :D]
    v = _layer_norm(z[:, D:], sln_ref[0:1, :], sln_ref[1:2, :]).astype(_BF16)
    n_chunks = TS // CHUNK
    row = lax.broadcasted_iota(jnp.int32, (CHUNK, CHUNK), 0)
    col = lax.broadcasted_iota(jnp.int32, (CHUNK, CHUNK), 1)
    causal = col <= row
    for hh in range(N_SGU_HEADS):
        lanes = slice(hh * CHUNK, (hh + 1) * CHUNK)
        w_h = jnp.where(causal, ws_ref[hh], 0.0).astype(_BF16)
        v_h = jnp.concatenate(
            [v[n * CHUNK:(n + 1) * CHUNK, lanes] for n in range(n_chunks)], axis=1)
        sv_h = _dot(w_h, v_h)
        for n in range(n_chunks):
            sv_buf[n * CHUNK:(n + 1) * CHUNK, lanes] = (
                sv_h[:, n * CHUNK:(n + 1) * CHUNK] + bs_ref[hh])
    y = _dot((u * sv_buf[...]).astype(_BF16), wout_ref[...]) + bout_ref[...]
    first = jnp.logical_and(b == 0, s == 0)
    _finish_sublayer(x, y, m1_ref, ln_ref, m2_ref, wr_hi_ref, wr_lo_ref, br_ref, tri_ref, first,
                     x1_ref, h2_ref, idx_ref, gate_ref, rank_ref, cnt_ref)


def _odd_layer(x, span, m1, m2, ln, router, tri, w_in, b_in, ln_g, ln_b, w_s, b_s, w_out, b_out):
    xb0, mb0, nb = span
    seq = x.shape[1]
    n_s = seq // TS
    wr_hi, wr_lo, br = router
    sln = jnp.stack([ln_g, ln_b])
    bs = jnp.broadcast_to(b_s[:, :, None], (N_SGU_HEADS, CHUNK, CHUNK))
    ep_in, ep_out = _epilogue_specs(n_s, mb0)
    const = lambda shape: pl.BlockSpec(shape, lambda b, s: (0,) * len(shape))
    in_specs = [
        pl.BlockSpec((None, TS, D), lambda b, s: (b + xb0, s, 0)),
        pl.BlockSpec((None, 3, D), lambda b, s: (b + mb0, 0, 0)),
        const((D, 2 * D)), const((1, 2 * D)), const((2, D)),
        const((N_SGU_HEADS, CHUNK, CHUNK)), const((N_SGU_HEADS, CHUNK, CHUNK)),
        const((D, D)), const((1, D)),
    ] + ep_in
    return pl.pallas_call(
        _odd_kernel,
        grid=(nb, n_s),
        in_specs=in_specs,
        out_specs=ep_out,
        out_shape=_epilogue_out_shapes(nb, seq),
        scratch_shapes=[pltpu.VMEM((TS, D), _F32)],
        compiler_params=pltpu.CompilerParams(
            dimension_semantics=("arbitrary", "arbitrary"), vmem_limit_bytes=VMEM_LIMIT),
        name="odd_mixer",
    )(x, m1, w_in.astype(_BF16), b_in[None, :], sln, w_s, bs, w_out.astype(_BF16),
      b_out[None, :], m2, ln, wr_hi, wr_lo, br, tri)


def _plan_kernel(pstart_ref, idx_ref, rank_ref, dest_ref):
    idx = idx_ref[...]
    dest = rank_ref[...]
    for e in range(N_EXPERTS):
        dest = dest + jnp.where(idx == e, pstart_ref[e], 0)
    dest_ref[...] = dest


def _plan(idx, rank, counts):
    t = idx.shape[1]
    n_blocks = (t * TOP_K) // EXPERT_BLOCK + N_EXPERTS
    cnt = counts[:, 0].astype(jnp.int32)
    padded = ((cnt + EXPERT_BLOCK - 1) // EXPERT_BLOCK) * EXPERT_BLOCK
    pends = jnp.cumsum(padded)
    pstart = pends - padded
    block_start = jnp.arange(n_blocks, dtype=jnp.int32) * EXPERT_BLOCK
    block_e = jnp.minimum(jnp.sum(block_start[:, None] >= pends[None, :], axis=1),
                          N_EXPERTS - 1).astype(jnp.int32)
    n_used = (pends[-1] // EXPERT_BLOCK).astype(jnp.int32)
    n_valid = jnp.clip(pstart[block_e] + cnt[block_e] - block_start, 0, EXPERT_BLOCK)
    n_valid = jnp.where(block_start < pends[-1], n_valid, 0).astype(jnp.int32)
    tl = min(t, 8192)
    dest = pl.pallas_call(
        _plan_kernel,
        grid_spec=pltpu.PrefetchScalarGridSpec(
            num_scalar_prefetch=1,
            grid=(t // tl,),
            in_specs=[pl.BlockSpec((TOP_K, tl), lambda i, p: (0, i)),
                      pl.BlockSpec((TOP_K, tl), lambda i, p: (0, i))],
            out_specs=pl.BlockSpec((TOP_K, tl), lambda i, p: (0, i)),
        ),
        out_shape=jax.ShapeDtypeStruct((TOP_K, t), jnp.int32),
        compiler_params=pltpu.CompilerParams(dimension_semantics=("arbitrary",)),
        name="moe_plan",
    )(pstart.astype(jnp.int32), idx, rank)
    dest3 = dest.reshape(TOP_K, t // SC_CHUNK, SC_CHUNK).transpose(1, 0, 2)
    return dest3, block_e, n_valid, n_used.reshape(1)


def _sc_mesh():
    return plsc.VectorSubcoreMesh(core_axis_name="c", subcore_axis_name="s")


def _sc_dispatch(h2_rows, dest3, cap):
    t = h2_rows.shape[0]
    per_w = (t // SC_CHUNK) // (SC_CORES * SC_SUBCORES)

    @functools.partial(
        pl.kernel, mesh=_sc_mesh(),
        out_type=jax.ShapeDtypeStruct((cap, ROW_TILES, LANES), _F32),
        scratch_types=[pltpu.VMEM((TOP_K, SC_CHUNK), jnp.int32),
                       pltpu.VMEM((SC_CHUNK, ROW_TILES, LANES), _F32)],
        name="moe_dispatch",
    )
    def run(x_hbm, dest_hbm, out_hbm, idx_v, rows_v):
        wid = lax.axis_index("s") * SC_CORES + lax.axis_index("c")

        @pl.loop(0, per_w)
        def _(i):
            c = wid * per_w + i
            pltpu.sync_copy(dest_hbm.at[c], idx_v)
            pltpu.sync_copy(x_hbm.at[pl.ds(c * SC_CHUNK, SC_CHUNK)], rows_v)
            for k in range(TOP_K):
                pltpu.sync_copy(rows_v, out_hbm.at[idx_v.at[k]])

    return run(h2_rows, dest3)


def _sc_combine(ys_rows, dest3, t):
    per_w = (t // SC_CHUNK) // (SC_CORES * SC_SUBCORES)

    @functools.partial(
        pl.kernel, mesh=_sc_mesh(),
        out_type=jax.ShapeDtypeStruct((TOP_K, t, ROW_TILES, LANES), _F32),
        scratch_types=[pltpu.VMEM((TOP_K, SC_CHUNK), jnp.int32),
                       pltpu.VMEM((SC_CHUNK, ROW_TILES, LANES), _F32)],
        name="moe_combine",
    )
    def run(ys_hbm, dest_hbm, out_hbm, idx_v, rows_v):
        wid = lax.axis_index("s") * SC_CORES + lax.axis_index("c")

        @pl.loop(0, per_w)
        def _(i):
            c = wid * per_w + i
            pltpu.sync_copy(dest_hbm.at[c], idx_v)
            for k in range(TOP_K):
                pltpu.sync_copy(ys_hbm.at[idx_v.at[k]], rows_v)
                pltpu.sync_copy(rows_v, out_hbm.at[k, pl.ds(c * SC_CHUNK, SC_CHUNK)])

    return run(ys_rows, dest3)


def _ffn_kernel(be_ref, nv_ref, nu_ref, xs_ref, w1_ref, b1_ref, w2_ref, b2_ref, perm_ref, ys_ref,
                w1_s, w2_s):
    i = pl.program_id(0)
    active = i < nu_ref[0]
    new_expert = jnp.logical_or(i == 0, be_ref[i] != be_ref[jnp.maximum(i - 1, 0)])

    @pl.when(jnp.logical_and(active, new_expert))
    def _():
        for n in range(2 * D_FF // PERM):
            blk = w1_ref[:, n * PERM:(n + 1) * PERM].astype(_BF16)
            moved = _dot(blk, perm_ref[...]).astype(_BF16)
            half = PERM // 2
            w1_s[:, n * half:(n + 1) * half] = moved[:, :half]
            w1_s[:, D_FF + n * half:D_FF + (n + 1) * half] = moved[:, half:]
        w2_s[...] = w2_ref[...].astype(_BF16)

    @pl.when(active)
    def _():
        x = _load_rows(xs_ref, EXPERT_BLOCK)
        rows = lax.broadcasted_iota(jnp.int32, (EXPERT_BLOCK, 1), 0)
        x = jnp.where(rows < nv_ref[i], x, 0.0).astype(_BF16)
        hb = _dot(x, w1_s[...]) + b1_ref[...]
        x_glu = jnp.minimum(hb[:, :D_FF], SWIGLU_LIMIT)
        x_lin = jnp.clip(hb[:, D_FF:], -SWIGLU_LIMIT, SWIGLU_LIMIT)
        act = x_glu * jax.nn.sigmoid(SWIGLU_ALPHA * x_glu) * (x_lin + 1.0)
        y = _dot(act.astype(_BF16), w2_s[...]) + b2_ref[...]
        _store_rows(ys_ref, y)


def _expert_ffn(xs_rows, block_e, n_valid, n_used, w1, b1, w2, b2):
    n_blocks = block_e.shape[0]
    half = PERM // 2
    src = jnp.arange(PERM)
    dst = jnp.where(src % 2 == 0, src // 2, half + src // 2)
    perm = (dst[:, None] == jnp.arange(PERM)[None, :]).astype(_BF16)
    blk = lambda i, be, nv, nu: (jnp.minimum(i, nu[0] - 1), 0)
    wsel = lambda i, be, nv, nu: (be[jnp.minimum(i, nu[0] - 1)], 0, 0)
    return pl.pallas_call(
        _ffn_kernel,
        grid_spec=pltpu.PrefetchScalarGridSpec(
            num_scalar_prefetch=3,
            grid=(n_blocks,),
            in_specs=[
                pl.BlockSpec((EXPERT_BLOCK * ROW_TILES, LANES), blk),
                pl.BlockSpec((None, D, 2 * D_FF), wsel),
                pl.BlockSpec((None, 1, 2 * D_FF), wsel),
                pl.BlockSpec((None, D_FF, D), wsel),
                pl.BlockSpec((None, 1, D), wsel),
                pl.BlockSpec((PERM, PERM), lambda i, be, nv, nu: (0, 0)),
            ],
            out_specs=pl.BlockSpec((EXPERT_BLOCK * ROW_TILES, LANES), blk),
            scratch_shapes=[pltpu.VMEM((D, 2 * D_FF), _BF16), pltpu.VMEM((D_FF, D), _BF16)],
        ),
        out_shape=jax.ShapeDtypeStruct(xs_rows.shape, _F32),
        compiler_params=pltpu.CompilerParams(
            dimension_semantics=("arbitrary",), vmem_limit_bytes=VMEM_LIMIT),
        name="expert_ffn",
    )(block_e, n_valid, n_used, xs_rows, w1, b1, w2, b2, perm)


def _post_kernel(x1_ref, yg_ref, gate_ref, m2_ref, ln_ref, o_ref):
    gt = gate_ref[...]
    pad = jnp.zeros((LANES - TOP_K, TS), _F32)
    g_rows = jnp.concatenate([gt, pad], axis=0).T
    y = jnp.zeros((TS, D), _F32)
    for k in range(TOP_K):
        y = y + g_rows[:, k:k + 1] * _load_rows(yg_ref.at[k], TS)
    gate2 = 1.0 + m2_ref[2:3, :]
    o_ref[...] = _layer_norm(ALPHA * x1_ref[...] + gate2 * y, ln_ref[0:1, :], ln_ref[1:2, :])


def _moe_post(x1, yg, gates, m2, ln, mb0):
    nb, seq, _ = x1.shape
    n_s = seq // TS
    tile = lambda b, s: b * n_s + s
    return pl.pallas_call(
        _post_kernel,
        grid=(nb, n_s),
        in_specs=[
            pl.BlockSpec((None, TS, D), lambda b, s: (b, s, 0)),
            pl.BlockSpec((TOP_K, TS * ROW_TILES, LANES), lambda b, s: (0, tile(b, s), 0)),
            pl.BlockSpec((TOP_K, TS), lambda b, s: (0, tile(b, s))),
            pl.BlockSpec((None, 3, D), lambda b, s: (b + mb0, 0, 0)),
            pl.BlockSpec((2, D), lambda b, s: (0, 0)),
        ],
        out_specs=pl.BlockSpec((None, TS, D), lambda b, s: (b, s, 0)),
        out_shape=jax.ShapeDtypeStruct((nb, seq, D), _F32),
        compiler_params=pltpu.CompilerParams(
            dimension_semantics=("arbitrary", "arbitrary"), vmem_limit_bytes=VMEM_LIMIT),
        name="moe_post",
    )(x1, yg, gates, m2, ln)


def _moe(x1, h2_rows, idx, gates, rank, counts, m2, ln, mb0, w1, b1, w2, b2):
    nb, seq, _ = x1.shape
    t = nb * seq
    cap = (t * TOP_K // EXPERT_BLOCK + N_EXPERTS) * EXPERT_BLOCK
    dest3, block_e, n_valid, n_used = _plan(idx, rank, counts)
    xs = _sc_dispatch(h2_rows.reshape(t, ROW_TILES, LANES), dest3, cap)
    ys = _expert_ffn(xs.reshape(cap * ROW_TILES, LANES), block_e, n_valid, n_used, w1, b1, w2, b2)
    yg = _sc_combine(ys.reshape(cap, ROW_TILES, LANES), dest3, t)
    return _moe_post(x1, yg.reshape(TOP_K, t * ROW_TILES, LANES), gates, m2, ln, mb0)


def kernel(x, c, mod_w, mod_b, ln_g, ln_b, ab_w_in, pool_w, pool_scale, conv_w, conv_b, lru_w_a, lru_b_a, lru_w_x, lru_b_x, lru_lambda, ab_w_out, ab_b_out, sgu_w_in, sgu_b_in, sgu_ln_g, sgu_ln_b, sgu_w_s, sgu_b_s, sgu_w_out, sgu_b_out, router_w, router_b, moe_w1, moe_b1, moe_w2, moe_b2):
    assert x.shape[1] % TS == 0 and x.shape[2] == D
    mods = _modulation(c, mod_w, mod_b)
    tri = (lax.broadcasted_iota(jnp.int32, (TS, TS), 0)
           < lax.broadcasted_iota(jnp.int32, (TS, TS), 1)).astype(_BF16)
    b1 = jnp.concatenate([moe_b1[..., 0::2], moe_b1[..., 1::2]], axis=-1)[:, :, None, :]
    b2 = moe_b2[:, :, None, :]
    nb = x.shape[0]
    n_groups = 2 if nb % 2 == 0 else 1
    gb = nb // n_groups
    xs = [x] * n_groups
    for layer in range(DEPTH):
        j = layer // 2
        m1, m2 = mods[layer, 0], mods[layer, 1]
        ln1 = jnp.stack([ln_g[layer, 0], ln_b[layer, 0]])
        ln2 = jnp.stack([ln_g[layer, 1], ln_b[layer, 1]])
        wr_hi, wr_lo = _split(router_w[layer].T)
        router = (wr_hi, wr_lo, router_b[layer][:, None])
        for g in range(n_groups):
            span = (g * gb if layer == 0 else 0, g * gb, gb)
            if layer % 2 == 0:
                outs = _even_layer(xs[g], span, m1, m2, ln1, router, tri, ab_w_in[j], pool_w[j],
                                   pool_scale[j], conv_w[j], conv_b[j], lru_w_a[j], lru_b_a[j],
                                   lru_w_x[j], lru_b_x[j], lru_lambda[j], ab_w_out[j], ab_b_out[j])
            else:
                outs = _odd_layer(xs[g], span, m1, m2, ln1, router, tri, sgu_w_in[j], sgu_b_in[j],
                                  sgu_ln_g[j], sgu_ln_b[j], sgu_w_s[j], sgu_b_s[j], sgu_w_out[j],
                                  sgu_b_out[j])
            x1, h2_rows, idx, gates, rank, counts = outs
            xs[g] = _moe(x1, h2_rows, idx, gates, rank, counts, m2, ln2, g * gb,
                         moe_w1[layer], b1[layer], moe_w2[layer], b2[layer])
    return jnp.concatenate(xs, axis=0) if n_groups > 1 else xs[0]
```

```python
import functools

import jax
import jax.numpy as jnp
from jax import lax
from jax.experimental import pallas as pl
from jax.experimental.pallas import tpu as pltpu
from jax.experimental.pallas import tpu_sc as plsc

D = 1024
DEPTH = 4
POOL_WINDOWS = (2, 4, 8, 16)
POOL_GROUP = 128
D_POOL = 512
D_REC = 512
N_REC_HEADS = 8
REC_HEAD = 64
CONV_WIDTH = 4
LRU_C = 8.0
D_IN_AB = 1536
CHUNK = 128
N_SGU_HEADS = 8
N_EXPERTS = 32
TOP_K = 4
D_FF = 1024
SWIGLU_LIMIT = 7.0
SWIGLU_ALPHA = 1.702
EXPERT_BLOCK = 512
LN_EPS = 1e-5
ALPHA = (2 * DEPTH) ** 0.25

TS = 512
LANES = 128
PACK_TILES = D // (2 * LANES)
_ROWS = jnp.uint32
POOL_PAD = 32
CONV_PAD = 8
SCAN_PAD = TS // 2
SC_CORES = 2
SC_SUBCORES = 16
SC_CHUNK = 128
PERM = 256
VMEM_LIMIT = 56 * 1024 * 1024

_F32 = jnp.float32
_BF16 = jnp.bfloat16
_NT = (((1,), (1,)), ((), ()))


def _split(a):
    hi = a.astype(_BF16)
    lo = (a - hi.astype(_F32)).astype(_BF16)
    return hi, lo


def _dot(a, b):
    return jnp.dot(a, b, preferred_element_type=_F32)


def _layer_norm(x, g, b):
    mu = jnp.mean(x, axis=-1, keepdims=True)
    xc = x - mu
    var = jnp.mean(xc * xc, axis=-1, keepdims=True)
    return xc * lax.rsqrt(var + LN_EPS) * g + b


def _gelu(x):
    return jax.nn.gelu(x)


def _mod_kernel(c_ref, w_ref, b_ref, o_ref):
    c = c_ref[...]
    ca = c * jax.nn.sigmoid(c)
    a_hi, a_lo = _split(ca)
    w_hi, w_lo = _split(w_ref[...])
    o_ref[...] = _dot(a_hi, w_hi) + _dot(a_lo, w_hi) + _dot(a_hi, w_lo) + b_ref[...]


def _modulation(c, mod_w, mod_b):
    nb = c.shape[0]
    n = DEPTH * 2
    w = mod_w.reshape(n, D, 3 * D)
    b = mod_b.reshape(n, 1, 3 * D)
    out = pl.pallas_call(
        _mod_kernel,
        grid=(n, 3),
        in_specs=[
            pl.BlockSpec((nb, D), lambda i, j: (0, 0)),
            pl.BlockSpec((None, D, D), lambda i, j: (i, 0, j)),
            pl.BlockSpec((None, 1, D), lambda i, j: (i, 0, j)),
        ],
        out_specs=pl.BlockSpec((None, nb, D), lambda i, j: (i, 0, j)),
        out_shape=jax.ShapeDtypeStruct((n, nb, 3 * D), _F32),
        compiler_params=pltpu.CompilerParams(
            dimension_semantics=("arbitrary", "arbitrary"), vmem_limit_bytes=VMEM_LIMIT),
        name="modulation",
    )(c, w, b)
    return out.reshape(DEPTH, 2, nb, 3, D)


def _store_rows(ref, val):
    rows = val.shape[0]
    for q in range(PACK_TILES):
        lo = val[:, q * LANES:(q + 1) * LANES].astype(_BF16).astype(_F32)
        hi = val[:, D // 2 + q * LANES:D // 2 + (q + 1) * LANES].astype(_BF16).astype(_F32)
        word = (pltpu.bitcast(lo, jnp.uint32) >> 16) | pltpu.bitcast(hi, jnp.uint32)
        ref[pl.ds(q, rows, stride=PACK_TILES), :] = word


def _load_rows(ref, rows):
    lo, hi = [], []
    for q in range(PACK_TILES):
        word = ref[pl.ds(q, rows, stride=PACK_TILES), :]
        lo.append(pltpu.bitcast(word << 16, _F32))
        hi.append(pltpu.bitcast(word & jnp.uint32(0xFFFF0000), _F32))
    return jnp.concatenate(lo + hi, axis=1)


def _finish_sublayer(x, y, m1_ref, ln_ref, m2_ref, wr_hi_ref, wr_lo_ref, br_ref, tri_ref, first,
                     x1_ref, h2_ref, idx_ref, gate_ref, rank_ref, cnt_ref):
    gate1 = 1.0 + m1_ref[2:3, :]
    x1 = _layer_norm(ALPHA * x + gate1 * y, ln_ref[0:1, :], ln_ref[1:2, :])
    x1_ref[...] = x1
    h2 = x1 * (1.0 + m2_ref[1:2, :]) + m2_ref[0:1, :]
    _store_rows(h2_ref, h2)

    h_hi, h_lo = _split(h2)
    w_hi = wr_hi_ref[...]
    logits = (lax.dot_general(w_hi, h_hi, _NT, preferred_element_type=_F32)
              + lax.dot_general(wr_lo_ref[...], h_hi, _NT, preferred_element_type=_F32)
              + lax.dot_general(w_hi, h_lo, _NT, preferred_element_type=_F32)
              + br_ref[...])
    n_tok = logits.shape[1]
    iota_e = lax.broadcasted_iota(jnp.int32, (N_EXPERTS, n_tok), 0)
    work = logits
    vals, idxs, hots = [], [], []
    for _ in range(TOP_K):
        m = jnp.max(work, axis=0, keepdims=True)
        idx = jnp.min(jnp.where(work == m, iota_e, N_EXPERTS), axis=0, keepdims=True)
        hot = iota_e == idx
        work = jnp.where(hot, -jnp.inf, work)
        vals.append(m)
        idxs.append(idx)
        hots.append(hot)
    exps = [jnp.exp(v - vals[0]) for v in vals]
    denom = exps[0] + exps[1] + exps[2] + exps[3]
    gate_ref[...] = jnp.concatenate([e / denom for e in exps], axis=0)
    idx_ref[...] = jnp.concatenate(idxs, axis=0)

    sel = jnp.zeros((N_EXPERTS, n_tok), _F32)
    for hot in hots:
        sel = sel + jnp.where(hot, 1.0, 0.0)

    @pl.when(first)
    def _():
        cnt_ref[...] = jnp.zeros_like(cnt_ref)

    base = cnt_ref[:, 0:1]
    before = _dot(sel.astype(_BF16), tri_ref[...]) + base
    ranks = [jnp.sum(jnp.where(hot, before, 0.0), axis=0, keepdims=True) for hot in hots]
    rank_ref[...] = jnp.concatenate(ranks, axis=0).astype(jnp.int32)
    cnt_ref[...] = cnt_ref[...] + jnp.sum(sel, axis=1, keepdims=True)


def _epilogue_specs(n_seq_tiles, mb0):
    tile = lambda b, s: b * n_seq_tiles + s
    in_specs = [
        pl.BlockSpec((None, 3, D), lambda b, s: (b + mb0, 0, 0)),
        pl.BlockSpec((2, D), lambda b, s: (0, 0)),
        pl.BlockSpec((N_EXPERTS, D), lambda b, s: (0, 0)),
        pl.BlockSpec((N_EXPERTS, D), lambda b, s: (0, 0)),
        pl.BlockSpec((N_EXPERTS, 1), lambda b, s: (0, 0)),
        pl.BlockSpec((TS, TS), lambda b, s: (0, 0)),
    ]
    out_specs = [
        pl.BlockSpec((None, TS, D), lambda b, s: (b, s, 0)),
        pl.BlockSpec((TS * PACK_TILES, LANES), lambda b, s: (tile(b, s), 0)),
        pl.BlockSpec((TOP_K, TS), lambda b, s: (0, tile(b, s))),
        pl.BlockSpec((TOP_K, TS), lambda b, s: (0, tile(b, s))),
        pl.BlockSpec((TOP_K, TS), lambda b, s: (0, tile(b, s))),
        pl.BlockSpec((N_EXPERTS, LANES), lambda b, s: (0, 0)),
    ]
    return in_specs, out_specs


def _epilogue_out_shapes(nb, seq):
    t = nb * seq
    return [
        jax.ShapeDtypeStruct((nb, seq, D), _F32),
        jax.ShapeDtypeStruct((t * PACK_TILES, LANES), _ROWS),
        jax.ShapeDtypeStruct((TOP_K, t), jnp.int32),
        jax.ShapeDtypeStruct((TOP_K, t), _F32),
        jax.ShapeDtypeStruct((TOP_K, t), jnp.int32),
        jax.ShapeDtypeStruct((N_EXPERTS, LANES), _F32),
    ]


def _even_kernel(x_ref, m1_ref, win_ref, poolw_ref, pscale_ref, convw_ref, convb_ref, wg_ref,
                 bg_ref, lam_ref, wout_ref, bout_ref,
                 m2_ref, ln_ref, wr_hi_ref, wr_lo_ref, br_ref, tri_ref,
                 x1_ref, h2_ref, idx_ref, gate_ref, rank_ref, cnt_ref,
                 p1, p2, p4, p8, cbuf, sa0, sa1, sb0, sb1, hstate):
    b = pl.program_id(0)
    s = pl.program_id(1)
    x = x_ref[...]
    h = x * (1.0 + m1_ref[1:2, :]) + m1_ref[0:1, :]
    z = _dot(h.astype(_BF16), win_ref[...])
    xp = z[:, :D_POOL]
    xr = z[:, D_POOL:D_POOL + D_REC]
    gr = z[:, D_POOL + D_REC:]

    @pl.when(s == 0)
    def _():
        p1[0:POOL_PAD, :] = jnp.zeros((POOL_PAD, D_POOL), _F32)
        p2[0:POOL_PAD, :] = jnp.zeros((POOL_PAD, D_POOL), _F32)
        p4[0:POOL_PAD, :] = jnp.zeros((POOL_PAD, 3 * POOL_GROUP), _F32)
        p8[0:POOL_PAD, :] = jnp.zeros((POOL_PAD, 2 * POOL_GROUP), _F32)
        cbuf[0:CONV_PAD, :] = jnp.zeros((CONV_PAD, D_REC), _F32)
        hstate[...] = jnp.zeros_like(hstate)
        for buf, fill in ((sa0, 1.0), (sa1, 1.0), (sb0, 0.0), (sb1, 0.0)):
            buf[0:SCAN_PAD, :] = jnp.full((SCAN_PAD, D_REC), fill, _F32)

    rr = TS + 16
    p1[POOL_PAD:, :] = xp
    p2[16:, :] = p1[16:16 + rr, :] + p1[15:15 + rr, :]
    p4[16:, :] = p2[16:16 + rr, POOL_GROUP:] + p2[14:14 + rr, POOL_GROUP:]
    p8[16:, :] = p4[16:16 + rr, POOL_GROUP:] + p4[12:12 + rr, POOL_GROUP:]
    s16 = p8[POOL_PAD:, POOL_GROUP:] + p8[POOL_PAD - 8:POOL_PAD - 8 + TS, POOL_GROUP:]
    sums = (p2[POOL_PAD:, 0:POOL_GROUP], p4[POOL_PAD:, 0:POOL_GROUP],
            p8[POOL_PAD:, 0:POOL_GROUP], s16)
    pos = (s * TS + lax.broadcasted_iota(jnp.int32, (TS, 1), 0) + 1).astype(_F32)
    pooled = []
    for g, w in enumerate(POOL_WINDOWS):
        cnt = jnp.minimum(pos, float(w))
        pooled.append(sums[g] / cnt - xp[:, g * POOL_GROUP:(g + 1) * POOL_GROUP])
    pooled = jnp.concatenate(pooled, axis=1).astype(_BF16)
    y_pool = _dot(pooled, poolw_ref[...]) * pscale_ref[...]
    p1[16:POOL_PAD, :] = p1[TS + 16:TS + POOL_PAD, :]

    cbuf[CONV_PAD:, :] = xr
    xc = convb_ref[...]
    for k in range(CONV_WIDTH):
        off = CONV_PAD - (CONV_WIDTH - 1) + k
        xc = xc + cbuf[off:off + TS, :] * convw_ref[k:k + 1, :]
    cbuf[0:CONV_PAD, :] = cbuf[TS:TS + CONV_PAD, :]
    gates = _dot(xc.astype(_BF16), wg_ref[...]) + bg_ref[...]
    r_gate = jax.nn.sigmoid(gates[:, :D_REC])
    i_gate = jax.nn.sigmoid(gates[:, D_REC:])
    log_a = -LRU_C * r_gate * jax.nn.softplus(-lam_ref[...])
    a = jnp.exp(log_a)
    mult = jnp.sqrt(-jnp.tanh(log_a) * (a * a + 1.0))
    b_in = mult * (i_gate * xc)

    sa0[SCAN_PAD:, :] = a
    sb0[SCAN_PAD:, :] = b_in
    bufs = ((sa0, sb0), (sa1, sb1))
    d = 1
    lvl = 0
    while d < TS:
        sa, sb = bufs[lvl % 2]
        da, db = bufs[(lvl + 1) % 2]
        a_cur = sa[SCAN_PAD:, :]
        a_prev = sa[SCAN_PAD - d:SCAN_PAD - d + TS, :]
        b_cur = sb[SCAN_PAD:, :]
        b_prev = sb[SCAN_PAD - d:SCAN_PAD - d + TS, :]
        da[SCAN_PAD:, :] = a_cur * a_prev
        db[SCAN_PAD:, :] = b_cur + a_cur * b_prev
        d *= 2
        lvl += 1
    fa, fb = bufs[lvl % 2]
    h_rec = fb[SCAN_PAD:, :] + fa[SCAN_PAD:, :] * hstate[0:1, :]
    hstate[0:1, :] = h_rec[TS - 1:TS, :]
    y_rec = h_rec * _gelu(gr)

    y_mix = jnp.concatenate([y_pool, y_rec], axis=1).astype(_BF16)
    y = _dot(y_mix, wout_ref[...]) + bout_ref[...]
    first = jnp.logical_and(b == 0, s == 0)
    _finish_sublayer(x, y, m1_ref, ln_ref, m2_ref, wr_hi_ref, wr_lo_ref, br_ref, tri_ref, first,
                     x1_ref, h2_ref, idx_ref, gate_ref, rank_ref, cnt_ref)


def _block_diag(w):
    n, k, _ = w.shape
    eye = jnp.eye(n, dtype=w.dtype)
    return (eye[:, None, :, None] * w[:, :, None, :]).reshape(n * k, n * k)


def _even_layer(x, span, m1, m2, ln, router, tri, w_in, pool_w, pool_scale, conv_w, conv_b, w_a,
                b_a, w_x, b_x, lam, w_out, b_out):
    xb0, mb0, nb = span
    seq = x.shape[1]
    n_s = seq // TS
    wr_hi, wr_lo, br = router
    wg = jnp.concatenate([_block_diag(w_a), _block_diag(w_x)], axis=1).astype(_BF16)
    bg = jnp.concatenate([b_a, b_x])[None, :]
    ep_in, ep_out = _epilogue_specs(n_s, mb0)
    const = lambda shape: pl.BlockSpec(shape, lambda b, s: (0,) * len(shape))
    in_specs = [
        pl.BlockSpec((None, TS, D), lambda b, s: (b + xb0, s, 0)),
        pl.BlockSpec((None, 3, D), lambda b, s: (b + mb0, 0, 0)),
        const((D, D_IN_AB)), const((D_POOL, D_POOL)), const((1, D_POOL)),
        const((CONV_WIDTH, D_REC)), const((1, D_REC)), const((D_REC, 2 * D_REC)),
        const((1, 2 * D_REC)), const((1, D_REC)), const((D, D)), const((1, D)),
    ] + ep_in
    scratch = [
        pltpu.VMEM((TS + POOL_PAD, D_POOL), _F32),
        pltpu.VMEM((TS + POOL_PAD, D_POOL), _F32),
        pltpu.VMEM((TS + POOL_PAD, 3 * POOL_GROUP), _F32),
        pltpu.VMEM((TS + POOL_PAD, 2 * POOL_GROUP), _F32),
        pltpu.VMEM((TS + CONV_PAD, D_REC), _F32),
        pltpu.VMEM((TS + SCAN_PAD, D_REC), _F32),
        pltpu.VMEM((TS + SCAN_PAD, D_REC), _F32),
        pltpu.VMEM((TS + SCAN_PAD, D_REC), _F32),
        pltpu.VMEM((TS + SCAN_PAD, D_REC), _F32),
        pltpu.VMEM((8, D_REC), _F32),
    ]
    return pl.pallas_call(
        _even_kernel,
        grid=(nb, n_s),
        in_specs=in_specs,
        out_specs=ep_out,
        out_shape=_epilogue_out_shapes(nb, seq),
        scratch_shapes=scratch,
        compiler_params=pltpu.CompilerParams(
            dimension_semantics=("arbitrary", "arbitrary"), vmem_limit_bytes=VMEM_LIMIT),
        name="even_mixer",
    )(x, m1, w_in.astype(_BF16), _block_diag(pool_w).astype(_BF16), pool_scale[None, :],
      conv_w, conv_b[None, :], wg, bg, lam[None, :], w_out.astype(_BF16), b_out[None, :],
      m2, ln, wr_hi, wr_lo, br, tri)


def _odd_kernel(x_ref, m1_ref, win_ref, bin_ref, sln_ref, ws_ref, bs_ref, wout_ref, bout_ref,
                m2_ref, ln_ref, wr_hi_ref, wr_lo_ref, br_ref, tri_ref,
                x1_ref, h2_ref, idx_ref, gate_ref, rank_ref, cnt_ref,
                sv_buf):
    b = pl.program_id(0)
    s = pl.program_id(1)
    x = x_ref[...]
    h = x * (1.0 + m1_ref[1:2, :]) + m1_ref[0:1, :]
    z = _gelu(_dot(h.astype(_BF16), win_ref[...]) + bin_ref[...])
    u = z[:, :D]
    v = _layer_norm(z[:, D:], sln_ref[0:1, :], sln_ref[1:2, :]).astype(_BF16)
    n_chunks = TS // CHUNK
    row = lax.broadcasted_iota(jnp.int32, (CHUNK, CHUNK), 0)
    col = lax.broadcasted_iota(jnp.int32, (CHUNK, CHUNK), 1)
    causal = col <= row
    for hh in range(N_SGU_HEADS):
        lanes = slice(hh * CHUNK, (hh + 1) * CHUNK)
        w_h = jnp.where(causal, ws_ref[hh], 0.0).astype(_BF16)
        v_h = jnp.concatenate(
            [v[n * CHUNK:(n + 1) * CHUNK, lanes] for n in range(n_chunks)], axis=1)
        sv_h = _dot(w_h, v_h)
        for n in range(n_chunks):
            sv_buf[n * CHUNK:(n + 1) * CHUNK, lanes] = (
                sv_h[:, n * CHUNK:(n + 1) * CHUNK] + bs_ref[hh])
    y = _dot((u * sv_buf[...]).astype(_BF16), wout_ref[...]) + bout_ref[...]
    first = jnp.logical_and(b == 0, s == 0)
    _finish_sublayer(x, y, m1_ref, ln_ref, m2_ref, wr_hi_ref, wr_lo_ref, br_ref, tri_ref, first,
                     x1_ref, h2_ref, idx_ref, gate_ref, rank_ref, cnt_ref)


def _odd_layer(x, span, m1, m2, ln, router, tri, w_in, b_in, ln_g, ln_b, w_s, b_s, w_out, b_out):
    xb0, mb0, nb = span
    seq = x.shape[1]
    n_s = seq // TS
    wr_hi, wr_lo, br = router
    sln = jnp.stack([ln_g, ln_b])
    bs = jnp.broadcast_to(b_s[:, :, None], (N_SGU_HEADS, CHUNK, CHUNK))
    ep_in, ep_out = _epilogue_specs(n_s, mb0)
    const = lambda shape: pl.BlockSpec(shape, lambda b, s: (0,) * len(shape))
    in_specs = [
        pl.BlockSpec((None, TS, D), lambda b, s: (b + xb0, s, 0)),
        pl.BlockSpec((None, 3, D), lambda b, s: (b + mb0, 0, 0)),
        const((D, 2 * D)), const((1, 2 * D)), const((2, D)),
        const((N_SGU_HEADS, CHUNK, CHUNK)), const((N_SGU_HEADS, CHUNK, CHUNK)),
        const((D, D)), const((1, D)),
    ] + ep_in
    return pl.pallas_call(
        _odd_kernel,
        grid=(nb, n_s),
        in_specs=in_specs,
        out_specs=ep_out,
        out_shape=_epilogue_out_shapes(nb, seq),
        scratch_shapes=[pltpu.VMEM((TS, D), _F32)],
        compiler_params=pltpu.CompilerParams(
            dimension_semantics=("arbitrary", "arbitrary"), vmem_limit_bytes=VMEM_LIMIT),
        name="odd_mixer",
    )(x, m1, w_in.astype(_BF16), b_in[None, :], sln, w_s, bs, w_out.astype(_BF16),
      b_out[None, :], m2, ln, wr_hi, wr_lo, br, tri)


def _plan_kernel(pstart_ref, idx_ref, rank_ref, dest_ref):
    idx = idx_ref[...]
    dest = rank_ref[...]
    for e in range(N_EXPERTS):
        dest = dest + jnp.where(idx == e, pstart_ref[e], 0)
    dest_ref[...] = dest


def _plan(idx, rank, counts):
    t = idx.shape[1]
    n_blocks = (t * TOP_K) // EXPERT_BLOCK + N_EXPERTS
    cnt = counts[:, 0].astype(jnp.int32)
    padded = ((cnt + EXPERT_BLOCK - 1) // EXPERT_BLOCK) * EXPERT_BLOCK
    pends = jnp.cumsum(padded)
    pstart = pends - padded
    block_start = jnp.arange(n_blocks, dtype=jnp.int32) * EXPERT_BLOCK
    block_e = jnp.minimum(jnp.sum(block_start[:, None] >= pends[None, :], axis=1),
                          N_EXPERTS - 1).astype(jnp.int32)
    n_used = (pends[-1] // EXPERT_BLOCK).astype(jnp.int32)
    n_valid = jnp.clip(pstart[block_e] + cnt[block_e] - block_start, 0, EXPERT_BLOCK)
    n_valid = jnp.where(block_start < pends[-1], n_valid, 0).astype(jnp.int32)
    experts = jnp.arange(N_EXPERTS, dtype=jnp.int32)
    later = jnp.logical_and(experts[None, :] > experts[:, None], (cnt > 0)[None, :])
    nxt = jnp.min(jnp.where(later, experts[None, :], N_EXPERTS), axis=1)
    next_e = jnp.where(nxt < N_EXPERTS, nxt, -1).astype(jnp.int32)[block_e]
    tl = min(t, 8192)
    dest = pl.pallas_call(
        _plan_kernel,
        grid_spec=pltpu.PrefetchScalarGridSpec(
            num_scalar_prefetch=1,
            grid=(t // tl,),
            in_specs=[pl.BlockSpec((TOP_K, tl), lambda i, p: (0, i)),
                      pl.BlockSpec((TOP_K, tl), lambda i, p: (0, i))],
            out_specs=pl.BlockSpec((TOP_K, tl), lambda i, p: (0, i)),
        ),
        out_shape=jax.ShapeDtypeStruct((TOP_K, t), jnp.int32),
        compiler_params=pltpu.CompilerParams(dimension_semantics=("arbitrary",)),
        name="moe_plan",
    )(pstart.astype(jnp.int32), idx, rank)
    dest3 = dest.reshape(TOP_K, t // SC_CHUNK, SC_CHUNK).transpose(1, 0, 2)
    return dest3, (block_e, n_valid, n_used.reshape(1), next_e)


def _sc_mesh():
    return plsc.VectorSubcoreMesh(core_axis_name="c", subcore_axis_name="s")


def _sc_dispatch(h2_rows, dest3, cap):
    t = h2_rows.shape[0]
    per_w = (t // SC_CHUNK) // (SC_CORES * SC_SUBCORES)

    @functools.partial(
        pl.kernel, mesh=_sc_mesh(),
        out_type=jax.ShapeDtypeStruct((cap, PACK_TILES, LANES), _ROWS),
        scratch_types=[pltpu.VMEM((TOP_K, SC_CHUNK), jnp.int32),
                       pltpu.VMEM((SC_CHUNK, PACK_TILES, LANES), _ROWS)],
        name="moe_dispatch",
    )
    def run(x_hbm, dest_hbm, out_hbm, idx_v, rows_v):
        wid = lax.axis_index("s") * SC_CORES + lax.axis_index("c")

        @pl.loop(0, per_w)
        def _(i):
            c = wid * per_w + i
            pltpu.sync_copy(dest_hbm.at[c], idx_v)
            pltpu.sync_copy(x_hbm.at[pl.ds(c * SC_CHUNK, SC_CHUNK)], rows_v)
            for k in range(TOP_K):
                pltpu.sync_copy(rows_v, out_hbm.at[idx_v.at[k]])

    return run(h2_rows, dest3)


def _sc_combine(ys_rows, dest3, t):
    per_w = (t // SC_CHUNK) // (SC_CORES * SC_SUBCORES)

    @functools.partial(
        pl.kernel, mesh=_sc_mesh(),
        out_type=jax.ShapeDtypeStruct((TOP_K, t, PACK_TILES, LANES), _ROWS),
        scratch_types=[pltpu.VMEM((TOP_K, SC_CHUNK), jnp.int32),
                       pltpu.VMEM((SC_CHUNK, PACK_TILES, LANES), _ROWS)],
        name="moe_combine",
    )
    def run(ys_hbm, dest_hbm, out_hbm, idx_v, rows_v):
        wid = lax.axis_index("s") * SC_CORES + lax.axis_index("c")

        @pl.loop(0, per_w)
        def _(i):
            c = wid * per_w + i
            pltpu.sync_copy(dest_hbm.at[c], idx_v)
            for k in range(TOP_K):
                pltpu.sync_copy(ys_hbm.at[idx_v.at[k]], rows_v)
                pltpu.sync_copy(rows_v, out_hbm.at[k, pl.ds(c * SC_CHUNK, SC_CHUNK)])

    return run(ys_rows, dest3)


def _ffn_kernel(be_ref, nv_ref, nu_ref, nx_ref, xs_ref, b1_ref, b2_ref, perm_ref, w1_hbm, w2_hbm,
                dep_ref, ys_ref, w1_f, w2_f, w1_s, w2_s, sems, *, layer):
    del dep_ref
    i = pl.program_id(0)
    active = i < nu_ref[0]
    e = be_ref[i]
    new_expert = jnp.logical_or(i == 0, e != be_ref[jnp.maximum(i - 1, 0)])

    def weight_copies(expert):
        return (pltpu.make_async_copy(w1_hbm.at[layer, expert], w1_f, sems.at[0]),
                pltpu.make_async_copy(w2_hbm.at[layer, expert], w2_f, sems.at[1]))

    @pl.when(jnp.logical_and(active, i == 0))
    def _():
        for cp in weight_copies(e):
            cp.start()

    @pl.when(jnp.logical_and(active, new_expert))
    def _():
        for cp in weight_copies(e):
            cp.wait()
        half = PERM // 2
        for n in range(2 * D_FF // PERM):
            blk = w1_f[:, n * PERM:(n + 1) * PERM].astype(_BF16)
            moved = _dot(blk, perm_ref[...]).astype(_BF16)
            w1_s[:, n * half:(n + 1) * half] = moved[:, :half]
            w1_s[:, D_FF + n * half:D_FF + (n + 1) * half] = moved[:, half:]
        w2_s[...] = w2_f[...].astype(_BF16)

        @pl.when(nx_ref[i] >= 0)
        def _():
            for cp in weight_copies(nx_ref[i]):
                cp.start()

    @pl.when(active)
    def _():
        x = _load_rows(xs_ref, EXPERT_BLOCK)
        rows = lax.broadcasted_iota(jnp.int32, (EXPERT_BLOCK, 1), 0)
        x = jnp.where(rows < nv_ref[i], x, 0.0).astype(_BF16)
        hb = _dot(x, w1_s[...]) + b1_ref[...]
        x_glu = jnp.minimum(hb[:, :D_FF], SWIGLU_LIMIT)
        x_lin = jnp.clip(hb[:, D_FF:], -SWIGLU_LIMIT, SWIGLU_LIMIT)
        act = x_glu * jax.nn.sigmoid(SWIGLU_ALPHA * x_glu) * (x_lin + 1.0)
        y = _dot(act.astype(_BF16), w2_s[...]) + b2_ref[...]
        _store_rows(ys_ref, y)


def _expert_ffn(xs_rows, blocks, w1, b1, w2, b2, layer, dep):
    block_e, n_valid, n_used, next_e = blocks
    n_blocks = block_e.shape[0]
    half = PERM // 2
    src = jnp.arange(PERM)
    dst = jnp.where(src % 2 == 0, src // 2, half + src // 2)
    perm = (dst[:, None] == jnp.arange(PERM)[None, :]).astype(_BF16)
    blk = lambda i, be, nv, nu, nx: (jnp.minimum(i, nu[0] - 1), 0)
    bsel = lambda i, be, nv, nu, nx: (layer, be[jnp.minimum(i, nu[0] - 1)], 0, 0)
    return pl.pallas_call(
        functools.partial(_ffn_kernel, layer=layer),
        grid_spec=pltpu.PrefetchScalarGridSpec(
            num_scalar_prefetch=4,
            grid=(n_blocks,),
            in_specs=[
                pl.BlockSpec((EXPERT_BLOCK * PACK_TILES, LANES), blk),
                pl.BlockSpec((None, None, 1, 2 * D_FF), bsel),
                pl.BlockSpec((None, None, 1, D), bsel),
                pl.BlockSpec((PERM, PERM), lambda i, be, nv, nu, nx: (0, 0)),
                pl.BlockSpec(memory_space=pl.ANY),
                pl.BlockSpec(memory_space=pl.ANY),
                pl.BlockSpec(memory_space=pl.ANY),
            ],
            out_specs=pl.BlockSpec((EXPERT_BLOCK * PACK_TILES, LANES), blk),
            scratch_shapes=[
                pltpu.VMEM((D, 2 * D_FF), _F32), pltpu.VMEM((D_FF, D), _F32),
                pltpu.VMEM((D, 2 * D_FF), _BF16), pltpu.VMEM((D_FF, D), _BF16),
                pltpu.SemaphoreType.DMA((2,)),
            ],
        ),
        out_shape=jax.ShapeDtypeStruct(xs_rows.shape, _ROWS),
        compiler_params=pltpu.CompilerParams(
            dimension_semantics=("arbitrary",), vmem_limit_bytes=VMEM_LIMIT),
        name="expert_ffn",
    )(block_e, n_valid, n_used, next_e, xs_rows, b1, b2, perm, w1, w2, dep)


def _post_kernel(x1_ref, yg_ref, gate_ref, m2_ref, ln_ref, dep_ref, o_ref):
    del dep_ref
    gt = gate_ref[...]
    pad = jnp.zeros((LANES - TOP_K, TS), _F32)
    g_rows = jnp.concatenate([gt, pad], axis=0).T
    y = jnp.zeros((TS, D), _F32)
    for k in range(TOP_K):
        y = y + g_rows[:, k:k + 1] * _load_rows(yg_ref.at[k], TS)
    gate2 = 1.0 + m2_ref[2:3, :]
    o_ref[...] = _layer_norm(ALPHA * x1_ref[...] + gate2 * y, ln_ref[0:1, :], ln_ref[1:2, :])


def _moe_post(x1, yg, gates, m2, ln, mb0, dep):
    nb, seq, _ = x1.shape
    n_s = seq // TS
    tile = lambda b, s: b * n_s + s
    return pl.pallas_call(
        _post_kernel,
        grid=(nb, n_s),
        in_specs=[
            pl.BlockSpec((None, TS, D), lambda b, s: (b, s, 0)),
            pl.BlockSpec((TOP_K, TS * PACK_TILES, LANES), lambda b, s: (0, tile(b, s), 0)),
            pl.BlockSpec((TOP_K, TS), lambda b, s: (0, tile(b, s))),
            pl.BlockSpec((None, 3, D), lambda b, s: (b + mb0, 0, 0)),
            pl.BlockSpec((2, D), lambda b, s: (0, 0)),
            pl.BlockSpec(memory_space=pl.ANY),
        ],
        out_specs=pl.BlockSpec((None, TS, D), lambda b, s: (b, s, 0)),
        out_shape=jax.ShapeDtypeStruct((nb, seq, D), _F32),
        compiler_params=pltpu.CompilerParams(
            dimension_semantics=("arbitrary", "arbitrary"), vmem_limit_bytes=VMEM_LIMIT),
        name="moe_post",
    )(x1, yg, gates, m2, ln, dep)


def _moe_layer(mixed, m2, ln, gb, w1, b1, w2, b2, layer):
    n = len(mixed)
    nb, seq, _ = mixed[0][0].shape
    t = nb * seq
    cap = (t * TOP_K // EXPERT_BLOCK + N_EXPERTS) * EXPERT_BLOCK
    dest3s, yss = [], []
    staged = []
    for x1, h2_rows, idx, gates, rank, counts in mixed:
        dest3, blocks = _plan(idx, rank, counts)
        xs = _sc_dispatch(h2_rows.reshape(t, PACK_TILES, LANES), dest3, cap)
        staged.append((xs, blocks))
        dest3s.append(dest3)
    for g, (xs, blocks) in enumerate(staged):
        dep = mixed[n - 1][5] if g == 0 else yss[g - 1]
        yss.append(_expert_ffn(xs.reshape(cap * PACK_TILES, LANES), blocks, w1, b1, w2, b2,
                               layer, dep))
    ygs = [_sc_combine(ys.reshape(cap, PACK_TILES, LANES), dest3, t)
           for ys, dest3 in zip(yss, dest3s)]
    outs = []
    for g, yg in enumerate(ygs):
        dep = yss[n - 1] if g == 0 else outs[g - 1]
        outs.append(_moe_post(mixed[g][0], yg.reshape(TOP_K, t * PACK_TILES, LANES), mixed[g][3],
                              m2, ln, g * gb, dep))
    return outs


def kernel(x, c, mod_w, mod_b, ln_g, ln_b, ab_w_in, pool_w, pool_scale, conv_w, conv_b, lru_w_a, lru_b_a, lru_w_x, lru_b_x, lru_lambda, ab_w_out, ab_b_out, sgu_w_in, sgu_b_in, sgu_ln_g, sgu_ln_b, sgu_w_s, sgu_b_s, sgu_w_out, sgu_b_out, router_w, router_b, moe_w1, moe_b1, moe_w2, moe_b2):
    assert x.shape[1] % TS == 0 and x.shape[2] == D
    mods = _modulation(c, mod_w, mod_b)
    tri = (lax.broadcasted_iota(jnp.int32, (TS, TS), 0)
           < lax.broadcasted_iota(jnp.int32, (TS, TS), 1)).astype(_BF16)
    b1 = jnp.concatenate([moe_b1[..., 0::2], moe_b1[..., 1::2]], axis=-1)[:, :, None, :]
    b2 = moe_b2[:, :, None, :]
    nb = x.shape[0]
    n_groups = 2 if nb % 2 == 0 else 1
    gb = nb // n_groups
    xs = [x] * n_groups
    for layer in range(DEPTH):
        j = layer // 2
        m1, m2 = mods[layer, 0], mods[layer, 1]
        ln1 = jnp.stack([ln_g[layer, 0], ln_b[layer, 0]])
        ln2 = jnp.stack([ln_g[layer, 1], ln_b[layer, 1]])
        wr_hi, wr_lo = _split(router_w[layer].T)
        router = (wr_hi, wr_lo, router_b[layer][:, None])
        mixed = []
        for g in range(n_groups):
            span = (g * gb if layer == 0 else 0, g * gb, gb)
            if layer % 2 == 0:
                outs = _even_layer(xs[g], span, m1, m2, ln1, router, tri, ab_w_in[j], pool_w[j],
                                   pool_scale[j], conv_w[j], conv_b[j], lru_w_a[j], lru_b_a[j],
                                   lru_w_x[j], lru_b_x[j], lru_lambda[j], ab_w_out[j], ab_b_out[j])
            else:
                outs = _odd_layer(xs[g], span, m1, m2, ln1, router, tri, sgu_w_in[j], sgu_b_in[j],
                                  sgu_ln_g[j], sgu_ln_b[j], sgu_w_s[j], sgu_b_s[j], sgu_w_out[j],
                                  sgu_b_out[j])
            mixed.append(outs)
        xs = _moe_layer(mixed, m2, ln2, gb, moe_w1, b1, moe_w2, b2, layer)
    return jnp.concatenate(xs, axis=0) if n_groups > 1 else xs[0]
```

```python
import functools

import jax
import jax.numpy as jnp
from jax import lax
from jax.experimental import pallas as pl
from jax.experimental.pallas import tpu as pltpu
from jax.experimental.pallas import tpu_sc as plsc

D = 1024
DEPTH = 4
POOL_WINDOWS = (2, 4, 8, 16)
POOL_GROUP = 128
D_POOL = 512
D_REC = 512
N_REC_HEADS = 8
REC_HEAD = 64
CONV_WIDTH = 4
LRU_C = 8.0
D_IN_AB = 1536
CHUNK = 128
N_SGU_HEADS = 8
N_EXPERTS = 32
TOP_K = 4
D_FF = 1024
SWIGLU_LIMIT = 7.0
SWIGLU_ALPHA = 1.702
EXPERT_BLOCK = 512
LN_EPS = 1e-5
ALPHA = (2 * DEPTH) ** 0.25

TS = 512
LANES = 128
PACK_TILES = D // (2 * LANES)
_ROWS = jnp.uint32
POOL_PAD = 32
CONV_PAD = 8
SCAN_PAD = TS // 2
SC_CORES = 2
SC_SUBCORES = 16
SC_CHUNK = 128
PERM = 256
VMEM_LIMIT = 56 * 1024 * 1024

_F32 = jnp.float32
_BF16 = jnp.bfloat16
_NT = (((1,), (1,)), ((), ()))


def _split(a):
    hi = a.astype(_BF16)
    lo = (a - hi.astype(_F32)).astype(_BF16)
    return hi, lo


def _dot(a, b):
    return jnp.dot(a, b, preferred_element_type=_F32)


def _layer_norm(x, g, b):
    mu = jnp.mean(x, axis=-1, keepdims=True)
    xc = x - mu
    var = jnp.mean(xc * xc, axis=-1, keepdims=True)
    return xc * lax.rsqrt(var + LN_EPS) * g + b


def _gelu(x):
    return jax.nn.gelu(x)


def _mod_kernel(c_ref, w_ref, b_ref, o_ref):
    c = c_ref[...]
    ca = c * jax.nn.sigmoid(c)
    a_hi, a_lo = _split(ca)
    w_hi, w_lo = _split(w_ref[...])
    o_ref[...] = _dot(a_hi, w_hi) + _dot(a_lo, w_hi) + _dot(a_hi, w_lo) + b_ref[...]


def _modulation(c, mod_w, mod_b):
    nb = c.shape[0]
    n = DEPTH * 2
    w = mod_w.reshape(n, D, 3 * D)
    b = mod_b.reshape(n, 1, 3 * D)
    out = pl.pallas_call(
        _mod_kernel,
        grid=(n, 3),
        in_specs=[
            pl.BlockSpec((nb, D), lambda i, j: (0, 0)),
            pl.BlockSpec((None, D, D), lambda i, j: (i, 0, j)),
            pl.BlockSpec((None, 1, D), lambda i, j: (i, 0, j)),
        ],
        out_specs=pl.BlockSpec((None, nb, D), lambda i, j: (i, 0, j)),
        out_shape=jax.ShapeDtypeStruct((n, nb, 3 * D), _F32),
        compiler_params=pltpu.CompilerParams(
            dimension_semantics=("arbitrary", "arbitrary"), vmem_limit_bytes=VMEM_LIMIT),
        name="modulation",
    )(c, w, b)
    return out.reshape(DEPTH, 2, nb, 3, D)


def _store_rows(ref, r0, val):
    rows = val.shape[0]
    for q in range(PACK_TILES):
        lo = val[:, q * LANES:(q + 1) * LANES].astype(_BF16).astype(_F32)
        hi = val[:, D // 2 + q * LANES:D // 2 + (q + 1) * LANES].astype(_BF16).astype(_F32)
        word = (pltpu.bitcast(lo, jnp.uint32) >> 16) | pltpu.bitcast(hi, jnp.uint32)
        ref[pl.ds(r0 * PACK_TILES + q, rows, stride=PACK_TILES), :] = word


def _load_rows(ref, rows):
    lo, hi = [], []
    for q in range(PACK_TILES):
        word = ref[pl.ds(q, rows, stride=PACK_TILES), :]
        lo.append(pltpu.bitcast(word << 16, _F32))
        hi.append(pltpu.bitcast(word & jnp.uint32(0xFFFF0000), _F32))
    return jnp.concatenate(lo + hi, axis=1)


def _moe_output(x1_ref, yg_ref, gate_ref, m2_ref, ln_ref):
    gt = gate_ref[...]
    pad = jnp.zeros((LANES - TOP_K, TS), _F32)
    g_rows = jnp.concatenate([gt, pad], axis=0).T
    y = jnp.zeros((TS, D), _F32)
    for k in range(TOP_K):
        y = y + g_rows[:, k:k + 1] * _load_rows(yg_ref.at[k], TS)
    gate2 = 1.0 + m2_ref[2:3, :]
    return _layer_norm(ALPHA * x1_ref[...] + gate2 * y, ln_ref[0:1, :], ln_ref[1:2, :])


def _moe_output_specs(n_seq_tiles, mb0):
    tile = lambda b, s: b * n_seq_tiles + s
    return [
        pl.BlockSpec((None, TS, D), lambda b, s: (b, s, 0)),
        pl.BlockSpec((TOP_K, TS * PACK_TILES, LANES), lambda b, s: (0, tile(b, s), 0)),
        pl.BlockSpec((TOP_K, TS), lambda b, s: (0, tile(b, s))),
        pl.BlockSpec((None, 3, D), lambda b, s: (b + mb0, 0, 0)),
        pl.BlockSpec((2, D), lambda b, s: (0, 0)),
    ]


def _mixer_input(refs, from_moe):
    if from_moe:
        return _moe_output(*refs[:5]), refs[5:]
    return refs[0][...], refs[1:]


def _finish_sublayer(parts, m1_ref, ln_ref, m2_ref, wr_hi_ref, wr_lo_ref, br_ref, tri_ref, first,
                     x1_ref, h2_ref, idx_ref, gate_ref, rank_ref, cnt_ref):
    @pl.when(first)
    def _():
        cnt_ref[...] = jnp.zeros_like(cnt_ref)

    gate1 = 1.0 + m1_ref[2:3, :]
    w_hi = wr_hi_ref[...]
    seen = cnt_ref[:, 0:1]
    for r0, x, y in parts:
        n = x.shape[0]
        x1 = _layer_norm(ALPHA * x + gate1 * y, ln_ref[0:1, :], ln_ref[1:2, :])
        x1_ref[r0:r0 + n, :] = x1
        h2 = x1 * (1.0 + m2_ref[1:2, :]) + m2_ref[0:1, :]
        _store_rows(h2_ref, r0, h2)

        h_hi, h_lo = _split(h2)
        logits = (lax.dot_general(w_hi, h_hi, _NT, preferred_element_type=_F32)
                  + lax.dot_general(wr_lo_ref[...], h_hi, _NT, preferred_element_type=_F32)
                  + lax.dot_general(w_hi, h_lo, _NT, preferred_element_type=_F32)
                  + br_ref[...])
        iota_e = lax.broadcasted_iota(jnp.int32, (N_EXPERTS, n), 0)
        work = logits
        vals, idxs, hots = [], [], []
        for _ in range(TOP_K):
            m = jnp.max(work, axis=0, keepdims=True)
            idx = jnp.min(jnp.where(work == m, iota_e, N_EXPERTS), axis=0, keepdims=True)
            hot = iota_e == idx
            work = jnp.where(hot, -jnp.inf, work)
            vals.append(m)
            idxs.append(idx)
            hots.append(hot)
        exps = [jnp.exp(v - vals[0]) for v in vals]
        denom = exps[0] + exps[1] + exps[2] + exps[3]
        gate_ref[:, r0:r0 + n] = jnp.concatenate([e / denom for e in exps], axis=0)
        idx_ref[:, r0:r0 + n] = jnp.concatenate(idxs, axis=0)

        sel = jnp.zeros((N_EXPERTS, n), _F32)
        for hot in hots:
            sel = sel + jnp.where(hot, 1.0, 0.0)
        before = _dot(sel.astype(_BF16), tri_ref[0:n, 0:n]) + seen
        ranks = [jnp.sum(jnp.where(hot, before, 0.0), axis=0, keepdims=True) for hot in hots]
        rank_ref[:, r0:r0 + n] = jnp.concatenate(ranks, axis=0).astype(jnp.int32)
        seen = seen + jnp.sum(sel, axis=1, keepdims=True)
    cnt_ref[...] = jnp.broadcast_to(seen, cnt_ref.shape)


def _epilogue_specs(n_seq_tiles, mb0):
    tile = lambda b, s: b * n_seq_tiles + s
    in_specs = [
        pl.BlockSpec((None, 3, D), lambda b, s: (b + mb0, 0, 0)),
        pl.BlockSpec((2, D), lambda b, s: (0, 0)),
        pl.BlockSpec((N_EXPERTS, D), lambda b, s: (0, 0)),
        pl.BlockSpec((N_EXPERTS, D), lambda b, s: (0, 0)),
        pl.BlockSpec((N_EXPERTS, 1), lambda b, s: (0, 0)),
        pl.BlockSpec((TS, TS), lambda b, s: (0, 0)),
        pl.BlockSpec(memory_space=pl.ANY),
    ]
    out_specs = [
        pl.BlockSpec((None, TS, D), lambda b, s: (b, s, 0)),
        pl.BlockSpec((TS * PACK_TILES, LANES), lambda b, s: (tile(b, s), 0)),
        pl.BlockSpec((TOP_K, TS), lambda b, s: (0, tile(b, s))),
        pl.BlockSpec((TOP_K, TS), lambda b, s: (0, tile(b, s))),
        pl.BlockSpec((TOP_K, TS), lambda b, s: (0, tile(b, s))),
        pl.BlockSpec((N_EXPERTS, LANES), lambda b, s: (0, 0)),
    ]
    return in_specs, out_specs


def _epilogue_out_shapes(nb, seq):
    t = nb * seq
    return [
        jax.ShapeDtypeStruct((nb, seq, D), _F32),
        jax.ShapeDtypeStruct((t * PACK_TILES, LANES), _ROWS),
        jax.ShapeDtypeStruct((TOP_K, t), jnp.int32),
        jax.ShapeDtypeStruct((TOP_K, t), _F32),
        jax.ShapeDtypeStruct((TOP_K, t), jnp.int32),
        jax.ShapeDtypeStruct((N_EXPERTS, LANES), _F32),
    ]


def _even_kernel(*refs, from_moe):
    x, refs = _mixer_input(refs, from_moe)
    (m1_ref, win_ref, poolw_ref, pscale_ref, convw_ref, convb_ref, wg_ref, bg_ref, lam_ref,
     wout_ref, bout_ref,
     m2_ref, ln_ref, wr_hi_ref, wr_lo_ref, br_ref, tri_ref, dep_ref,
     x1_ref, h2_ref, idx_ref, gate_ref, rank_ref, cnt_ref,
     p1, p2, p4, p8, cbuf, sa0, sa1, sb0, sb1, hstate) = refs
    del dep_ref
    b = pl.program_id(0)
    s = pl.program_id(1)
    h = x * (1.0 + m1_ref[1:2, :]) + m1_ref[0:1, :]
    z = _dot(h.astype(_BF16), win_ref[...])
    xp = z[:, :D_POOL]
    xr = z[:, D_POOL:D_POOL + D_REC]
    gr = z[:, D_POOL + D_REC:]

    @pl.when(s == 0)
    def _():
        p1[0:POOL_PAD, :] = jnp.zeros((POOL_PAD, D_POOL), _F32)
        p2[0:POOL_PAD, :] = jnp.zeros((POOL_PAD, D_POOL), _F32)
        p4[0:POOL_PAD, :] = jnp.zeros((POOL_PAD, 3 * POOL_GROUP), _F32)
        p8[0:POOL_PAD, :] = jnp.zeros((POOL_PAD, 2 * POOL_GROUP), _F32)
        cbuf[0:CONV_PAD, :] = jnp.zeros((CONV_PAD, D_REC), _F32)
        hstate[...] = jnp.zeros_like(hstate)
        for buf, fill in ((sa0, 1.0), (sa1, 1.0), (sb0, 0.0), (sb1, 0.0)):
            buf[0:SCAN_PAD, :] = jnp.full((SCAN_PAD, D_REC), fill, _F32)

    rr = TS + 16
    p1[POOL_PAD:, :] = xp
    p2[16:, :] = p1[16:16 + rr, :] + p1[15:15 + rr, :]
    p4[16:, :] = p2[16:16 + rr, POOL_GROUP:] + p2[14:14 + rr, POOL_GROUP:]
    p8[16:, :] = p4[16:16 + rr, POOL_GROUP:] + p4[12:12 + rr, POOL_GROUP:]
    s16 = p8[POOL_PAD:, POOL_GROUP:] + p8[POOL_PAD - 8:POOL_PAD - 8 + TS, POOL_GROUP:]
    sums = (p2[POOL_PAD:, 0:POOL_GROUP], p4[POOL_PAD:, 0:POOL_GROUP],
            p8[POOL_PAD:, 0:POOL_GROUP], s16)
    pos = (s * TS + lax.broadcasted_iota(jnp.int32, (TS, 1), 0) + 1).astype(_F32)
    pooled = []
    for g, w in enumerate(POOL_WINDOWS):
        cnt = jnp.minimum(pos, float(w))
        pooled.append(sums[g] / cnt - xp[:, g * POOL_GROUP:(g + 1) * POOL_GROUP])
    pooled = jnp.concatenate(pooled, axis=1).astype(_BF16)
    y_pool = _dot(pooled, poolw_ref[...]) * pscale_ref[...]
    p1[16:POOL_PAD, :] = p1[TS + 16:TS + POOL_PAD, :]

    cbuf[CONV_PAD:, :] = xr
    xc = convb_ref[...]
    for k in range(CONV_WIDTH):
        off = CONV_PAD - (CONV_WIDTH - 1) + k
        xc = xc + cbuf[off:off + TS, :] * convw_ref[k:k + 1, :]
    cbuf[0:CONV_PAD, :] = cbuf[TS:TS + CONV_PAD, :]
    gates = _dot(xc.astype(_BF16), wg_ref[...]) + bg_ref[...]
    r_gate = jax.nn.sigmoid(gates[:, :D_REC])
    i_gate = jax.nn.sigmoid(gates[:, D_REC:])
    log_a = -LRU_C * r_gate * jax.nn.softplus(-lam_ref[...])
    a = jnp.exp(log_a)
    mult = jnp.sqrt(-jnp.tanh(log_a) * (a * a + 1.0))
    b_in = mult * (i_gate * xc)

    sa0[SCAN_PAD:, :] = a
    sb0[SCAN_PAD:, :] = b_in
    bufs = ((sa0, sb0), (sa1, sb1))
    d = 1
    lvl = 0
    while d < TS:
        sa, sb = bufs[lvl % 2]
        da, db = bufs[(lvl + 1) % 2]
        a_cur = sa[SCAN_PAD:, :]
        a_prev = sa[SCAN_PAD - d:SCAN_PAD - d + TS, :]
        b_cur = sb[SCAN_PAD:, :]
        b_prev = sb[SCAN_PAD - d:SCAN_PAD - d + TS, :]
        da[SCAN_PAD:, :] = a_cur * a_prev
        db[SCAN_PAD:, :] = b_cur + a_cur * b_prev
        d *= 2
        lvl += 1
    fa, fb = bufs[lvl % 2]
    h_rec = fb[SCAN_PAD:, :] + fa[SCAN_PAD:, :] * hstate[0:1, :]
    hstate[0:1, :] = h_rec[TS - 1:TS, :]
    y_rec = h_rec * _gelu(gr)

    y_mix = jnp.concatenate([y_pool, y_rec], axis=1).astype(_BF16)
    y = _dot(y_mix, wout_ref[...]) + bout_ref[...]
    first = jnp.logical_and(b == 0, s == 0)
    _finish_sublayer([(0, x, y)], m1_ref, ln_ref, m2_ref, wr_hi_ref, wr_lo_ref, br_ref, tri_ref,
                     first, x1_ref, h2_ref, idx_ref, gate_ref, rank_ref, cnt_ref)


def _block_diag(w):
    n, k, _ = w.shape
    eye = jnp.eye(n, dtype=w.dtype)
    return (eye[:, None, :, None] * w[:, :, None, :]).reshape(n * k, n * k)


def _source(src, mb0):
    if src[0] == 'raw':
        _, x, xb0 = src
        return False, x.shape[1], [pl.BlockSpec((None, TS, D), lambda b, s: (b + xb0, s, 0))], [x]
    seq = src[1].shape[1]
    return True, seq, _moe_output_specs(seq // TS, mb0), list(src[1:])


def _even_layer(src, span, dep, m1, m2, ln, router, tri, w_in, pool_w, pool_scale, conv_w, conv_b,
                w_a, b_a, w_x, b_x, lam, w_out, b_out):
    mb0, nb = span
    from_moe, seq, src_specs, src_ops = _source(src, mb0)
    n_s = seq // TS
    wr_hi, wr_lo, br = router
    wg = jnp.concatenate([_block_diag(w_a), _block_diag(w_x)], axis=1).astype(_BF16)
    bg = jnp.concatenate([b_a, b_x])[None, :]
    ep_in, ep_out = _epilogue_specs(n_s, mb0)
    const = lambda shape: pl.BlockSpec(shape, lambda b, s: (0,) * len(shape))
    in_specs = src_specs + [
        pl.BlockSpec((None, 3, D), lambda b, s: (b + mb0, 0, 0)),
        const((D, D_IN_AB)), const((D_POOL, D_POOL)), const((1, D_POOL)),
        const((CONV_WIDTH, D_REC)), const((1, D_REC)), const((D_REC, 2 * D_REC)),
        const((1, 2 * D_REC)), const((1, D_REC)), const((D, D)), const((1, D)),
    ] + ep_in
    scratch = [
        pltpu.VMEM((TS + POOL_PAD, D_POOL), _F32),
        pltpu.VMEM((TS + POOL_PAD, D_POOL), _F32),
        pltpu.VMEM((TS + POOL_PAD, 3 * POOL_GROUP), _F32),
        pltpu.VMEM((TS + POOL_PAD, 2 * POOL_GROUP), _F32),
        pltpu.VMEM((TS + CONV_PAD, D_REC), _F32),
        pltpu.VMEM((TS + SCAN_PAD, D_REC), _F32),
        pltpu.VMEM((TS + SCAN_PAD, D_REC), _F32),
        pltpu.VMEM((TS + SCAN_PAD, D_REC), _F32),
        pltpu.VMEM((TS + SCAN_PAD, D_REC), _F32),
        pltpu.VMEM((8, D_REC), _F32),
    ]
    return pl.pallas_call(
        functools.partial(_even_kernel, from_moe=from_moe),
        grid=(nb, n_s),
        in_specs=in_specs,
        out_specs=ep_out,
        out_shape=_epilogue_out_shapes(nb, seq),
        scratch_shapes=scratch,
        compiler_params=pltpu.CompilerParams(
            dimension_semantics=("arbitrary", "arbitrary"), vmem_limit_bytes=VMEM_LIMIT),
        name="even_mixer",
    )(*src_ops, m1, w_in.astype(_BF16), _block_diag(pool_w).astype(_BF16), pool_scale[None, :],
      conv_w, conv_b[None, :], wg, bg, lam[None, :], w_out.astype(_BF16), b_out[None, :],
      m2, ln, wr_hi, wr_lo, br, tri, dep)


def _odd_kernel(*refs, from_moe):
    x, refs = _mixer_input(refs, from_moe)
    (m1_ref, win_ref, bin_ref, sln_ref, ws_ref, bs_ref, wout_ref, bout_ref,
     m2_ref, ln_ref, wr_hi_ref, wr_lo_ref, br_ref, tri_ref, dep_ref,
     x1_ref, h2_ref, idx_ref, gate_ref, rank_ref, cnt_ref,
     sv_buf) = refs
    del dep_ref
    b = pl.program_id(0)
    s = pl.program_id(1)
    h = x * (1.0 + m1_ref[1:2, :]) + m1_ref[0:1, :]
    z = _gelu(_dot(h.astype(_BF16), win_ref[...]) + bin_ref[...])
    u = z[:, :D]
    v = _layer_norm(z[:, D:], sln_ref[0:1, :], sln_ref[1:2, :]).astype(_BF16)
    n_chunks = TS // CHUNK
    row = lax.broadcasted_iota(jnp.int32, (CHUNK, CHUNK), 0)
    col = lax.broadcasted_iota(jnp.int32, (CHUNK, CHUNK), 1)
    causal = col <= row
    for hh in range(N_SGU_HEADS):
        lanes = slice(hh * CHUNK, (hh + 1) * CHUNK)
        w_h = jnp.where(causal, ws_ref[hh], 0.0).astype(_BF16)
        v_h = jnp.concatenate(
            [v[n * CHUNK:(n + 1) * CHUNK, lanes] for n in range(n_chunks)], axis=1)
        sv_h = _dot(w_h, v_h)
        for n in range(n_chunks):
            sv_buf[n * CHUNK:(n + 1) * CHUNK, lanes] = (
                sv_h[:, n * CHUNK:(n + 1) * CHUNK] + bs_ref[hh])
    y = _dot((u * sv_buf[...]).astype(_BF16), wout_ref[...]) + bout_ref[...]
    first = jnp.logical_and(b == 0, s == 0)
    _finish_sublayer([(0, x, y)], m1_ref, ln_ref, m2_ref, wr_hi_ref, wr_lo_ref, br_ref, tri_ref,
                     first, x1_ref, h2_ref, idx_ref, gate_ref, rank_ref, cnt_ref)


def _odd_layer(src, span, dep, m1, m2, ln, router, tri, w_in, b_in, ln_g, ln_b, w_s, b_s, w_out,
               b_out):
    mb0, nb = span
    from_moe, seq, src_specs, src_ops = _source(src, mb0)
    n_s = seq // TS
    wr_hi, wr_lo, br = router
    sln = jnp.stack([ln_g, ln_b])
    bs = jnp.broadcast_to(b_s[:, :, None], (N_SGU_HEADS, CHUNK, CHUNK))
    ep_in, ep_out = _epilogue_specs(n_s, mb0)
    const = lambda shape: pl.BlockSpec(shape, lambda b, s: (0,) * len(shape))
    in_specs = src_specs + [
        pl.BlockSpec((None, 3, D), lambda b, s: (b + mb0, 0, 0)),
        const((D, 2 * D)), const((1, 2 * D)), const((2, D)),
        const((N_SGU_HEADS, CHUNK, CHUNK)), const((N_SGU_HEADS, CHUNK, CHUNK)),
        const((D, D)), const((1, D)),
    ] + ep_in
    return pl.pallas_call(
        functools.partial(_odd_kernel, from_moe=from_moe),
        grid=(nb, n_s),
        in_specs=in_specs,
        out_specs=ep_out,
        out_shape=_epilogue_out_shapes(nb, seq),
        scratch_shapes=[pltpu.VMEM((TS, D), _F32)],
        compiler_params=pltpu.CompilerParams(
            dimension_semantics=("arbitrary", "arbitrary"), vmem_limit_bytes=VMEM_LIMIT),
        name="odd_mixer",
    )(*src_ops, m1, w_in.astype(_BF16), b_in[None, :], sln, w_s, bs, w_out.astype(_BF16),
      b_out[None, :], m2, ln, wr_hi, wr_lo, br, tri, dep)


_BLOCK_SHIFT = EXPERT_BLOCK.bit_length() - 1
_TAB_ROWS = 8


def _blocks_kernel(cnt_ref, tab_ref, pstart_ref):
    n_lanes = tab_ref.shape[1]
    cnt = cnt_ref[:, 0:1].astype(jnp.int32)
    padded = ((cnt + (EXPERT_BLOCK - 1)) >> _BLOCK_SHIFT) << _BLOCK_SHIFT
    r = lax.broadcasted_iota(jnp.int32, (N_EXPERTS, N_EXPERTS), 0)
    c = lax.broadcasted_iota(jnp.int32, (N_EXPERTS, N_EXPERTS), 1)
    as_row = lambda col: jnp.sum(jnp.where(r == c, col, 0), axis=0, keepdims=True)
    pends = jnp.sum(jnp.where(c <= r, as_row(padded), 0), axis=1, keepdims=True)
    pstart = pends - padded
    total = jnp.sum(padded, axis=0, keepdims=True)
    later = jnp.where(c > r, as_row(cnt), 0) > 0
    nxt = jnp.min(jnp.where(later, c, N_EXPERTS), axis=1, keepdims=True)
    nxt = jnp.where(nxt < N_EXPERTS, nxt, -1)
    e_iota = lax.broadcasted_iota(jnp.int32, (N_EXPERTS, n_lanes), 0)
    start = lax.broadcasted_iota(jnp.int32, (N_EXPERTS, n_lanes), 1) << _BLOCK_SHIFT
    block_e = jnp.minimum(
        jnp.sum(jnp.where(start >= pends, 1, 0), axis=0, keepdims=True), N_EXPERTS - 1)
    mine = e_iota == block_e
    seg_end = jnp.sum(jnp.where(mine, pstart + cnt, 0), axis=0, keepdims=True)
    n_valid = jnp.clip(seg_end - start[0:1, :], 0, EXPERT_BLOCK)
    n_valid = jnp.where(start[0:1, :] < total, n_valid, 0)
    next_e = jnp.sum(jnp.where(mine, nxt, 0), axis=0, keepdims=True)
    n_used = jnp.broadcast_to(total >> _BLOCK_SHIFT, (1, n_lanes))
    zeros = jnp.zeros((_TAB_ROWS - 4, n_lanes), jnp.int32)
    tab_ref[...] = jnp.concatenate([block_e, n_valid, next_e, n_used, zeros], axis=0)
    pstart_ref[...] = jnp.broadcast_to(pstart, pstart_ref.shape)


def _dest_kernel(pstart_ref, idx_ref, rank_ref, dest_ref):
    idx = idx_ref[...]
    dest = rank_ref[...]
    for e in range(N_EXPERTS):
        dest = dest + jnp.where(idx == e, pstart_ref[e], 0)
    for ch in range(dest_ref.shape[0]):
        dest_ref[ch] = dest[:, ch * SC_CHUNK:(ch + 1) * SC_CHUNK]


def _plan(idx, rank, counts):
    t = idx.shape[1]
    n_blocks = (t * TOP_K) // EXPERT_BLOCK + N_EXPERTS
    n_lanes = -(-n_blocks // LANES) * LANES
    tab, pstart = pl.pallas_call(
        _blocks_kernel,
        out_shape=[jax.ShapeDtypeStruct((_TAB_ROWS, n_lanes), jnp.int32),
                   jax.ShapeDtypeStruct((N_EXPERTS, LANES), jnp.int32)],
        name="moe_blocks",
    )(counts)
    tl = min(t, 8192)
    dest3 = pl.pallas_call(
        _dest_kernel,
        grid_spec=pltpu.PrefetchScalarGridSpec(
            num_scalar_prefetch=1,
            grid=(t // tl,),
            in_specs=[pl.BlockSpec((TOP_K, tl), lambda i, p: (0, i)),
                      pl.BlockSpec((TOP_K, tl), lambda i, p: (0, i))],
            out_specs=pl.BlockSpec((tl // SC_CHUNK, TOP_K, SC_CHUNK), lambda i, p: (i, 0, 0)),
        ),
        out_shape=jax.ShapeDtypeStruct((t // SC_CHUNK, TOP_K, SC_CHUNK), jnp.int32),
        compiler_params=pltpu.CompilerParams(dimension_semantics=("arbitrary",)),
        name="moe_dest",
    )(pstart[:, 0], idx, rank)
    return dest3, (tab[0, :n_blocks], tab[1, :n_blocks], tab[3, :1], tab[2, :n_blocks])


def _sc_mesh():
    return plsc.VectorSubcoreMesh(core_axis_name="c", subcore_axis_name="s")


def _sc_dispatch(h2_rows, dest3, cap):
    t = h2_rows.shape[0]
    per_w = (t // SC_CHUNK) // (SC_CORES * SC_SUBCORES)

    @functools.partial(
        pl.kernel, mesh=_sc_mesh(),
        out_type=jax.ShapeDtypeStruct((cap, PACK_TILES, LANES), _ROWS),
        scratch_types=[pltpu.VMEM((TOP_K, SC_CHUNK), jnp.int32),
                       pltpu.VMEM((SC_CHUNK, PACK_TILES, LANES), _ROWS)],
        name="moe_dispatch",
    )
    def run(x_hbm, dest_hbm, out_hbm, idx_v, rows_v):
        wid = lax.axis_index("s") * SC_CORES + lax.axis_index("c")

        @pl.loop(0, per_w)
        def _(i):
            c = wid * per_w + i
            pltpu.sync_copy(dest_hbm.at[c], idx_v)
            pltpu.sync_copy(x_hbm.at[pl.ds(c * SC_CHUNK, SC_CHUNK)], rows_v)
            for k in range(TOP_K):
                pltpu.sync_copy(rows_v, out_hbm.at[idx_v.at[k]])

    return run(h2_rows, dest3)


def _sc_combine(ys_rows, dest3, t):
    per_w = (t // SC_CHUNK) // (SC_CORES * SC_SUBCORES)

    @functools.partial(
        pl.kernel, mesh=_sc_mesh(),
        out_type=jax.ShapeDtypeStruct((TOP_K, t, PACK_TILES, LANES), _ROWS),
        scratch_types=[pltpu.VMEM((TOP_K, SC_CHUNK), jnp.int32),
                       pltpu.VMEM((SC_CHUNK, PACK_TILES, LANES), _ROWS)],
        name="moe_combine",
    )
    def run(ys_hbm, dest_hbm, out_hbm, idx_v, rows_v):
        wid = lax.axis_index("s") * SC_CORES + lax.axis_index("c")

        @pl.loop(0, per_w)
        def _(i):
            c = wid * per_w + i
            pltpu.sync_copy(dest_hbm.at[c], idx_v)
            for k in range(TOP_K):
                pltpu.sync_copy(ys_hbm.at[idx_v.at[k]], rows_v)
                pltpu.sync_copy(rows_v, out_hbm.at[k, pl.ds(c * SC_CHUNK, SC_CHUNK)])

    return run(ys_rows, dest3)


def _ffn_kernel(be_ref, nv_ref, nu_ref, nx_ref, xs_ref, b1_ref, b2_ref, perm_ref, w1_hbm, w2_hbm,
                dep_ref, ys_ref, w1_f, w2_f, w1_s, w2_s, sems, *, layer):
    del dep_ref
    i = pl.program_id(0)
    active = i < nu_ref[0]
    e = be_ref[i]
    new_expert = jnp.logical_or(i == 0, e != be_ref[jnp.maximum(i - 1, 0)])

    def weight_copies(expert):
        return (pltpu.make_async_copy(w1_hbm.at[layer, expert], w1_f, sems.at[0]),
                pltpu.make_async_copy(w2_hbm.at[layer, expert], w2_f, sems.at[1]))

    @pl.when(jnp.logical_and(active, i == 0))
    def _():
        for cp in weight_copies(e):
            cp.start()

    @pl.when(jnp.logical_and(active, new_expert))
    def _():
        for cp in weight_copies(e):
            cp.wait()
        half = PERM // 2
        for n in range(2 * D_FF // PERM):
            blk = w1_f[:, n * PERM:(n + 1) * PERM].astype(_BF16)
            moved = _dot(blk, perm_ref[...]).astype(_BF16)
            w1_s[:, n * half:(n + 1) * half] = moved[:, :half]
            w1_s[:, D_FF + n * half:D_FF + (n + 1) * half] = moved[:, half:]
        w2_s[...] = w2_f[...].astype(_BF16)

        @pl.when(nx_ref[i] >= 0)
        def _():
            for cp in weight_copies(nx_ref[i]):
                cp.start()

    @pl.when(active)
    def _():
        x = _load_rows(xs_ref, EXPERT_BLOCK)
        rows = lax.broadcasted_iota(jnp.int32, (EXPERT_BLOCK, 1), 0)
        x = jnp.where(rows < nv_ref[i], x, 0.0).astype(_BF16)
        hb = _dot(x, w1_s[...]) + b1_ref[...]
        x_glu = jnp.minimum(hb[:, :D_FF], SWIGLU_LIMIT)
        x_lin = jnp.clip(hb[:, D_FF:], -SWIGLU_LIMIT, SWIGLU_LIMIT)
        act = x_glu * jax.nn.sigmoid(SWIGLU_ALPHA * x_glu) * (x_lin + 1.0)
        y = _dot(act.astype(_BF16), w2_s[...]) + b2_ref[...]
        _store_rows(ys_ref, 0, y)


def _expert_ffn(xs_rows, blocks, w1, b1, w2, b2, layer, dep):
    block_e, n_valid, n_used, next_e = blocks
    n_blocks = block_e.shape[0]
    half = PERM // 2
    src = jnp.arange(PERM)
    dst = jnp.where(src % 2 == 0, src // 2, half + src // 2)
    perm = (dst[:, None] == jnp.arange(PERM)[None, :]).astype(_BF16)
    blk = lambda i, be, nv, nu, nx: (jnp.minimum(i, nu[0] - 1), 0)
    bsel = lambda i, be, nv, nu, nx: (layer, be[jnp.minimum(i, nu[0] - 1)], 0, 0)
    return pl.pallas_call(
        functools.partial(_ffn_kernel, layer=layer),
        grid_spec=pltpu.PrefetchScalarGridSpec(
            num_scalar_prefetch=4,
            grid=(n_blocks,),
            in_specs=[
                pl.BlockSpec((EXPERT_BLOCK * PACK_TILES, LANES), blk),
                pl.BlockSpec((None, None, 1, 2 * D_FF), bsel),
                pl.BlockSpec((None, None, 1, D), bsel),
                pl.BlockSpec((PERM, PERM), lambda i, be, nv, nu, nx: (0, 0)),
                pl.BlockSpec(memory_space=pl.ANY),
                pl.BlockSpec(memory_space=pl.ANY),
                pl.BlockSpec(memory_space=pl.ANY),
            ],
            out_specs=pl.BlockSpec((EXPERT_BLOCK * PACK_TILES, LANES), blk),
            scratch_shapes=[
                pltpu.VMEM((D, 2 * D_FF), _F32), pltpu.VMEM((D_FF, D), _F32),
                pltpu.VMEM((D, 2 * D_FF), _BF16), pltpu.VMEM((D_FF, D), _BF16),
                pltpu.SemaphoreType.DMA((2,)),
            ],
        ),
        out_shape=jax.ShapeDtypeStruct(xs_rows.shape, _ROWS),
        compiler_params=pltpu.CompilerParams(
            dimension_semantics=("arbitrary",), vmem_limit_bytes=VMEM_LIMIT),
        name="expert_ffn",
    )(block_e, n_valid, n_used, next_e, xs_rows, b1, b2, perm, w1, w2, dep)


def _post_kernel(x1_ref, yg_ref, gate_ref, m2_ref, ln_ref, prev_ref, o_ref):
    del prev_ref
    o_ref[...] = _moe_output(x1_ref, yg_ref, gate_ref, m2_ref, ln_ref)


def _moe_post(src, mb0, total_nb, prev, keep_prev):
    nb, seq, _ = src[1].shape
    n_s = seq // TS
    out_shape = jax.ShapeDtypeStruct((total_nb, seq, D), _F32)
    aliases = {5: 0} if keep_prev else {}
    return pl.pallas_call(
        _post_kernel,
        grid=(nb, n_s),
        in_specs=_moe_output_specs(n_s, mb0) + [pl.BlockSpec(memory_space=pl.ANY)],
        out_specs=pl.BlockSpec((None, TS, D), lambda b, s: (b + mb0, s, 0)),
        out_shape=out_shape,
        input_output_aliases=aliases,
        compiler_params=pltpu.CompilerParams(
            dimension_semantics=("arbitrary", "arbitrary"), vmem_limit_bytes=VMEM_LIMIT),
        name="moe_post",
    )(*src[1:], prev)


def _moe_layer(mixed, m2, ln, w1, b1, w2, b2, layer):
    n = len(mixed)
    nb, seq, _ = mixed[0][0].shape
    t = nb * seq
    cap = (t * TOP_K // EXPERT_BLOCK + N_EXPERTS) * EXPERT_BLOCK
    dest3s, yss = [], []
    staged = []
    for x1, h2_rows, idx, gates, rank, counts in mixed:
        dest3, blocks = _plan(idx, rank, counts)
        xs = _sc_dispatch(h2_rows.reshape(t, PACK_TILES, LANES), dest3, cap)
        staged.append((xs, blocks))
        dest3s.append(dest3)
    for g, (xs, blocks) in enumerate(staged):
        dep = mixed[n - 1][5] if g == 0 else yss[g - 1]
        yss.append(_expert_ffn(xs.reshape(cap * PACK_TILES, LANES), blocks, w1, b1, w2, b2,
                               layer, dep))
    srcs = []
    for g, (ys, dest3) in enumerate(zip(yss, dest3s)):
        yg = _sc_combine(ys.reshape(cap, PACK_TILES, LANES), dest3, t)
        srcs.append(('moe', mixed[g][0], yg.reshape(TOP_K, t * PACK_TILES, LANES), mixed[g][3],
                     m2, ln))
    return srcs, yss[n - 1]


def kernel(x, c, mod_w, mod_b, ln_g, ln_b, ab_w_in, pool_w, pool_scale, conv_w, conv_b, lru_w_a, lru_b_a, lru_w_x, lru_b_x, lru_lambda, ab_w_out, ab_b_out, sgu_w_in, sgu_b_in, sgu_ln_g, sgu_ln_b, sgu_w_s, sgu_b_s, sgu_w_out, sgu_b_out, router_w, router_b, moe_w1, moe_b1, moe_w2, moe_b2):
    assert x.shape[1] % TS == 0 and x.shape[2] == D
    mods = _modulation(c, mod_w, mod_b)
    tri = (lax.broadcasted_iota(jnp.int32, (TS, TS), 0)
           < lax.broadcasted_iota(jnp.int32, (TS, TS), 1)).astype(_BF16)
    b1 = jnp.concatenate([moe_b1[..., 0::2], moe_b1[..., 1::2]], axis=-1)[:, :, None, :]
    b2 = moe_b2[:, :, None, :]
    nb = x.shape[0]
    n_groups = 2 if nb % 2 == 0 else 1
    gb = nb // n_groups
    srcs = [('raw', x, g * gb) for g in range(n_groups)]
    order = tri
    for layer in range(DEPTH):
        j = layer // 2
        m1, m2 = mods[layer, 0], mods[layer, 1]
        ln1 = jnp.stack([ln_g[layer, 0], ln_b[layer, 0]])
        ln2 = jnp.stack([ln_g[layer, 1], ln_b[layer, 1]])
        wr_hi, wr_lo = _split(router_w[layer].T)
        router = (wr_hi, wr_lo, router_b[layer][:, None])
        mixed = []
        for g in range(n_groups):
            span = (g * gb, gb)
            if layer % 2 == 0:
                outs = _even_layer(srcs[g], span, order, m1, m2, ln1, router, tri, ab_w_in[j],
                                   pool_w[j], pool_scale[j], conv_w[j], conv_b[j], lru_w_a[j],
                                   lru_b_a[j], lru_w_x[j], lru_b_x[j], lru_lambda[j], ab_w_out[j],
                                   ab_b_out[j])
            else:
                outs = _odd_layer(srcs[g], span, order, m1, m2, ln1, router, tri, sgu_w_in[j],
                                  sgu_b_in[j], sgu_ln_g[j], sgu_ln_b[j], sgu_w_s[j], sgu_b_s[j],
                                  sgu_w_out[j], sgu_b_out[j])
            mixed.append(outs)
            order = outs[5]
        srcs, order = _moe_layer(mixed, m2, ln2, moe_w1, b1, moe_w2, b2, layer)
    out = order
    for g in range(n_groups):
        out = _moe_post(srcs[g], g * gb, nb, out, keep_prev=g > 0)
    return out
```

```python
import functools

import jax
import jax.numpy as jnp
from jax import lax
from jax.experimental import pallas as pl
from jax.experimental.pallas import tpu as pltpu
from jax.experimental.pallas import tpu_sc as plsc

D = 1024
DEPTH = 4
POOL_WINDOWS = (2, 4, 8, 16)
POOL_GROUP = 128
D_POOL = 512
D_REC = 512
N_REC_HEADS = 8
REC_HEAD = 64
CONV_WIDTH = 4
LRU_C = 8.0
D_IN_AB = 1536
CHUNK = 128
N_SGU_HEADS = 8
N_EXPERTS = 32
TOP_K = 4
D_FF = 1024
SWIGLU_LIMIT = 7.0
SWIGLU_ALPHA = 1.702
EXPERT_BLOCK = 512
LN_EPS = 1e-5
ALPHA = (2 * DEPTH) ** 0.25

TS = 512
LANES = 128
PACK_TILES = D // (2 * LANES)
_ROWS = jnp.uint32
POOL_PAD = 32
CONV_PAD = 8
SCAN_CHUNKS = 8
SCAN_LEN = TS // SCAN_CHUNKS
SCAN_PITCH = SCAN_LEN + 8
SC_CORES = 2
SC_SUBCORES = 16
SC_CHUNK = 128
PERM = 256
VMEM_LIMIT = 56 * 1024 * 1024

_F32 = jnp.float32
_BF16 = jnp.bfloat16
_NT = (((1,), (1,)), ((), ()))


def _split(a):
    hi = a.astype(_BF16)
    lo = (a - hi.astype(_F32)).astype(_BF16)
    return hi, lo


def _dot(a, b):
    return jnp.dot(a, b, preferred_element_type=_F32)


def _layer_norm(x, g, b):
    mu = jnp.mean(x, axis=-1, keepdims=True)
    xc = x - mu
    var = jnp.mean(xc * xc, axis=-1, keepdims=True)
    return xc * lax.rsqrt(var + LN_EPS) * g + b


def _gelu(x):
    return jax.nn.gelu(x)


def _mod_kernel(c_ref, w_ref, b_ref, o_ref):
    c = c_ref[...]
    ca = c * jax.nn.sigmoid(c)
    a_hi, a_lo = _split(ca)
    w_hi, w_lo = _split(w_ref[...])
    o_ref[...] = _dot(a_hi, w_hi) + _dot(a_lo, w_hi) + _dot(a_hi, w_lo) + b_ref[...]


def _modulation(c, mod_w, mod_b):
    nb = c.shape[0]
    n = DEPTH * 2
    w = mod_w.reshape(n, D, 3 * D)
    b = mod_b.reshape(n, 1, 3 * D)
    out = pl.pallas_call(
        _mod_kernel,
        grid=(n, 3),
        in_specs=[
            pl.BlockSpec((nb, D), lambda i, j: (0, 0)),
            pl.BlockSpec((None, D, D), lambda i, j: (i, 0, j)),
            pl.BlockSpec((None, 1, D), lambda i, j: (i, 0, j)),
        ],
        out_specs=pl.BlockSpec((None, nb, D), lambda i, j: (i, 0, j)),
        out_shape=jax.ShapeDtypeStruct((n, nb, 3 * D), _F32),
        compiler_params=pltpu.CompilerParams(
            dimension_semantics=("arbitrary", "arbitrary"), vmem_limit_bytes=VMEM_LIMIT),
        name="modulation",
    )(c, w, b)
    return out.reshape(DEPTH, 2, nb, 3, D)


def _store_rows(ref, r0, val):
    rows = val.shape[0]
    for q in range(PACK_TILES):
        lo = val[:, q * LANES:(q + 1) * LANES].astype(_BF16).astype(_F32)
        hi = val[:, D // 2 + q * LANES:D // 2 + (q + 1) * LANES].astype(_BF16).astype(_F32)
        word = (pltpu.bitcast(lo, jnp.uint32) >> 16) | pltpu.bitcast(hi, jnp.uint32)
        ref[pl.ds(r0 * PACK_TILES + q, rows, stride=PACK_TILES), :] = word


def _load_rows(ref, rows):
    lo, hi = [], []
    for q in range(PACK_TILES):
        word = ref[pl.ds(q, rows, stride=PACK_TILES), :]
        lo.append(pltpu.bitcast(word << 16, _F32))
        hi.append(pltpu.bitcast(word & jnp.uint32(0xFFFF0000), _F32))
    return jnp.concatenate(lo + hi, axis=1)


def _moe_output(x1_ref, yg_ref, gate_ref, m2_ref, ln_ref):
    gt = gate_ref[...]
    pad = jnp.zeros((LANES - TOP_K, TS), _F32)
    g_rows = jnp.concatenate([gt, pad], axis=0).T
    y = jnp.zeros((TS, D), _F32)
    for k in range(TOP_K):
        y = y + g_rows[:, k:k + 1] * _load_rows(yg_ref.at[k], TS)
    gate2 = 1.0 + m2_ref[2:3, :]
    return _layer_norm(ALPHA * x1_ref[...] + gate2 * y, ln_ref[0:1, :], ln_ref[1:2, :])


def _moe_output_specs(n_seq_tiles, mb0):
    tile = lambda b, s: b * n_seq_tiles + s
    return [
        pl.BlockSpec((None, TS, D), lambda b, s: (b, s, 0)),
        pl.BlockSpec((TOP_K, TS * PACK_TILES, LANES), lambda b, s: (0, tile(b, s), 0)),
        pl.BlockSpec((TOP_K, TS), lambda b, s: (0, tile(b, s))),
        pl.BlockSpec((None, 3, D), lambda b, s: (b + mb0, 0, 0)),
        pl.BlockSpec((2, D), lambda b, s: (0, 0)),
    ]


def _mixer_input(refs, from_moe):
    if from_moe:
        return _moe_output(*refs[:5]), refs[5:]
    return refs[0][...], refs[1:]


def _finish_sublayer(parts, m1_ref, ln_ref, m2_ref, wr_hi_ref, wr_lo_ref, br_ref, tri_ref, first,
                     x1_ref, h2_ref, idx_ref, gate_ref, rank_ref, cnt_ref):
    @pl.when(first)
    def _():
        cnt_ref[...] = jnp.zeros_like(cnt_ref)

    gate1 = 1.0 + m1_ref[2:3, :]
    w_hi = wr_hi_ref[...]
    seen = cnt_ref[:, 0:1]
    for r0, x, y in parts:
        n = x.shape[0]
        x1 = _layer_norm(ALPHA * x + gate1 * y, ln_ref[0:1, :], ln_ref[1:2, :])
        x1_ref[r0:r0 + n, :] = x1
        h2 = x1 * (1.0 + m2_ref[1:2, :]) + m2_ref[0:1, :]
        _store_rows(h2_ref, r0, h2)

        h_hi, h_lo = _split(h2)
        logits = (lax.dot_general(w_hi, h_hi, _NT, preferred_element_type=_F32)
                  + lax.dot_general(wr_lo_ref[...], h_hi, _NT, preferred_element_type=_F32)
                  + lax.dot_general(w_hi, h_lo, _NT, preferred_element_type=_F32)
                  + br_ref[...])
        iota_e = lax.broadcasted_iota(jnp.int32, (N_EXPERTS, n), 0)
        work = logits
        vals, idxs, hots = [], [], []
        for _ in range(TOP_K):
            m = jnp.max(work, axis=0, keepdims=True)
            idx = jnp.min(jnp.where(work == m, iota_e, N_EXPERTS), axis=0, keepdims=True)
            hot = iota_e == idx
            work = jnp.where(hot, -jnp.inf, work)
            vals.append(m)
            idxs.append(idx)
            hots.append(hot)
        exps = [jnp.exp(v - vals[0]) for v in vals]
        denom = exps[0] + exps[1] + exps[2] + exps[3]
        gate_ref[:, r0:r0 + n] = jnp.concatenate([e / denom for e in exps], axis=0)
        idx_ref[:, r0:r0 + n] = jnp.concatenate(idxs, axis=0)

        sel = jnp.zeros((N_EXPERTS, n), _F32)
        for hot in hots:
            sel = sel + jnp.where(hot, 1.0, 0.0)
        before = _dot(sel.astype(_BF16), tri_ref[0:n, 0:n]) + seen
        ranks = [jnp.sum(jnp.where(hot, before, 0.0), axis=0, keepdims=True) for hot in hots]
        rank_ref[:, r0:r0 + n] = jnp.concatenate(ranks, axis=0).astype(jnp.int32)
        seen = seen + jnp.sum(sel, axis=1, keepdims=True)
    cnt_ref[...] = jnp.broadcast_to(seen, cnt_ref.shape)


def _epilogue_specs(n_seq_tiles, mb0):
    tile = lambda b, s: b * n_seq_tiles + s
    in_specs = [
        pl.BlockSpec((None, 3, D), lambda b, s: (b + mb0, 0, 0)),
        pl.BlockSpec((2, D), lambda b, s: (0, 0)),
        pl.BlockSpec((N_EXPERTS, D), lambda b, s: (0, 0)),
        pl.BlockSpec((N_EXPERTS, D), lambda b, s: (0, 0)),
        pl.BlockSpec((N_EXPERTS, 1), lambda b, s: (0, 0)),
        pl.BlockSpec((TS, TS), lambda b, s: (0, 0)),
        pl.BlockSpec(memory_space=pl.ANY),
    ]
    out_specs = [
        pl.BlockSpec((None, TS, D), lambda b, s: (b, s, 0)),
        pl.BlockSpec((TS * PACK_TILES, LANES), lambda b, s: (tile(b, s), 0)),
        pl.BlockSpec((TOP_K, TS), lambda b, s: (0, tile(b, s))),
        pl.BlockSpec((TOP_K, TS), lambda b, s: (0, tile(b, s))),
        pl.BlockSpec((TOP_K, TS), lambda b, s: (0, tile(b, s))),
        pl.BlockSpec((N_EXPERTS, LANES), lambda b, s: (0, 0)),
    ]
    return in_specs, out_specs


def _epilogue_out_shapes(nb, seq):
    t = nb * seq
    return [
        jax.ShapeDtypeStruct((nb, seq, D), _F32),
        jax.ShapeDtypeStruct((t * PACK_TILES, LANES), _ROWS),
        jax.ShapeDtypeStruct((TOP_K, t), jnp.int32),
        jax.ShapeDtypeStruct((TOP_K, t), _F32),
        jax.ShapeDtypeStruct((TOP_K, t), jnp.int32),
        jax.ShapeDtypeStruct((N_EXPERTS, LANES), _F32),
    ]


def _even_kernel(*refs, from_moe):
    x, refs = _mixer_input(refs, from_moe)
    (m1_ref, win_ref, poolw_ref, pscale_ref, convw_ref, convb_ref, wg_ref, bg_ref, lam_ref,
     wout_ref, bout_ref,
     m2_ref, ln_ref, wr_hi_ref, wr_lo_ref, br_ref, tri_ref, dep_ref,
     x1_ref, h2_ref, idx_ref, gate_ref, rank_ref, cnt_ref,
     p1, p2, p4, p8, cbuf, s_a, s_b, s_h, s_p, hstate) = refs
    del dep_ref
    b = pl.program_id(0)
    s = pl.program_id(1)
    h = x * (1.0 + m1_ref[1:2, :]) + m1_ref[0:1, :]
    z = _dot(h.astype(_BF16), win_ref[...])
    xp = z[:, :D_POOL]
    xr = z[:, D_POOL:D_POOL + D_REC]
    gr = z[:, D_POOL + D_REC:]

    @pl.when(s == 0)
    def _():
        p1[0:POOL_PAD, :] = jnp.zeros((POOL_PAD, D_POOL), _F32)
        p2[0:POOL_PAD, :] = jnp.zeros((POOL_PAD, D_POOL), _F32)
        p4[0:POOL_PAD, :] = jnp.zeros((POOL_PAD, 3 * POOL_GROUP), _F32)
        p8[0:POOL_PAD, :] = jnp.zeros((POOL_PAD, 2 * POOL_GROUP), _F32)
        cbuf[0:CONV_PAD, :] = jnp.zeros((CONV_PAD, D_REC), _F32)
        hstate[...] = jnp.zeros_like(hstate)

    rr = TS + 16
    p1[POOL_PAD:, :] = xp
    p2[16:, :] = p1[16:16 + rr, :] + p1[15:15 + rr, :]
    p4[16:, :] = p2[16:16 + rr, POOL_GROUP:] + p2[14:14 + rr, POOL_GROUP:]
    p8[16:, :] = p4[16:16 + rr, POOL_GROUP:] + p4[12:12 + rr, POOL_GROUP:]
    s16 = p8[POOL_PAD:, POOL_GROUP:] + p8[POOL_PAD - 8:POOL_PAD - 8 + TS, POOL_GROUP:]
    sums = (p2[POOL_PAD:, 0:POOL_GROUP], p4[POOL_PAD:, 0:POOL_GROUP],
            p8[POOL_PAD:, 0:POOL_GROUP], s16)
    pos = (s * TS + lax.broadcasted_iota(jnp.int32, (TS, 1), 0) + 1).astype(_F32)
    pooled = []
    for g, w in enumerate(POOL_WINDOWS):
        cnt = jnp.minimum(pos, float(w))
        pooled.append(sums[g] / cnt - xp[:, g * POOL_GROUP:(g + 1) * POOL_GROUP])
    pooled = jnp.concatenate(pooled, axis=1).astype(_BF16)
    y_pool = _dot(pooled, poolw_ref[...]) * pscale_ref[...]
    p1[16:POOL_PAD, :] = p1[TS + 16:TS + POOL_PAD, :]

    cbuf[CONV_PAD:, :] = xr
    xc = convb_ref[...]
    for k in range(CONV_WIDTH):
        off = CONV_PAD - (CONV_WIDTH - 1) + k
        xc = xc + cbuf[off:off + TS, :] * convw_ref[k:k + 1, :]
    cbuf[0:CONV_PAD, :] = cbuf[TS:TS + CONV_PAD, :]
    gates = _dot(xc.astype(_BF16), wg_ref[...]) + bg_ref[...]
    r_gate = jax.nn.sigmoid(gates[:, :D_REC])
    i_gate = jax.nn.sigmoid(gates[:, D_REC:])
    log_a = -LRU_C * r_gate * jax.nn.softplus(-lam_ref[...])
    a = jnp.exp(log_a)
    mult = jnp.sqrt(-jnp.tanh(log_a) * (a * a + 1.0))
    b_in = mult * (i_gate * xc)

    lane_tiles = [slice(lt * LANES, (lt + 1) * LANES) for lt in range(D_REC // LANES)]
    for lt, lanes in enumerate(lane_tiles):
        for ch in range(SCAN_CHUNKS):
            dst = pl.ds(ch * SCAN_PITCH, SCAN_LEN)
            s_a[lt, dst, :] = a[ch * SCAN_LEN:(ch + 1) * SCAN_LEN, lanes]
            s_b[lt, dst, :] = b_in[ch * SCAN_LEN:(ch + 1) * SCAN_LEN, lanes]
    h_loc = [jnp.zeros((SCAN_CHUNKS, LANES), _F32) for _ in lane_tiles]
    decay = [jnp.ones((SCAN_CHUNKS, LANES), _F32) for _ in lane_tiles]
    for step in range(SCAN_LEN):
        rows = pl.ds(step, SCAN_CHUNKS, stride=SCAN_PITCH)
        for lt in range(len(lane_tiles)):
            a_s = s_a[lt, rows, :]
            h_loc[lt] = a_s * h_loc[lt] + s_b[lt, rows, :]
            decay[lt] = a_s * decay[lt]
            s_h[lt, rows, :] = h_loc[lt]
            s_p[lt, rows, :] = decay[lt]
    cols = []
    for lt, lanes in enumerate(lane_tiles):
        carry = hstate[0:1, lanes]
        chunks = []
        for ch in range(SCAN_CHUNKS):
            src = pl.ds(ch * SCAN_PITCH, SCAN_LEN)
            chunks.append(s_h[lt, src, :] + s_p[lt, src, :] * carry)
            carry = decay[lt][ch:ch + 1, :] * carry + h_loc[lt][ch:ch + 1, :]
        hstate[0:1, lanes] = carry
        cols.append(jnp.concatenate(chunks, axis=0))
    h_rec = jnp.concatenate(cols, axis=1)
    y_rec = h_rec * _gelu(gr)

    y_mix = jnp.concatenate([y_pool, y_rec], axis=1).astype(_BF16)
    y = _dot(y_mix, wout_ref[...]) + bout_ref[...]
    first = jnp.logical_and(b == 0, s == 0)
    _finish_sublayer([(0, x, y)], m1_ref, ln_ref, m2_ref, wr_hi_ref, wr_lo_ref, br_ref, tri_ref,
                     first, x1_ref, h2_ref, idx_ref, gate_ref, rank_ref, cnt_ref)


def _block_diag(w):
    n, k, _ = w.shape
    eye = jnp.eye(n, dtype=w.dtype)
    return (eye[:, None, :, None] * w[:, :, None, :]).reshape(n * k, n * k)


def _source(src, mb0):
    if src[0] == 'raw':
        _, x, xb0 = src
        return False, x.shape[1], [pl.BlockSpec((None, TS, D), lambda b, s: (b + xb0, s, 0))], [x]
    seq = src[1].shape[1]
    return True, seq, _moe_output_specs(seq // TS, mb0), list(src[1:])


def _even_layer(src, span, dep, m1, m2, ln, router, tri, w_in, pool_w, pool_scale, conv_w, conv_b,
                w_a, b_a, w_x, b_x, lam, w_out, b_out):
    mb0, nb = span
    from_moe, seq, src_specs, src_ops = _source(src, mb0)
    n_s = seq // TS
    wr_hi, wr_lo, br = router
    wg = jnp.concatenate([_block_diag(w_a), _block_diag(w_x)], axis=1).astype(_BF16)
    bg = jnp.concatenate([b_a, b_x])[None, :]
    ep_in, ep_out = _epilogue_specs(n_s, mb0)
    const = lambda shape: pl.BlockSpec(shape, lambda b, s: (0,) * len(shape))
    in_specs = src_specs + [
        pl.BlockSpec((None, 3, D), lambda b, s: (b + mb0, 0, 0)),
        const((D, D_IN_AB)), const((D_POOL, D_POOL)), const((1, D_POOL)),
        const((CONV_WIDTH, D_REC)), const((1, D_REC)), const((D_REC, 2 * D_REC)),
        const((1, 2 * D_REC)), const((1, D_REC)), const((D, D)), const((1, D)),
    ] + ep_in
    scratch = [
        pltpu.VMEM((TS + POOL_PAD, D_POOL), _F32),
        pltpu.VMEM((TS + POOL_PAD, D_POOL), _F32),
        pltpu.VMEM((TS + POOL_PAD, 3 * POOL_GROUP), _F32),
        pltpu.VMEM((TS + POOL_PAD, 2 * POOL_GROUP), _F32),
        pltpu.VMEM((TS + CONV_PAD, D_REC), _F32),
    ] + [pltpu.VMEM((D_REC // LANES, SCAN_CHUNKS * SCAN_PITCH, LANES), _F32)] * 4 + [
        pltpu.VMEM((8, D_REC), _F32),
    ]
    return pl.pallas_call(
        functools.partial(_even_kernel, from_moe=from_moe),
        grid=(nb, n_s),
        in_specs=in_specs,
        out_specs=ep_out,
        out_shape=_epilogue_out_shapes(nb, seq),
        scratch_shapes=scratch,
        compiler_params=pltpu.CompilerParams(
            dimension_semantics=("arbitrary", "arbitrary"), vmem_limit_bytes=VMEM_LIMIT),
        name="even_mixer",
    )(*src_ops, m1, w_in.astype(_BF16), _block_diag(pool_w).astype(_BF16), pool_scale[None, :],
      conv_w, conv_b[None, :], wg, bg, lam[None, :], w_out.astype(_BF16), b_out[None, :],
      m2, ln, wr_hi, wr_lo, br, tri, dep)


def _odd_kernel(*refs, from_moe):
    x, refs = _mixer_input(refs, from_moe)
    (m1_ref, win_ref, bin_ref, sln_ref, ws_ref, bs_ref, wout_ref, bout_ref,
     m2_ref, ln_ref, wr_hi_ref, wr_lo_ref, br_ref, tri_ref, dep_ref,
     x1_ref, h2_ref, idx_ref, gate_ref, rank_ref, cnt_ref,
     sv_buf) = refs
    del dep_ref
    b = pl.program_id(0)
    s = pl.program_id(1)
    h = x * (1.0 + m1_ref[1:2, :]) + m1_ref[0:1, :]
    z = _gelu(_dot(h.astype(_BF16), win_ref[...]) + bin_ref[...])
    u = z[:, :D]
    v = _layer_norm(z[:, D:], sln_ref[0:1, :], sln_ref[1:2, :]).astype(_BF16)
    n_chunks = TS // CHUNK
    row = lax.broadcasted_iota(jnp.int32, (CHUNK, CHUNK), 0)
    col = lax.broadcasted_iota(jnp.int32, (CHUNK, CHUNK), 1)
    causal = col <= row
    for hh in range(N_SGU_HEADS):
        lanes = slice(hh * CHUNK, (hh + 1) * CHUNK)
        w_h = jnp.where(causal, ws_ref[hh], 0.0).astype(_BF16)
        v_h = jnp.concatenate(
            [v[n * CHUNK:(n + 1) * CHUNK, lanes] for n in range(n_chunks)], axis=1)
        sv_h = _dot(w_h, v_h)
        for n in range(n_chunks):
            sv_buf[n * CHUNK:(n + 1) * CHUNK, lanes] = (
                sv_h[:, n * CHUNK:(n + 1) * CHUNK] + bs_ref[hh])
    y = _dot((u * sv_buf[...]).astype(_BF16), wout_ref[...]) + bout_ref[...]
    first = jnp.logical_and(b == 0, s == 0)
    _finish_sublayer([(0, x, y)], m1_ref, ln_ref, m2_ref, wr_hi_ref, wr_lo_ref, br_ref, tri_ref,
                     first, x1_ref, h2_ref, idx_ref, gate_ref, rank_ref, cnt_ref)


def _odd_layer(src, span, dep, m1, m2, ln, router, tri, w_in, b_in, ln_g, ln_b, w_s, b_s, w_out,
               b_out):
    mb0, nb = span
    from_moe, seq, src_specs, src_ops = _source(src, mb0)
    n_s = seq // TS
    wr_hi, wr_lo, br = router
    sln = jnp.stack([ln_g, ln_b])
    bs = jnp.broadcast_to(b_s[:, :, None], (N_SGU_HEADS, CHUNK, CHUNK))
    ep_in, ep_out = _epilogue_specs(n_s, mb0)
    const = lambda shape: pl.BlockSpec(shape, lambda b, s: (0,) * len(shape))
    in_specs = src_specs + [
        pl.BlockSpec((None, 3, D), lambda b, s: (b + mb0, 0, 0)),
        const((D, 2 * D)), const((1, 2 * D)), const((2, D)),
        const((N_SGU_HEADS, CHUNK, CHUNK)), const((N_SGU_HEADS, CHUNK, CHUNK)),
        const((D, D)), const((1, D)),
    ] + ep_in
    return pl.pallas_call(
        functools.partial(_odd_kernel, from_moe=from_moe),
        grid=(nb, n_s),
        in_specs=in_specs,
        out_specs=ep_out,
        out_shape=_epilogue_out_shapes(nb, seq),
        scratch_shapes=[pltpu.VMEM((TS, D), _F32)],
        compiler_params=pltpu.CompilerParams(
            dimension_semantics=("arbitrary", "arbitrary"), vmem_limit_bytes=VMEM_LIMIT),
        name="odd_mixer",
    )(*src_ops, m1, w_in.astype(_BF16), b_in[None, :], sln, w_s, bs, w_out.astype(_BF16),
      b_out[None, :], m2, ln, wr_hi, wr_lo, br, tri, dep)


_BLOCK_SHIFT = EXPERT_BLOCK.bit_length() - 1
_TAB_ROWS = 8


def _blocks_kernel(cnt_ref, tab_ref, pstart_ref):
    n_lanes = tab_ref.shape[1]
    cnt = cnt_ref[:, 0:1].astype(jnp.int32)
    padded = ((cnt + (EXPERT_BLOCK - 1)) >> _BLOCK_SHIFT) << _BLOCK_SHIFT
    r = lax.broadcasted_iota(jnp.int32, (N_EXPERTS, N_EXPERTS), 0)
    c = lax.broadcasted_iota(jnp.int32, (N_EXPERTS, N_EXPERTS), 1)
    as_row = lambda col: jnp.sum(jnp.where(r == c, col, 0), axis=0, keepdims=True)
    pends = jnp.sum(jnp.where(c <= r, as_row(padded), 0), axis=1, keepdims=True)
    pstart = pends - padded
    total = jnp.sum(padded, axis=0, keepdims=True)
    later = jnp.where(c > r, as_row(cnt), 0) > 0
    nxt = jnp.min(jnp.where(later, c, N_EXPERTS), axis=1, keepdims=True)
    nxt = jnp.where(nxt < N_EXPERTS, nxt, -1)
    e_iota = lax.broadcasted_iota(jnp.int32, (N_EXPERTS, n_lanes), 0)
    start = lax.broadcasted_iota(jnp.int32, (N_EXPERTS, n_lanes), 1) << _BLOCK_SHIFT
    block_e = jnp.minimum(
        jnp.sum(jnp.where(start >= pends, 1, 0), axis=0, keepdims=True), N_EXPERTS - 1)
    mine = e_iota == block_e
    seg_end = jnp.sum(jnp.where(mine, pstart + cnt, 0), axis=0, keepdims=True)
    n_valid = jnp.clip(seg_end - start[0:1, :], 0, EXPERT_BLOCK)
    n_valid = jnp.where(start[0:1, :] < total, n_valid, 0)
    next_e = jnp.sum(jnp.where(mine, nxt, 0), axis=0, keepdims=True)
    n_used = jnp.broadcast_to(total >> _BLOCK_SHIFT, (1, n_lanes))
    zeros = jnp.zeros((_TAB_ROWS - 4, n_lanes), jnp.int32)
    tab_ref[...] = jnp.concatenate([block_e, n_valid, next_e, n_used, zeros], axis=0)
    pstart_ref[...] = jnp.broadcast_to(pstart, pstart_ref.shape)


def _dest_kernel(pstart_ref, idx_ref, rank_ref, dest_ref):
    idx = idx_ref[...]
    dest = rank_ref[...]
    for e in range(N_EXPERTS):
        dest = dest + jnp.where(idx == e, pstart_ref[e], 0)
    for ch in range(dest_ref.shape[0]):
        dest_ref[ch] = dest[:, ch * SC_CHUNK:(ch + 1) * SC_CHUNK]


def _plan(idx, rank, counts):
    t = idx.shape[1]
    n_blocks = (t * TOP_K) // EXPERT_BLOCK + N_EXPERTS
    n_lanes = -(-n_blocks // LANES) * LANES
    tab, pstart = pl.pallas_call(
        _blocks_kernel,
        out_shape=[jax.ShapeDtypeStruct((_TAB_ROWS, n_lanes), jnp.int32),
                   jax.ShapeDtypeStruct((N_EXPERTS, LANES), jnp.int32)],
        name="moe_blocks",
    )(counts)
    tl = min(t, 8192)
    dest3 = pl.pallas_call(
        _dest_kernel,
        grid_spec=pltpu.PrefetchScalarGridSpec(
            num_scalar_prefetch=1,
            grid=(t // tl,),
            in_specs=[pl.BlockSpec((TOP_K, tl), lambda i, p: (0, i)),
                      pl.BlockSpec((TOP_K, tl), lambda i, p: (0, i))],
            out_specs=pl.BlockSpec((tl // SC_CHUNK, TOP_K, SC_CHUNK), lambda i, p: (i, 0, 0)),
        ),
        out_shape=jax.ShapeDtypeStruct((t // SC_CHUNK, TOP_K, SC_CHUNK), jnp.int32),
        compiler_params=pltpu.CompilerParams(dimension_semantics=("arbitrary",)),
        name="moe_dest",
    )(pstart[:, 0], idx, rank)
    return dest3, (tab[0, :n_blocks], tab[1, :n_blocks], tab[3, :1], tab[2, :n_blocks])


def _sc_mesh():
    return plsc.VectorSubcoreMesh(core_axis_name="c", subcore_axis_name="s")


def _sc_dispatch(h2_rows, dest3, cap):
    t = h2_rows.shape[0]
    per_w = (t // SC_CHUNK) // (SC_CORES * SC_SUBCORES)

    @functools.partial(
        pl.kernel, mesh=_sc_mesh(),
        out_type=jax.ShapeDtypeStruct((cap, PACK_TILES, LANES), _ROWS),
        scratch_types=[pltpu.VMEM((TOP_K, SC_CHUNK), jnp.int32),
                       pltpu.VMEM((SC_CHUNK, PACK_TILES, LANES), _ROWS)],
        name="moe_dispatch",
    )
    def run(x_hbm, dest_hbm, out_hbm, idx_v, rows_v):
        wid = lax.axis_index("s") * SC_CORES + lax.axis_index("c")

        @pl.loop(0, per_w)
        def _(i):
            c = wid * per_w + i
            pltpu.sync_copy(dest_hbm.at[c], idx_v)
            pltpu.sync_copy(x_hbm.at[pl.ds(c * SC_CHUNK, SC_CHUNK)], rows_v)
            for k in range(TOP_K):
                pltpu.sync_copy(rows_v, out_hbm.at[idx_v.at[k]])

    return run(h2_rows, dest3)


def _sc_combine(ys_rows, dest3, t):
    per_w = (t // SC_CHUNK) // (SC_CORES * SC_SUBCORES)

    @functools.partial(
        pl.kernel, mesh=_sc_mesh(),
        out_type=jax.ShapeDtypeStruct((TOP_K, t, PACK_TILES, LANES), _ROWS),
        scratch_types=[pltpu.VMEM((TOP_K, SC_CHUNK), jnp.int32),
                       pltpu.VMEM((SC_CHUNK, PACK_TILES, LANES), _ROWS)],
        name="moe_combine",
    )
    def run(ys_hbm, dest_hbm, out_hbm, idx_v, rows_v):
        wid = lax.axis_index("s") * SC_CORES + lax.axis_index("c")

        @pl.loop(0, per_w)
        def _(i):
            c = wid * per_w + i
            pltpu.sync_copy(dest_hbm.at[c], idx_v)
            for k in range(TOP_K):
                pltpu.sync_copy(ys_hbm.at[idx_v.at[k]], rows_v)
                pltpu.sync_copy(rows_v, out_hbm.at[k, pl.ds(c * SC_CHUNK, SC_CHUNK)])

    return run(ys_rows, dest3)


def _ffn_kernel(be_ref, nv_ref, nu_ref, nx_ref, xs_ref, b1_ref, b2_ref, perm_ref, w1_hbm, w2_hbm,
                dep_ref, ys_ref, w1_f, w2_f, w1_s, w2_s, sems, *, layer):
    del dep_ref
    i = pl.program_id(0)
    active = i < nu_ref[0]
    e = be_ref[i]
    new_expert = jnp.logical_or(i == 0, e != be_ref[jnp.maximum(i - 1, 0)])

    def weight_copies(expert):
        return (pltpu.make_async_copy(w1_hbm.at[layer, expert], w1_f, sems.at[0]),
                pltpu.make_async_copy(w2_hbm.at[layer, expert], w2_f, sems.at[1]))

    @pl.when(jnp.logical_and(active, i == 0))
    def _():
        for cp in weight_copies(e):
            cp.start()

    @pl.when(jnp.logical_and(active, new_expert))
    def _():
        for cp in weight_copies(e):
            cp.wait()
        half = PERM // 2
        for n in range(2 * D_FF // PERM):
            blk = w1_f[:, n * PERM:(n + 1) * PERM].astype(_BF16)
            moved = _dot(blk, perm_ref[...]).astype(_BF16)
            w1_s[:, n * half:(n + 1) * half] = moved[:, :half]
            w1_s[:, D_FF + n * half:D_FF + (n + 1) * half] = moved[:, half:]
        w2_s[...] = w2_f[...].astype(_BF16)

        @pl.when(nx_ref[i] >= 0)
        def _():
            for cp in weight_copies(nx_ref[i]):
                cp.start()

    def ffn_rows(n_rows):
        x = _load_rows(xs_ref, n_rows)
        rows = lax.broadcasted_iota(jnp.int32, (n_rows, 1), 0)
        x = jnp.where(rows < nv_ref[i], x, 0.0).astype(_BF16)
        hb = _dot(x, w1_s[...]) + b1_ref[...]
        x_glu = jnp.minimum(hb[:, :D_FF], SWIGLU_LIMIT)
        x_lin = jnp.clip(hb[:, D_FF:], -SWIGLU_LIMIT, SWIGLU_LIMIT)
        act = x_glu * jax.nn.sigmoid(SWIGLU_ALPHA * x_glu) * (x_lin + 1.0)
        y = _dot(act.astype(_BF16), w2_s[...]) + b2_ref[...]
        _store_rows(ys_ref, 0, y)

    half_full = nv_ref[i] <= EXPERT_BLOCK // 2
    pl.when(jnp.logical_and(active, jnp.logical_not(half_full)))(
        functools.partial(ffn_rows, EXPERT_BLOCK))
    pl.when(jnp.logical_and(active, half_full))(
        functools.partial(ffn_rows, EXPERT_BLOCK // 2))


def _expert_ffn(xs_rows, blocks, w1, b1, w2, b2, layer, dep):
    block_e, n_valid, n_used, next_e = blocks
    n_blocks = block_e.shape[0]
    half = PERM // 2
    src = jnp.arange(PERM)
    dst = jnp.where(src % 2 == 0, src // 2, half + src // 2)
    perm = (dst[:, None] == jnp.arange(PERM)[None, :]).astype(_BF16)
    blk = lambda i, be, nv, nu, nx: (jnp.minimum(i, nu[0] - 1), 0)
    bsel = lambda i, be, nv, nu, nx: (layer, be[jnp.minimum(i, nu[0] - 1)], 0, 0)
    return pl.pallas_call(
        functools.partial(_ffn_kernel, layer=layer),
        grid_spec=pltpu.PrefetchScalarGridSpec(
            num_scalar_prefetch=4,
            grid=(n_blocks,),
            in_specs=[
                pl.BlockSpec((EXPERT_BLOCK * PACK_TILES, LANES), blk),
                pl.BlockSpec((None, None, 1, 2 * D_FF), bsel),
                pl.BlockSpec((None, None, 1, D), bsel),
                pl.BlockSpec((PERM, PERM), lambda i, be, nv, nu, nx: (0, 0)),
                pl.BlockSpec(memory_space=pl.ANY),
                pl.BlockSpec(memory_space=pl.ANY),
                pl.BlockSpec(memory_space=pl.ANY),
            ],
            out_specs=pl.BlockSpec((EXPERT_BLOCK * PACK_TILES, LANES), blk),
            scratch_shapes=[
                pltpu.VMEM((D, 2 * D_FF), _F32), pltpu.VMEM((D_FF, D), _F32),
                pltpu.VMEM((D, 2 * D_FF), _BF16), pltpu.VMEM((D_FF, D), _BF16),
                pltpu.SemaphoreType.DMA((2,)),
            ],
        ),
        out_shape=jax.ShapeDtypeStruct(xs_rows.shape, _ROWS),
        compiler_params=pltpu.CompilerParams(
            dimension_semantics=("arbitrary",), vmem_limit_bytes=VMEM_LIMIT),
        name="expert_ffn",
    )(block_e, n_valid, n_used, next_e, xs_rows, b1, b2, perm, w1, w2, dep)


def _post_kernel(x1_ref, yg_ref, gate_ref, m2_ref, ln_ref, prev_ref, o_ref):
    del prev_ref
    o_ref[...] = _moe_output(x1_ref, yg_ref, gate_ref, m2_ref, ln_ref)


def _moe_post(src, mb0, total_nb, prev, keep_prev):
    nb, seq, _ = src[1].shape
    n_s = seq // TS
    out_shape = jax.ShapeDtypeStruct((total_nb, seq, D), _F32)
    aliases = {5: 0} if keep_prev else {}
    return pl.pallas_call(
        _post_kernel,
        grid=(nb, n_s),
        in_specs=_moe_output_specs(n_s, mb0) + [pl.BlockSpec(memory_space=pl.ANY)],
        out_specs=pl.BlockSpec((None, TS, D), lambda b, s: (b + mb0, s, 0)),
        out_shape=out_shape,
        input_output_aliases=aliases,
        compiler_params=pltpu.CompilerParams(
            dimension_semantics=("arbitrary", "arbitrary"), vmem_limit_bytes=VMEM_LIMIT),
        name="moe_post",
    )(*src[1:], prev)


def _moe_layer(mixed, m2, ln, w1, b1, w2, b2, layer):
    n = len(mixed)
    nb, seq, _ = mixed[0][0].shape
    t = nb * seq
    cap = (t * TOP_K // EXPERT_BLOCK + N_EXPERTS) * EXPERT_BLOCK
    dest3s, yss = [], []
    staged = []
    for x1, h2_rows, idx, gates, rank, counts in mixed:
        dest3, blocks = _plan(idx, rank, counts)
        xs = _sc_dispatch(h2_rows.reshape(t, PACK_TILES, LANES), dest3, cap)
        staged.append((xs, blocks))
        dest3s.append(dest3)
    for g, (xs, blocks) in enumerate(staged):
        dep = mixed[n - 1][5] if g == 0 else yss[g - 1]
        yss.append(_expert_ffn(xs.reshape(cap * PACK_TILES, LANES), blocks, w1, b1, w2, b2,
                               layer, dep))
    srcs = []
    for g, (ys, dest3) in enumerate(zip(yss, dest3s)):
        yg = _sc_combine(ys.reshape(cap, PACK_TILES, LANES), dest3, t)
        srcs.append(('moe', mixed[g][0], yg.reshape(TOP_K, t * PACK_TILES, LANES), mixed[g][3],
                     m2, ln))
    return srcs, yss[n - 1]


def kernel(x, c, mod_w, mod_b, ln_g, ln_b, ab_w_in, pool_w, pool_scale, conv_w, conv_b, lru_w_a, lru_b_a, lru_w_x, lru_b_x, lru_lambda, ab_w_out, ab_b_out, sgu_w_in, sgu_b_in, sgu_ln_g, sgu_ln_b, sgu_w_s, sgu_b_s, sgu_w_out, sgu_b_out, router_w, router_b, moe_w1, moe_b1, moe_w2, moe_b2):
    assert x.shape[1] % TS == 0 and x.shape[2] == D
    mods = _modulation(c, mod_w, mod_b)
    tri = (lax.broadcasted_iota(jnp.int32, (TS, TS), 0)
           < lax.broadcasted_iota(jnp.int32, (TS, TS), 1)).astype(_BF16)
    b1 = jnp.concatenate([moe_b1[..., 0::2], moe_b1[..., 1::2]], axis=-1)[:, :, None, :]
    b2 = moe_b2[:, :, None, :]
    nb = x.shape[0]
    n_groups = 2 if nb % 2 == 0 else 1
    gb = nb // n_groups
    srcs = [('raw', x, g * gb) for g in range(n_groups)]
    order = tri
    for layer in range(DEPTH):
        j = layer // 2
        m1, m2 = mods[layer, 0], mods[layer, 1]
        ln1 = jnp.stack([ln_g[layer, 0], ln_b[layer, 0]])
        ln2 = jnp.stack([ln_g[layer, 1], ln_b[layer, 1]])
        wr_hi, wr_lo = _split(router_w[layer].T)
        router = (wr_hi, wr_lo, router_b[layer][:, None])
        mixed = []
        for g in range(n_groups):
            span = (g * gb, gb)
            if layer % 2 == 0:
                outs = _even_layer(srcs[g], span, order, m1, m2, ln1, router, tri, ab_w_in[j],
                                   pool_w[j], pool_scale[j], conv_w[j], conv_b[j], lru_w_a[j],
                                   lru_b_a[j], lru_w_x[j], lru_b_x[j], lru_lambda[j], ab_w_out[j],
                                   ab_b_out[j])
            else:
                outs = _odd_layer(srcs[g], span, order, m1, m2, ln1, router, tri, sgu_w_in[j],
                                  sgu_b_in[j], sgu_ln_g[j], sgu_ln_b[j], sgu_w_s[j], sgu_b_s[j],
                                  sgu_w_out[j], sgu_b_out[j])
            mixed.append(outs)
            order = outs[5]
        srcs, order = _moe_layer(mixed, m2, ln2, moe_w1, b1, moe_w2, b2, layer)
    out = order
    for g in range(n_groups):
        out = _moe_post(srcs[g], g * gb, nb, out, keep_prev=g > 0)
    return out
```

```python
import functools

import jax
import jax.numpy as jnp
from jax import lax
from jax.experimental import pallas as pl
from jax.experimental.pallas import tpu as pltpu
from jax.experimental.pallas import tpu_sc as plsc

D = 1024
DEPTH = 4
POOL_WINDOWS = (2, 4, 8, 16)
POOL_GROUP = 128
D_POOL = 512
D_REC = 512
N_REC_HEADS = 8
REC_HEAD = 64
CONV_WIDTH = 4
LRU_C = 8.0
D_IN_AB = 1536
CHUNK = 128
N_SGU_HEADS = 8
N_EXPERTS = 32
TOP_K = 4
D_FF = 1024
SWIGLU_LIMIT = 7.0
SWIGLU_ALPHA = 1.702
EXPERT_BLOCK = 512
LN_EPS = 1e-5
ALPHA = (2 * DEPTH) ** 0.25

TS = 512
LANES = 128
PACK_TILES = D // (2 * LANES)
_ROWS = jnp.uint32
POOL_PAD = 32
CONV_PAD = 8
SCAN_CHUNKS = 8
SCAN_LEN = TS // SCAN_CHUNKS
SCAN_PITCH = SCAN_LEN + 8
SC_CORES = 2
SC_SUBCORES = 16
SC_CHUNK = 128
PERM = 256
VMEM_LIMIT = 56 * 1024 * 1024

_F32 = jnp.float32
_BF16 = jnp.bfloat16
_NT = (((1,), (1,)), ((), ()))


def _split(a):
    hi = a.astype(_BF16)
    lo = (a - hi.astype(_F32)).astype(_BF16)
    return hi, lo


def _dot(a, b):
    return jnp.dot(a, b, preferred_element_type=_F32)


def _layer_norm(x, g, b):
    mu = jnp.mean(x, axis=-1, keepdims=True)
    xc = x - mu
    var = jnp.mean(xc * xc, axis=-1, keepdims=True)
    return xc * lax.rsqrt(var + LN_EPS) * g + b


def _gelu(x):
    return jax.nn.gelu(x)


def _mod_kernel(c_ref, w_ref, b_ref, o_ref):
    c = c_ref[...]
    ca = c * jax.nn.sigmoid(c)
    a_hi, a_lo = _split(ca)
    w_hi, w_lo = _split(w_ref[...])
    o_ref[...] = _dot(a_hi, w_hi) + _dot(a_lo, w_hi) + _dot(a_hi, w_lo) + b_ref[...]


def _modulation(c, mod_w, mod_b):
    nb = c.shape[0]
    n = DEPTH * 2
    w = mod_w.reshape(n, D, 3 * D)
    b = mod_b.reshape(n, 1, 3 * D)
    out = pl.pallas_call(
        _mod_kernel,
        grid=(n, 3),
        in_specs=[
            pl.BlockSpec((nb, D), lambda i, j: (0, 0)),
            pl.BlockSpec((None, D, D), lambda i, j: (i, 0, j)),
            pl.BlockSpec((None, 1, D), lambda i, j: (i, 0, j)),
        ],
        out_specs=pl.BlockSpec((None, nb, D), lambda i, j: (i, 0, j)),
        out_shape=jax.ShapeDtypeStruct((n, nb, 3 * D), _F32),
        compiler_params=pltpu.CompilerParams(
            dimension_semantics=("arbitrary", "arbitrary"), vmem_limit_bytes=VMEM_LIMIT),
        name="modulation",
    )(c, w, b)
    return out.reshape(DEPTH, 2, nb, 3, D)


def _round_bf16(val):
    return val.astype(_BF16).astype(_F32)


def _store_rows(ref, r0, rounded):
    rows = rounded.shape[0]
    for q in range(PACK_TILES):
        lo = rounded[:, q * LANES:(q + 1) * LANES]
        hi = rounded[:, D // 2 + q * LANES:D // 2 + (q + 1) * LANES]
        word = (pltpu.bitcast(lo, jnp.uint32) >> 16) | pltpu.bitcast(hi, jnp.uint32)
        ref[pl.ds(r0 * PACK_TILES + q, rows, stride=PACK_TILES), :] = word


def _load_rows(ref, rows):
    lo, hi = [], []
    for q in range(PACK_TILES):
        word = ref[pl.ds(q, rows, stride=PACK_TILES), :]
        lo.append(pltpu.bitcast(word << 16, _F32))
        hi.append(pltpu.bitcast(word & jnp.uint32(0xFFFF0000), _F32))
    return jnp.concatenate(lo + hi, axis=1)


def _moe_output(x1_ref, yg_ref, gate_ref, m2_ref, ln_ref):
    gt = gate_ref[...]
    pad = jnp.zeros((LANES - TOP_K, TS), _F32)
    g_rows = jnp.concatenate([gt, pad], axis=0).T
    y = jnp.zeros((TS, D), _F32)
    for k in range(TOP_K):
        y = y + g_rows[:, k:k + 1] * _load_rows(yg_ref.at[k], TS)
    gate2 = 1.0 + m2_ref[2:3, :]
    return _layer_norm(ALPHA * x1_ref[...] + gate2 * y, ln_ref[0:1, :], ln_ref[1:2, :])


def _moe_output_specs(n_seq_tiles, mb0):
    tile = lambda b, s: b * n_seq_tiles + s
    return [
        pl.BlockSpec((None, TS, D), lambda b, s: (b, s, 0)),
        pl.BlockSpec((TOP_K, TS * PACK_TILES, LANES), lambda b, s: (0, tile(b, s), 0)),
        pl.BlockSpec((TOP_K, TS), lambda b, s: (0, tile(b, s))),
        pl.BlockSpec((None, 3, D), lambda b, s: (b + mb0, 0, 0)),
        pl.BlockSpec((2, D), lambda b, s: (0, 0)),
    ]


def _mixer_input(refs, from_moe):
    if from_moe:
        return _moe_output(*refs[:5]), refs[5:]
    return refs[0][...], refs[1:]


def _finish_sublayer(parts, m1_ref, ln_ref, m2_ref, wr_hi_ref, wr_lo_ref, br_ref, tri_ref, first,
                     x1_ref, h2_ref, idx_ref, gate_ref, rank_ref, cnt_ref):
    @pl.when(first)
    def _():
        cnt_ref[...] = jnp.zeros_like(cnt_ref)

    gate1 = 1.0 + m1_ref[2:3, :]
    w_hi = wr_hi_ref[...]
    w_both = jnp.concatenate([w_hi, wr_lo_ref[...]], axis=0)
    seen = cnt_ref[:, 0:1]
    for r0, x, y in parts:
        n = x.shape[0]
        x1 = _layer_norm(ALPHA * x + gate1 * y, ln_ref[0:1, :], ln_ref[1:2, :])
        x1_ref[r0:r0 + n, :] = x1
        h2 = x1 * (1.0 + m2_ref[1:2, :]) + m2_ref[0:1, :]
        h2_r = _round_bf16(h2)
        _store_rows(h2_ref, r0, h2_r)

        h_hi = h2_r.astype(_BF16)
        h_lo = (h2 - h2_r).astype(_BF16)
        both = lax.dot_general(w_both, h_hi, _NT, preferred_element_type=_F32)
        logits = (both[:N_EXPERTS] + both[N_EXPERTS:]
                  + lax.dot_general(w_hi, h_lo, _NT, preferred_element_type=_F32)
                  + br_ref[...])
        iota_e = lax.broadcasted_iota(jnp.int32, (N_EXPERTS, n), 0)
        work = logits
        vals, idxs, hots = [], [], []
        for _ in range(TOP_K):
            m = jnp.max(work, axis=0, keepdims=True)
            idx = jnp.min(jnp.where(work == m, iota_e, N_EXPERTS), axis=0, keepdims=True)
            hot = iota_e == idx
            work = jnp.where(hot, -jnp.inf, work)
            vals.append(m)
            idxs.append(idx)
            hots.append(hot)
        exps = [jnp.exp(v - vals[0]) for v in vals]
        denom = exps[0] + exps[1] + exps[2] + exps[3]
        gate_ref[:, r0:r0 + n] = jnp.concatenate([e / denom for e in exps], axis=0)
        idx_ref[:, r0:r0 + n] = jnp.concatenate(idxs, axis=0)

        sel = jnp.zeros((N_EXPERTS, n), _F32)
        for hot in hots:
            sel = sel + jnp.where(hot, 1.0, 0.0)
        before = _dot(sel.astype(_BF16), tri_ref[0:n, 0:n]) + seen
        ranks = [jnp.sum(jnp.where(hot, before, 0.0), axis=0, keepdims=True) for hot in hots]
        rank_ref[:, r0:r0 + n] = jnp.concatenate(ranks, axis=0).astype(jnp.int32)
        seen = seen + jnp.sum(sel, axis=1, keepdims=True)
    cnt_ref[...] = jnp.broadcast_to(seen, cnt_ref.shape)


def _epilogue_specs(n_seq_tiles, mb0):
    tile = lambda b, s: b * n_seq_tiles + s
    in_specs = [
        pl.BlockSpec((None, 3, D), lambda b, s: (b + mb0, 0, 0)),
        pl.BlockSpec((2, D), lambda b, s: (0, 0)),
        pl.BlockSpec((N_EXPERTS, D), lambda b, s: (0, 0)),
        pl.BlockSpec((N_EXPERTS, D), lambda b, s: (0, 0)),
        pl.BlockSpec((N_EXPERTS, 1), lambda b, s: (0, 0)),
        pl.BlockSpec((TS, TS), lambda b, s: (0, 0)),
        pl.BlockSpec(memory_space=pl.ANY),
    ]
    out_specs = [
        pl.BlockSpec((None, TS, D), lambda b, s: (b, s, 0)),
        pl.BlockSpec((TS * PACK_TILES, LANES), lambda b, s: (tile(b, s), 0)),
        pl.BlockSpec((TOP_K, TS), lambda b, s: (0, tile(b, s))),
        pl.BlockSpec((TOP_K, TS), lambda b, s: (0, tile(b, s))),
        pl.BlockSpec((TOP_K, TS), lambda b, s: (0, tile(b, s))),
        pl.BlockSpec((N_EXPERTS, LANES), lambda b, s: (0, 0)),
    ]
    return in_specs, out_specs


def _epilogue_out_shapes(nb, seq):
    t = nb * seq
    return [
        jax.ShapeDtypeStruct((nb, seq, D), _F32),
        jax.ShapeDtypeStruct((t * PACK_TILES, LANES), _ROWS),
        jax.ShapeDtypeStruct((TOP_K, t), jnp.int32),
        jax.ShapeDtypeStruct((TOP_K, t), _F32),
        jax.ShapeDtypeStruct((TOP_K, t), jnp.int32),
        jax.ShapeDtypeStruct((N_EXPERTS, LANES), _F32),
    ]


def _even_kernel(*refs, from_moe):
    x, refs = _mixer_input(refs, from_moe)
    (m1_ref, win_ref, poolw_ref, pscale_ref, convw_ref, convb_ref, wg_ref, bg_ref, lam_ref,
     wout_ref, bout_ref,
     m2_ref, ln_ref, wr_hi_ref, wr_lo_ref, br_ref, tri_ref, dep_ref,
     x1_ref, h2_ref, idx_ref, gate_ref, rank_ref, cnt_ref,
     p1, p2, p4, p8, cbuf, s_a, s_b, s_h, s_p, hstate) = refs
    del dep_ref
    b = pl.program_id(0)
    s = pl.program_id(1)
    h = x * (1.0 + m1_ref[1:2, :]) + m1_ref[0:1, :]
    z = _dot(h.astype(_BF16), win_ref[...])
    xp = z[:, :D_POOL]
    xr = z[:, D_POOL:D_POOL + D_REC]
    gr = z[:, D_POOL + D_REC:]

    @pl.when(s == 0)
    def _():
        p1[0:POOL_PAD, :] = jnp.zeros((POOL_PAD, D_POOL), _F32)
        p2[0:POOL_PAD, :] = jnp.zeros((POOL_PAD, D_POOL), _F32)
        p4[0:POOL_PAD, :] = jnp.zeros((POOL_PAD, 3 * POOL_GROUP), _F32)
        p8[0:POOL_PAD, :] = jnp.zeros((POOL_PAD, 2 * POOL_GROUP), _F32)
        cbuf[0:CONV_PAD, :] = jnp.zeros((CONV_PAD, D_REC), _F32)
        hstate[...] = jnp.zeros_like(hstate)

    rr = TS + 16
    p1[POOL_PAD:, :] = xp
    p2[16:, :] = p1[16:16 + rr, :] + p1[15:15 + rr, :]
    p4[16:, :] = p2[16:16 + rr, POOL_GROUP:] + p2[14:14 + rr, POOL_GROUP:]
    p8[16:, :] = p4[16:16 + rr, POOL_GROUP:] + p4[12:12 + rr, POOL_GROUP:]
    s16 = p8[POOL_PAD:, POOL_GROUP:] + p8[POOL_PAD - 8:POOL_PAD - 8 + TS, POOL_GROUP:]
    sums = (p2[POOL_PAD:, 0:POOL_GROUP], p4[POOL_PAD:, 0:POOL_GROUP],
            p8[POOL_PAD:, 0:POOL_GROUP], s16)
    pos = (s * TS + lax.broadcasted_iota(jnp.int32, (TS, 1), 0) + 1).astype(_F32)
    pooled = []
    for g, w in enumerate(POOL_WINDOWS):
        cnt = jnp.minimum(pos, float(w))
        pooled.append(sums[g] / cnt - xp[:, g * POOL_GROUP:(g + 1) * POOL_GROUP])
    pooled = jnp.concatenate(pooled, axis=1).astype(_BF16)
    y_pool = _dot(pooled, poolw_ref[...]) * pscale_ref[...]
    p1[16:POOL_PAD, :] = p1[TS + 16:TS + POOL_PAD, :]

    cbuf[CONV_PAD:, :] = xr
    xc = convb_ref[...]
    for k in range(CONV_WIDTH):
        off = CONV_PAD - (CONV_WIDTH - 1) + k
        xc = xc + cbuf[off:off + TS, :] * convw_ref[k:k + 1, :]
    cbuf[0:CONV_PAD, :] = cbuf[TS:TS + CONV_PAD, :]
    gates = _dot(xc.astype(_BF16), wg_ref[...]) + bg_ref[...]
    r_gate = jax.nn.sigmoid(gates[:, :D_REC])
    i_gate = jax.nn.sigmoid(gates[:, D_REC:])
    log_a = -LRU_C * r_gate * jax.nn.softplus(-lam_ref[...])
    a = jnp.exp(log_a)
    mult = jnp.sqrt(-jnp.tanh(log_a) * (a * a + 1.0))
    b_in = mult * (i_gate * xc)

    lane_tiles = [slice(lt * LANES, (lt + 1) * LANES) for lt in range(D_REC // LANES)]
    for lt, lanes in enumerate(lane_tiles):
        for ch in range(SCAN_CHUNKS):
            dst = pl.ds(ch * SCAN_PITCH, SCAN_LEN)
            s_a[lt, dst, :] = a[ch * SCAN_LEN:(ch + 1) * SCAN_LEN, lanes]
            s_b[lt, dst, :] = b_in[ch * SCAN_LEN:(ch + 1) * SCAN_LEN, lanes]
    h_loc = [jnp.zeros((SCAN_CHUNKS, LANES), _F32) for _ in lane_tiles]
    decay = [jnp.ones((SCAN_CHUNKS, LANES), _F32) for _ in lane_tiles]
    for step in range(SCAN_LEN):
        rows = pl.ds(step, SCAN_CHUNKS, stride=SCAN_PITCH)
        for lt in range(len(lane_tiles)):
            a_s = s_a[lt, rows, :]
            h_loc[lt] = a_s * h_loc[lt] + s_b[lt, rows, :]
            decay[lt] = a_s * decay[lt]
            s_h[lt, rows, :] = h_loc[lt]
            s_p[lt, rows, :] = decay[lt]
    cols = []
    for lt, lanes in enumerate(lane_tiles):
        carry = hstate[0:1, lanes]
        chunks = []
        for ch in range(SCAN_CHUNKS):
            src = pl.ds(ch * SCAN_PITCH, SCAN_LEN)
            chunks.append(s_h[lt, src, :] + s_p[lt, src, :] * carry)
            carry = decay[lt][ch:ch + 1, :] * carry + h_loc[lt][ch:ch + 1, :]
        hstate[0:1, lanes] = carry
        cols.append(jnp.concatenate(chunks, axis=0))
    h_rec = jnp.concatenate(cols, axis=1)
    y_rec = h_rec * _gelu(gr)

    y_mix = jnp.concatenate([y_pool, y_rec], axis=1).astype(_BF16)
    y = _dot(y_mix, wout_ref[...]) + bout_ref[...]
    first = jnp.logical_and(b == 0, s == 0)
    _finish_sublayer([(0, x, y)], m1_ref, ln_ref, m2_ref, wr_hi_ref, wr_lo_ref, br_ref, tri_ref,
                     first, x1_ref, h2_ref, idx_ref, gate_ref, rank_ref, cnt_ref)


def _block_diag(w):
    n, k, _ = w.shape
    eye = jnp.eye(n, dtype=w.dtype)
    return (eye[:, None, :, None] * w[:, :, None, :]).reshape(n * k, n * k)


def _source(src, mb0):
    if src[0] == 'raw':
        _, x, xb0 = src
        return False, x.shape[1], [pl.BlockSpec((None, TS, D), lambda b, s: (b + xb0, s, 0))], [x]
    seq = src[1].shape[1]
    return True, seq, _moe_output_specs(seq // TS, mb0), list(src[1:])


def _even_layer(src, span, dep, m1, m2, ln, router, tri, w_in, pool_w, pool_scale, conv_w, conv_b,
                w_a, b_a, w_x, b_x, lam, w_out, b_out):
    mb0, nb = span
    from_moe, seq, src_specs, src_ops = _source(src, mb0)
    n_s = seq // TS
    wr_hi, wr_lo, br = router
    wg = jnp.concatenate([_block_diag(w_a), _block_diag(w_x)], axis=1).astype(_BF16)
    bg = jnp.concatenate([b_a, b_x])[None, :]
    ep_in, ep_out = _epilogue_specs(n_s, mb0)
    const = lambda shape: pl.BlockSpec(shape, lambda b, s: (0,) * len(shape))
    in_specs = src_specs + [
        pl.BlockSpec((None, 3, D), lambda b, s: (b + mb0, 0, 0)),
        const((D, D_IN_AB)), const((D_POOL, D_POOL)), const((1, D_POOL)),
        const((CONV_WIDTH, D_REC)), const((1, D_REC)), const((D_REC, 2 * D_REC)),
        const((1, 2 * D_REC)), const((1, D_REC)), const((D, D)), const((1, D)),
    ] + ep_in
    scratch = [
        pltpu.VMEM((TS + POOL_PAD, D_POOL), _F32),
        pltpu.VMEM((TS + POOL_PAD, D_POOL), _F32),
        pltpu.VMEM((TS + POOL_PAD, 3 * POOL_GROUP), _F32),
        pltpu.VMEM((TS + POOL_PAD, 2 * POOL_GROUP), _F32),
        pltpu.VMEM((TS + CONV_PAD, D_REC), _F32),
    ] + [pltpu.VMEM((D_REC // LANES, SCAN_CHUNKS * SCAN_PITCH, LANES), _F32)] * 4 + [
        pltpu.VMEM((8, D_REC), _F32),
    ]
    return pl.pallas_call(
        functools.partial(_even_kernel, from_moe=from_moe),
        grid=(nb, n_s),
        in_specs=in_specs,
        out_specs=ep_out,
        out_shape=_epilogue_out_shapes(nb, seq),
        scratch_shapes=scratch,
        compiler_params=pltpu.CompilerParams(
            dimension_semantics=("arbitrary", "arbitrary"), vmem_limit_bytes=VMEM_LIMIT),
        name="even_mixer",
    )(*src_ops, m1, w_in.astype(_BF16), _block_diag(pool_w).astype(_BF16), pool_scale[None, :],
      conv_w, conv_b[None, :], wg, bg, lam[None, :], w_out.astype(_BF16), b_out[None, :],
      m2, ln, wr_hi, wr_lo, br, tri, dep)


def _odd_kernel(*refs, from_moe):
    x, refs = _mixer_input(refs, from_moe)
    (m1_ref, win_ref, bin_ref, sln_ref, ws_ref, bs_ref, wout_ref, bout_ref,
     m2_ref, ln_ref, wr_hi_ref, wr_lo_ref, br_ref, tri_ref, dep_ref,
     x1_ref, h2_ref, idx_ref, gate_ref, rank_ref, cnt_ref,
     sv_buf) = refs
    del dep_ref
    b = pl.program_id(0)
    s = pl.program_id(1)
    h = x * (1.0 + m1_ref[1:2, :]) + m1_ref[0:1, :]
    z = _gelu(_dot(h.astype(_BF16), win_ref[...]) + bin_ref[...])
    u = z[:, :D]
    v = _layer_norm(z[:, D:], sln_ref[0:1, :], sln_ref[1:2, :]).astype(_BF16)
    n_chunks = TS // CHUNK
    row = lax.broadcasted_iota(jnp.int32, (CHUNK, CHUNK), 0)
    col = lax.broadcasted_iota(jnp.int32, (CHUNK, CHUNK), 1)
    causal = col <= row
    for hh in range(N_SGU_HEADS):
        lanes = slice(hh * CHUNK, (hh + 1) * CHUNK)
        w_h = jnp.where(causal, ws_ref[hh], 0.0).astype(_BF16)
        v_h = jnp.concatenate(
            [v[n * CHUNK:(n + 1) * CHUNK, lanes] for n in range(n_chunks)], axis=1)
        sv_h = _dot(w_h, v_h)
        for n in range(n_chunks):
            sv_buf[n * CHUNK:(n + 1) * CHUNK, lanes] = (
                sv_h[:, n * CHUNK:(n + 1) * CHUNK] + bs_ref[hh])
    y = _dot((u * sv_buf[...]).astype(_BF16), wout_ref[...]) + bout_ref[...]
    first = jnp.logical_and(b == 0, s == 0)
    _finish_sublayer([(0, x, y)], m1_ref, ln_ref, m2_ref, wr_hi_ref, wr_lo_ref, br_ref, tri_ref,
                     first, x1_ref, h2_ref, idx_ref, gate_ref, rank_ref, cnt_ref)


def _odd_layer(src, span, dep, m1, m2, ln, router, tri, w_in, b_in, ln_g, ln_b, w_s, b_s, w_out,
               b_out):
    mb0, nb = span
    from_moe, seq, src_specs, src_ops = _source(src, mb0)
    n_s = seq // TS
    wr_hi, wr_lo, br = router
    sln = jnp.stack([ln_g, ln_b])
    bs = jnp.broadcast_to(b_s[:, :, None], (N_SGU_HEADS, CHUNK, CHUNK))
    ep_in, ep_out = _epilogue_specs(n_s, mb0)
    const = lambda shape: pl.BlockSpec(shape, lambda b, s: (0,) * len(shape))
    in_specs = src_specs + [
        pl.BlockSpec((None, 3, D), lambda b, s: (b + mb0, 0, 0)),
        const((D, 2 * D)), const((1, 2 * D)), const((2, D)),
        const((N_SGU_HEADS, CHUNK, CHUNK)), const((N_SGU_HEADS, CHUNK, CHUNK)),
        const((D, D)), const((1, D)),
    ] + ep_in
    return pl.pallas_call(
        functools.partial(_odd_kernel, from_moe=from_moe),
        grid=(nb, n_s),
        in_specs=in_specs,
        out_specs=ep_out,
        out_shape=_epilogue_out_shapes(nb, seq),
        scratch_shapes=[pltpu.VMEM((TS, D), _F32)],
        compiler_params=pltpu.CompilerParams(
            dimension_semantics=("arbitrary", "arbitrary"), vmem_limit_bytes=VMEM_LIMIT),
        name="odd_mixer",
    )(*src_ops, m1, w_in.astype(_BF16), b_in[None, :], sln, w_s, bs, w_out.astype(_BF16),
      b_out[None, :], m2, ln, wr_hi, wr_lo, br, tri, dep)


_BLOCK_SHIFT = EXPERT_BLOCK.bit_length() - 1
_TAB_ROWS = 8


def _blocks_kernel(cnt_ref, tab_ref, pstart_ref):
    n_lanes = tab_ref.shape[1]
    cnt = cnt_ref[:, 0:1].astype(jnp.int32)
    padded = ((cnt + (EXPERT_BLOCK - 1)) >> _BLOCK_SHIFT) << _BLOCK_SHIFT
    r = lax.broadcasted_iota(jnp.int32, (N_EXPERTS, N_EXPERTS), 0)
    c = lax.broadcasted_iota(jnp.int32, (N_EXPERTS, N_EXPERTS), 1)
    as_row = lambda col: jnp.sum(jnp.where(r == c, col, 0), axis=0, keepdims=True)
    pends = jnp.sum(jnp.where(c <= r, as_row(padded), 0), axis=1, keepdims=True)
    pstart = pends - padded
    total = jnp.sum(padded, axis=0, keepdims=True)
    later = jnp.where(c > r, as_row(cnt), 0) > 0
    nxt = jnp.min(jnp.where(later, c, N_EXPERTS), axis=1, keepdims=True)
    nxt = jnp.where(nxt < N_EXPERTS, nxt, -1)
    e_iota = lax.broadcasted_iota(jnp.int32, (N_EXPERTS, n_lanes), 0)
    start = lax.broadcasted_iota(jnp.int32, (N_EXPERTS, n_lanes), 1) << _BLOCK_SHIFT
    block_e = jnp.minimum(
        jnp.sum(jnp.where(start >= pends, 1, 0), axis=0, keepdims=True), N_EXPERTS - 1)
    mine = e_iota == block_e
    seg_end = jnp.sum(jnp.where(mine, pstart + cnt, 0), axis=0, keepdims=True)
    n_valid = jnp.clip(seg_end - start[0:1, :], 0, EXPERT_BLOCK)
    n_valid = jnp.where(start[0:1, :] < total, n_valid, 0)
    next_e = jnp.sum(jnp.where(mine, nxt, 0), axis=0, keepdims=True)
    n_used = jnp.broadcast_to(total >> _BLOCK_SHIFT, (1, n_lanes))
    zeros = jnp.zeros((_TAB_ROWS - 4, n_lanes), jnp.int32)
    tab_ref[...] = jnp.concatenate([block_e, n_valid, next_e, n_used, zeros], axis=0)
    pstart_ref[...] = jnp.broadcast_to(pstart, pstart_ref.shape)


def _dest_kernel(pstart_ref, idx_ref, rank_ref, dest_ref):
    idx = idx_ref[...]
    dest = rank_ref[...]
    for e in range(N_EXPERTS):
        dest = dest + jnp.where(idx == e, pstart_ref[e], 0)
    for ch in range(dest_ref.shape[0]):
        dest_ref[ch] = dest[:, ch * SC_CHUNK:(ch + 1) * SC_CHUNK]


def _plan(idx, rank, counts):
    t = idx.shape[1]
    n_blocks = (t * TOP_K) // EXPERT_BLOCK + N_EXPERTS
    n_lanes = -(-n_blocks // LANES) * LANES
    tab, pstart = pl.pallas_call(
        _blocks_kernel,
        out_shape=[jax.ShapeDtypeStruct((_TAB_ROWS, n_lanes), jnp.int32),
                   jax.ShapeDtypeStruct((N_EXPERTS, LANES), jnp.int32)],
        name="moe_blocks",
    )(counts)
    tl = min(t, 8192)
    dest3 = pl.pallas_call(
        _dest_kernel,
        grid_spec=pltpu.PrefetchScalarGridSpec(
            num_scalar_prefetch=1,
            grid=(t // tl,),
            in_specs=[pl.BlockSpec((TOP_K, tl), lambda i, p: (0, i)),
                      pl.BlockSpec((TOP_K, tl), lambda i, p: (0, i))],
            out_specs=pl.BlockSpec((tl // SC_CHUNK, TOP_K, SC_CHUNK), lambda i, p: (i, 0, 0)),
        ),
        out_shape=jax.ShapeDtypeStruct((t // SC_CHUNK, TOP_K, SC_CHUNK), jnp.int32),
        compiler_params=pltpu.CompilerParams(dimension_semantics=("arbitrary",)),
        name="moe_dest",
    )(pstart[:, 0], idx, rank)
    return dest3, (tab[0, :n_blocks], tab[1, :n_blocks], tab[3, :1], tab[2, :n_blocks])


def _sc_mesh():
    return plsc.VectorSubcoreMesh(core_axis_name="c", subcore_axis_name="s")


def _sc_dispatch(h2_rows, dest3, cap):
    t = h2_rows.shape[0]
    per_w = (t // SC_CHUNK) // (SC_CORES * SC_SUBCORES)

    @functools.partial(
        pl.kernel, mesh=_sc_mesh(),
        out_type=jax.ShapeDtypeStruct((cap, PACK_TILES, LANES), _ROWS),
        scratch_types=[pltpu.VMEM((TOP_K, SC_CHUNK), jnp.int32),
                       pltpu.VMEM((SC_CHUNK, PACK_TILES, LANES), _ROWS)],
        name="moe_dispatch",
    )
    def run(x_hbm, dest_hbm, out_hbm, idx_v, rows_v):
        wid = lax.axis_index("s") * SC_CORES + lax.axis_index("c")

        @pl.loop(0, per_w)
        def _(i):
            c = wid * per_w + i
            pltpu.sync_copy(dest_hbm.at[c], idx_v)
            pltpu.sync_copy(x_hbm.at[pl.ds(c * SC_CHUNK, SC_CHUNK)], rows_v)
            for k in range(TOP_K):
                pltpu.sync_copy(rows_v, out_hbm.at[idx_v.at[k]])

    return run(h2_rows, dest3)


def _sc_combine(ys_rows, dest3, t):
    per_w = (t // SC_CHUNK) // (SC_CORES * SC_SUBCORES)

    @functools.partial(
        pl.kernel, mesh=_sc_mesh(),
        out_type=jax.ShapeDtypeStruct((TOP_K, t, PACK_TILES, LANES), _ROWS),
        scratch_types=[pltpu.VMEM((TOP_K, SC_CHUNK), jnp.int32),
                       pltpu.VMEM((SC_CHUNK, PACK_TILES, LANES), _ROWS)],
        name="moe_combine",
    )
    def run(ys_hbm, dest_hbm, out_hbm, idx_v, rows_v):
        wid = lax.axis_index("s") * SC_CORES + lax.axis_index("c")

        @pl.loop(0, per_w)
        def _(i):
            c = wid * per_w + i
            pltpu.sync_copy(dest_hbm.at[c], idx_v)
            for k in range(TOP_K):
                pltpu.sync_copy(ys_hbm.at[idx_v.at[k]], rows_v)
                pltpu.sync_copy(rows_v, out_hbm.at[k, pl.ds(c * SC_CHUNK, SC_CHUNK)])

    return run(ys_rows, dest3)


def _ffn_kernel(be_ref, nv_ref, nu_ref, nx_ref, xs_ref, b1_ref, b2_ref, perm_ref, w1_hbm, w2_hbm,
                dep_ref, ys_ref, w1_f, w2_f, w1_s, w2_s, sems, *, layer):
    del dep_ref
    i = pl.program_id(0)
    active = i < nu_ref[0]
    e = be_ref[i]
    new_expert = jnp.logical_or(i == 0, e != be_ref[jnp.maximum(i - 1, 0)])

    def weight_copies(expert):
        return (pltpu.make_async_copy(w1_hbm.at[layer, expert], w1_f, sems.at[0]),
                pltpu.make_async_copy(w2_hbm.at[layer, expert], w2_f, sems.at[1]))

    @pl.when(jnp.logical_and(active, i == 0))
    def _():
        for cp in weight_copies(e):
            cp.start()

    @pl.when(jnp.logical_and(active, new_expert))
    def _():
        for cp in weight_copies(e):
            cp.wait()
        half = PERM // 2
        for n in range(2 * D_FF // PERM):
            blk = w1_f[:, n * PERM:(n + 1) * PERM].astype(_BF16)
            moved = _dot(blk, perm_ref[...]).astype(_BF16)
            w1_s[:, n * half:(n + 1) * half] = moved[:, :half]
            w1_s[:, D_FF + n * half:D_FF + (n + 1) * half] = moved[:, half:]
        w2_s[...] = w2_f[...].astype(_BF16)

        @pl.when(nx_ref[i] >= 0)
        def _():
            for cp in weight_copies(nx_ref[i]):
                cp.start()

    def ffn_rows(n_rows):
        x = _load_rows(xs_ref, n_rows)
        rows = lax.broadcasted_iota(jnp.int32, (n_rows, 1), 0)
        x = jnp.where(rows < nv_ref[i], x, 0.0).astype(_BF16)
        hb = _dot(x, w1_s[...]) + b1_ref[...]
        x_glu = jnp.minimum(hb[:, :D_FF], SWIGLU_LIMIT)
        x_lin = jnp.clip(hb[:, D_FF:], -SWIGLU_LIMIT, SWIGLU_LIMIT)
        act = x_glu * jax.nn.sigmoid(SWIGLU_ALPHA * x_glu) * (x_lin + 1.0)
        y = _dot(act.astype(_BF16), w2_s[...]) + b2_ref[...]
        _store_rows(ys_ref, 0, _round_bf16(y))

    half_full = nv_ref[i] <= EXPERT_BLOCK // 2
    pl.when(jnp.logical_and(active, jnp.logical_not(half_full)))(
        functools.partial(ffn_rows, EXPERT_BLOCK))
    pl.when(jnp.logical_and(active, half_full))(
        functools.partial(ffn_rows, EXPERT_BLOCK // 2))


def _expert_ffn(xs_rows, blocks, w1, b1, w2, b2, layer, dep):
    block_e, n_valid, n_used, next_e = blocks
    n_blocks = block_e.shape[0]
    half = PERM // 2
    src = jnp.arange(PERM)
    dst = jnp.where(src % 2 == 0, src // 2, half + src // 2)
    perm = (dst[:, None] == jnp.arange(PERM)[None, :]).astype(_BF16)
    blk = lambda i, be, nv, nu, nx: (jnp.minimum(i, nu[0] - 1), 0)
    bsel = lambda i, be, nv, nu, nx: (layer, be[jnp.minimum(i, nu[0] - 1)], 0, 0)
    return pl.pallas_call(
        functools.partial(_ffn_kernel, layer=layer),
        grid_spec=pltpu.PrefetchScalarGridSpec(
            num_scalar_prefetch=4,
            grid=(n_blocks,),
            in_specs=[
                pl.BlockSpec((EXPERT_BLOCK * PACK_TILES, LANES), blk),
                pl.BlockSpec((None, None, 1, 2 * D_FF), bsel),
                pl.BlockSpec((None, None, 1, D), bsel),
                pl.BlockSpec((PERM, PERM), lambda i, be, nv, nu, nx: (0, 0)),
                pl.BlockSpec(memory_space=pl.ANY),
                pl.BlockSpec(memory_space=pl.ANY),
                pl.BlockSpec(memory_space=pl.ANY),
            ],
            out_specs=pl.BlockSpec((EXPERT_BLOCK * PACK_TILES, LANES), blk),
            scratch_shapes=[
                pltpu.VMEM((D, 2 * D_FF), _F32), pltpu.VMEM((D_FF, D), _F32),
                pltpu.VMEM((D, 2 * D_FF), _BF16), pltpu.VMEM((D_FF, D), _BF16),
                pltpu.SemaphoreType.DMA((2,)),
            ],
        ),
        out_shape=jax.ShapeDtypeStruct(xs_rows.shape, _ROWS),
        compiler_params=pltpu.CompilerParams(
            dimension_semantics=("arbitrary",), vmem_limit_bytes=VMEM_LIMIT),
        name="expert_ffn",
    )(block_e, n_valid, n_used, next_e, xs_rows, b1, b2, perm, w1, w2, dep)


def _post_kernel(x1_ref, yg_ref, gate_ref, m2_ref, ln_ref, prev_ref, o_ref):
    del prev_ref
    o_ref[...] = _moe_output(x1_ref, yg_ref, gate_ref, m2_ref, ln_ref)


def _moe_post(src, mb0, total_nb, prev, keep_prev):
    nb, seq, _ = src[1].shape
    n_s = seq // TS
    out_shape = jax.ShapeDtypeStruct((total_nb, seq, D), _F32)
    aliases = {5: 0} if keep_prev else {}
    return pl.pallas_call(
        _post_kernel,
        grid=(nb, n_s),
        in_specs=_moe_output_specs(n_s, mb0) + [pl.BlockSpec(memory_space=pl.ANY)],
        out_specs=pl.BlockSpec((None, TS, D), lambda b, s: (b + mb0, s, 0)),
        out_shape=out_shape,
        input_output_aliases=aliases,
        compiler_params=pltpu.CompilerParams(
            dimension_semantics=("arbitrary", "arbitrary"), vmem_limit_bytes=VMEM_LIMIT),
        name="moe_post",
    )(*src[1:], prev)


def _moe_layer(mixed, m2, ln, w1, b1, w2, b2, layer):
    n = len(mixed)
    nb, seq, _ = mixed[0][0].shape
    t = nb * seq
    cap = (t * TOP_K // EXPERT_BLOCK + N_EXPERTS) * EXPERT_BLOCK
    dest3s, yss = [], []
    staged = []
    for x1, h2_rows, idx, gates, rank, counts in mixed:
        dest3, blocks = _plan(idx, rank, counts)
        xs = _sc_dispatch(h2_rows.reshape(t, PACK_TILES, LANES), dest3, cap)
        staged.append((xs, blocks))
        dest3s.append(dest3)
    for g, (xs, blocks) in enumerate(staged):
        dep = mixed[n - 1][5] if g == 0 else yss[g - 1]
        yss.append(_expert_ffn(xs.reshape(cap * PACK_TILES, LANES), blocks, w1, b1, w2, b2,
                               layer, dep))
    srcs = []
    for g, (ys, dest3) in enumerate(zip(yss, dest3s)):
        yg = _sc_combine(ys.reshape(cap, PACK_TILES, LANES), dest3, t)
        srcs.append(('moe', mixed[g][0], yg.reshape(TOP_K, t * PACK_TILES, LANES), mixed[g][3],
                     m2, ln))
    return srcs, yss[n - 1]


def kernel(x, c, mod_w, mod_b, ln_g, ln_b, ab_w_in, pool_w, pool_scale, conv_w, conv_b, lru_w_a, lru_b_a, lru_w_x, lru_b_x, lru_lambda, ab_w_out, ab_b_out, sgu_w_in, sgu_b_in, sgu_ln_g, sgu_ln_b, sgu_w_s, sgu_b_s, sgu_w_out, sgu_b_out, router_w, router_b, moe_w1, moe_b1, moe_w2, moe_b2):
    assert x.shape[1] % TS == 0 and x.shape[2] == D
    mods = _modulation(c, mod_w, mod_b)
    tri = (lax.broadcasted_iota(jnp.int32, (TS, TS), 0)
           < lax.broadcasted_iota(jnp.int32, (TS, TS), 1)).astype(_BF16)
    b1 = jnp.concatenate([moe_b1[..., 0::2], moe_b1[..., 1::2]], axis=-1)[:, :, None, :]
    b2 = moe_b2[:, :, None, :]
    nb = x.shape[0]
    n_groups = 2 if nb % 2 == 0 else 1
    gb = nb // n_groups
    srcs = [('raw', x, g * gb) for g in range(n_groups)]
    order = tri
    for layer in range(DEPTH):
        j = layer // 2
        m1, m2 = mods[layer, 0], mods[layer, 1]
        ln1 = jnp.stack([ln_g[layer, 0], ln_b[layer, 0]])
        ln2 = jnp.stack([ln_g[layer, 1], ln_b[layer, 1]])
        wr_hi, wr_lo = _split(router_w[layer].T)
        router = (wr_hi, wr_lo, router_b[layer][:, None])
        mixed = []
        for g in range(n_groups):
            span = (g * gb, gb)
            if layer % 2 == 0:
                outs = _even_layer(srcs[g], span, order, m1, m2, ln1, router, tri, ab_w_in[j],
                                   pool_w[j], pool_scale[j], conv_w[j], conv_b[j], lru_w_a[j],
                                   lru_b_a[j], lru_w_x[j], lru_b_x[j], lru_lambda[j], ab_w_out[j],
                                   ab_b_out[j])
            else:
                outs = _odd_layer(srcs[g], span, order, m1, m2, ln1, router, tri, sgu_w_in[j],
                                  sgu_b_in[j], sgu_ln_g[j], sgu_ln_b[j], sgu_w_s[j], sgu_b_s[j],
                                  sgu_w_out[j], sgu_b_out[j])
            mixed.append(outs)
            order = outs[5]
        srcs, order = _moe_layer(mixed, m2, ln2, moe_w1, b1, moe_w2, b2, layer)
    out = order
    for g in range(n_groups):
        out = _moe_post(srcs[g], g * gb, nb, out, keep_prev=g > 0)
    return out
```

```python
import functools

import jax
import jax.numpy as jnp
from jax import lax
from jax.experimental import pallas as pl
from jax.experimental.pallas import tpu as pltpu
from jax.experimental.pallas import tpu_sc as plsc

D = 1024
DEPTH = 4
POOL_WINDOWS = (2, 4, 8, 16)
POOL_GROUP = 128
D_POOL = 512
D_REC = 512
N_REC_HEADS = 8
REC_HEAD = 64
CONV_WIDTH = 4
LRU_C = 8.0
D_IN_AB = 1536
CHUNK = 128
N_SGU_HEADS = 8
N_EXPERTS = 32
TOP_K = 4
D_FF = 1024
SWIGLU_LIMIT = 7.0
SWIGLU_ALPHA = 1.702
EXPERT_BLOCK = 512
LN_EPS = 1e-5
ALPHA = (2 * DEPTH) ** 0.25

TS = 512
LANES = 128
PACK_TILES = D // (2 * LANES)
_ROWS = jnp.uint32
POOL_PAD = 32
CONV_PAD = 8
SCAN_CHUNKS = 8
SCAN_LEN = TS // SCAN_CHUNKS
SCAN_PITCH = SCAN_LEN + 8
SC_CORES = 2
SC_SUBCORES = 16
F32_TILES = D // LANES
SC_CHUNK = 128
SC_SUB = 16
SC_LANES = 16
PERM = 256
VMEM_LIMIT = 56 * 1024 * 1024

_F32 = jnp.float32
_BF16 = jnp.bfloat16
_NT = (((1,), (1,)), ((), ()))


def _split(a):
    hi = a.astype(_BF16)
    lo = (a - hi.astype(_F32)).astype(_BF16)
    return hi, lo


def _dot(a, b):
    return jnp.dot(a, b, preferred_element_type=_F32)


def _layer_norm(x, g, b):
    mu = jnp.mean(x, axis=-1, keepdims=True)
    xc = x - mu
    var = jnp.mean(xc * xc, axis=-1, keepdims=True)
    return xc * lax.rsqrt(var + LN_EPS) * g + b


def _gelu(x):
    return jax.nn.gelu(x)


def _mod_kernel(c_ref, w_ref, b_ref, o_ref):
    c = c_ref[...]
    ca = c * jax.nn.sigmoid(c)
    a_hi, a_lo = _split(ca)
    w_hi, w_lo = _split(w_ref[...])
    o_ref[...] = _dot(a_hi, w_hi) + _dot(a_lo, w_hi) + _dot(a_hi, w_lo) + b_ref[...]


def _modulation(c, mod_w, mod_b):
    nb = c.shape[0]
    n = DEPTH * 2
    w = mod_w.reshape(n, D, 3 * D)
    b = mod_b.reshape(n, 1, 3 * D)
    out = pl.pallas_call(
        _mod_kernel,
        grid=(n, 3),
        in_specs=[
            pl.BlockSpec((nb, D), lambda i, j: (0, 0)),
            pl.BlockSpec((None, D, D), lambda i, j: (i, 0, j)),
            pl.BlockSpec((None, 1, D), lambda i, j: (i, 0, j)),
        ],
        out_specs=pl.BlockSpec((None, nb, D), lambda i, j: (i, 0, j)),
        out_shape=jax.ShapeDtypeStruct((n, nb, 3 * D), _F32),
        compiler_params=pltpu.CompilerParams(
            dimension_semantics=("arbitrary", "arbitrary"), vmem_limit_bytes=VMEM_LIMIT),
        name="modulation",
    )(c, w, b)
    return out.reshape(DEPTH, 2, nb, 3, D)


def _round_bf16(val):
    return val.astype(_BF16).astype(_F32)


def _store_rows(ref, r0, rounded):
    rows = rounded.shape[0]
    for q in range(PACK_TILES):
        lo = rounded[:, q * LANES:(q + 1) * LANES]
        hi = rounded[:, D // 2 + q * LANES:D // 2 + (q + 1) * LANES]
        word = (pltpu.bitcast(lo, jnp.uint32) >> 16) | pltpu.bitcast(hi, jnp.uint32)
        ref[pl.ds(r0 * PACK_TILES + q, rows, stride=PACK_TILES), :] = word


def _load_rows(ref, rows):
    lo, hi = [], []
    for q in range(PACK_TILES):
        word = ref[pl.ds(q, rows, stride=PACK_TILES), :]
        lo.append(pltpu.bitcast(word << 16, _F32))
        hi.append(pltpu.bitcast(word & jnp.uint32(0xFFFF0000), _F32))
    return jnp.concatenate(lo + hi, axis=1)


def _moe_output(x1_ref, y_ref, m2_ref, ln_ref):
    y = jnp.concatenate(
        [y_ref[pl.ds(j, TS, stride=F32_TILES), :] for j in range(F32_TILES)], axis=1)
    gate2 = 1.0 + m2_ref[2:3, :]
    return _layer_norm(ALPHA * x1_ref[...] + gate2 * y, ln_ref[0:1, :], ln_ref[1:2, :])


_N_MOE_OPERANDS = 4


def _moe_output_specs(n_seq_tiles, mb0):
    tile = lambda b, s: b * n_seq_tiles + s
    return [
        pl.BlockSpec((None, TS, D), lambda b, s: (b, s, 0)),
        pl.BlockSpec((TS * F32_TILES, LANES), lambda b, s: (tile(b, s), 0)),
        pl.BlockSpec((None, 3, D), lambda b, s: (b + mb0, 0, 0)),
        pl.BlockSpec((2, D), lambda b, s: (0, 0)),
    ]


def _mixer_input(refs, from_moe):
    if from_moe:
        return _moe_output(*refs[:_N_MOE_OPERANDS]), refs[_N_MOE_OPERANDS:]
    return refs[0][...], refs[1:]


def _finish_sublayer(parts, m1_ref, ln_ref, m2_ref, wr_hi_ref, wr_lo_ref, br_ref, tri_ref, first,
                     x1_ref, h2_ref, idx_ref, gate_ref, rank_ref, cnt_ref):
    @pl.when(first)
    def _():
        cnt_ref[...] = jnp.zeros_like(cnt_ref)

    gate1 = 1.0 + m1_ref[2:3, :]
    w_hi = wr_hi_ref[...]
    w_both = jnp.concatenate([w_hi, wr_lo_ref[...]], axis=0)
    seen = cnt_ref[:, 0:1]
    for r0, x, y in parts:
        n = x.shape[0]
        x1 = _layer_norm(ALPHA * x + gate1 * y, ln_ref[0:1, :], ln_ref[1:2, :])
        x1_ref[r0:r0 + n, :] = x1
        h2 = x1 * (1.0 + m2_ref[1:2, :]) + m2_ref[0:1, :]
        h2_r = _round_bf16(h2)
        _store_rows(h2_ref, r0, h2_r)

        h_hi = h2_r.astype(_BF16)
        h_lo = (h2 - h2_r).astype(_BF16)
        both = lax.dot_general(w_both, h_hi, _NT, preferred_element_type=_F32)
        logits = (both[:N_EXPERTS] + both[N_EXPERTS:]
                  + lax.dot_general(w_hi, h_lo, _NT, preferred_element_type=_F32)
                  + br_ref[...])
        iota_e = lax.broadcasted_iota(jnp.int32, (N_EXPERTS, n), 0)
        work = logits
        vals, idxs, hots = [], [], []
        for _ in range(TOP_K):
            m = jnp.max(work, axis=0, keepdims=True)
            idx = jnp.min(jnp.where(work == m, iota_e, N_EXPERTS), axis=0, keepdims=True)
            hot = iota_e == idx
            work = jnp.where(hot, -jnp.inf, work)
            vals.append(m)
            idxs.append(idx)
            hots.append(hot)
        exps = [jnp.exp(v - vals[0]) for v in vals]
        denom = exps[0] + exps[1] + exps[2] + exps[3]
        gate_ref[:, r0:r0 + n] = jnp.concatenate([e / denom for e in exps], axis=0)
        idx_ref[:, r0:r0 + n] = jnp.concatenate(idxs, axis=0)

        sel = jnp.zeros((N_EXPERTS, n), _F32)
        for hot in hots:
            sel = sel + jnp.where(hot, 1.0, 0.0)
        before = _dot(sel.astype(_BF16), tri_ref[0:n, 0:n]) + seen
        ranks = [jnp.sum(jnp.where(hot, before, 0.0), axis=0, keepdims=True) for hot in hots]
        rank_ref[:, r0:r0 + n] = jnp.concatenate(ranks, axis=0).astype(jnp.int32)
        seen = seen + jnp.sum(sel, axis=1, keepdims=True)
    cnt_ref[...] = jnp.broadcast_to(seen, cnt_ref.shape)


def _epilogue_specs(n_seq_tiles, mb0):
    tile = lambda b, s: b * n_seq_tiles + s
    in_specs = [
        pl.BlockSpec((None, 3, D), lambda b, s: (b + mb0, 0, 0)),
        pl.BlockSpec((2, D), lambda b, s: (0, 0)),
        pl.BlockSpec((N_EXPERTS, D), lambda b, s: (0, 0)),
        pl.BlockSpec((N_EXPERTS, D), lambda b, s: (0, 0)),
        pl.BlockSpec((N_EXPERTS, 1), lambda b, s: (0, 0)),
        pl.BlockSpec((TS, TS), lambda b, s: (0, 0)),
        pl.BlockSpec(memory_space=pl.ANY),
    ]
    out_specs = [
        pl.BlockSpec((None, TS, D), lambda b, s: (b, s, 0)),
        pl.BlockSpec((TS * PACK_TILES, LANES), lambda b, s: (tile(b, s), 0)),
        pl.BlockSpec((TOP_K, TS), lambda b, s: (0, tile(b, s))),
        pl.BlockSpec((TOP_K, TS), lambda b, s: (0, tile(b, s))),
        pl.BlockSpec((TOP_K, TS), lambda b, s: (0, tile(b, s))),
        pl.BlockSpec((N_EXPERTS, LANES), lambda b, s: (0, 0)),
    ]
    return in_specs, out_specs


def _epilogue_out_shapes(nb, seq):
    t = nb * seq
    return [
        jax.ShapeDtypeStruct((nb, seq, D), _F32),
        jax.ShapeDtypeStruct((t * PACK_TILES, LANES), _ROWS),
        jax.ShapeDtypeStruct((TOP_K, t), jnp.int32),
        jax.ShapeDtypeStruct((TOP_K, t), _F32),
        jax.ShapeDtypeStruct((TOP_K, t), jnp.int32),
        jax.ShapeDtypeStruct((N_EXPERTS, LANES), _F32),
    ]


def _even_kernel(*refs, from_moe):
    x, refs = _mixer_input(refs, from_moe)
    (m1_ref, win_ref, poolw_ref, pscale_ref, convw_ref, convb_ref, wg_ref, bg_ref, lam_ref,
     wout_ref, bout_ref,
     m2_ref, ln_ref, wr_hi_ref, wr_lo_ref, br_ref, tri_ref, dep_ref,
     x1_ref, h2_ref, idx_ref, gate_ref, rank_ref, cnt_ref,
     p1, p2, p4, p8, cbuf, s_a, s_b, s_h, s_p, hstate) = refs
    del dep_ref
    b = pl.program_id(0)
    s = pl.program_id(1)
    h = x * (1.0 + m1_ref[1:2, :]) + m1_ref[0:1, :]
    z = _dot(h.astype(_BF16), win_ref[...])
    xp = z[:, :D_POOL]
    xr = z[:, D_POOL:D_POOL + D_REC]
    gr = z[:, D_POOL + D_REC:]

    @pl.when(s == 0)
    def _():
        p1[0:POOL_PAD, :] = jnp.zeros((POOL_PAD, D_POOL), _F32)
        p2[0:POOL_PAD, :] = jnp.zeros((POOL_PAD, D_POOL), _F32)
        p4[0:POOL_PAD, :] = jnp.zeros((POOL_PAD, 3 * POOL_GROUP), _F32)
        p8[0:POOL_PAD, :] = jnp.zeros((POOL_PAD, 2 * POOL_GROUP), _F32)
        cbuf[0:CONV_PAD, :] = jnp.zeros((CONV_PAD, D_REC), _F32)
        hstate[...] = jnp.zeros_like(hstate)

    rr = TS + 16
    p1[POOL_PAD:, :] = xp
    p2[16:, :] = p1[16:16 + rr, :] + p1[15:15 + rr, :]
    p4[16:, :] = p2[16:16 + rr, POOL_GROUP:] + p2[14:14 + rr, POOL_GROUP:]
    p8[16:, :] = p4[16:16 + rr, POOL_GROUP:] + p4[12:12 + rr, POOL_GROUP:]
    s16 = p8[POOL_PAD:, POOL_GROUP:] + p8[POOL_PAD - 8:POOL_PAD - 8 + TS, POOL_GROUP:]
    sums = (p2[POOL_PAD:, 0:POOL_GROUP], p4[POOL_PAD:, 0:POOL_GROUP],
            p8[POOL_PAD:, 0:POOL_GROUP], s16)
    pos = (s * TS + lax.broadcasted_iota(jnp.int32, (TS, 1), 0) + 1).astype(_F32)
    pooled = []
    for g, w in enumerate(POOL_WINDOWS):
        cnt = jnp.minimum(pos, float(w))
        pooled.append(sums[g] / cnt - xp[:, g * POOL_GROUP:(g + 1) * POOL_GROUP])
    pooled = jnp.concatenate(pooled, axis=1).astype(_BF16)
    y_pool = _dot(pooled, poolw_ref[...]) * pscale_ref[...]
    p1[16:POOL_PAD, :] = p1[TS + 16:TS + POOL_PAD, :]

    cbuf[CONV_PAD:, :] = xr
    xc = convb_ref[...]
    for k in range(CONV_WIDTH):
        off = CONV_PAD - (CONV_WIDTH - 1) + k
        xc = xc + cbuf[off:off + TS, :] * convw_ref[k:k + 1, :]
    cbuf[0:CONV_PAD, :] = cbuf[TS:TS + CONV_PAD, :]
    gates = _dot(xc.astype(_BF16), wg_ref[...]) + bg_ref[...]
    r_gate = jax.nn.sigmoid(gates[:, :D_REC])
    i_gate = jax.nn.sigmoid(gates[:, D_REC:])
    log_a = -LRU_C * r_gate * jax.nn.softplus(-lam_ref[...])
    a = jnp.exp(log_a)
    mult = jnp.sqrt(-jnp.tanh(log_a) * (a * a + 1.0))
    b_in = mult * (i_gate * xc)

    lane_tiles = [slice(lt * LANES, (lt + 1) * LANES) for lt in range(D_REC // LANES)]
    for lt, lanes in enumerate(lane_tiles):
        for ch in range(SCAN_CHUNKS):
            dst = pl.ds(ch * SCAN_PITCH, SCAN_LEN)
            s_a[lt, dst, :] = a[ch * SCAN_LEN:(ch + 1) * SCAN_LEN, lanes]
            s_b[lt, dst, :] = b_in[ch * SCAN_LEN:(ch + 1) * SCAN_LEN, lanes]
    h_loc = [jnp.zeros((SCAN_CHUNKS, LANES), _F32) for _ in lane_tiles]
    decay = [jnp.ones((SCAN_CHUNKS, LANES), _F32) for _ in lane_tiles]
    for step in range(SCAN_LEN):
        rows = pl.ds(step, SCAN_CHUNKS, stride=SCAN_PITCH)
        for lt in range(len(lane_tiles)):
            a_s = s_a[lt, rows, :]
            h_loc[lt] = a_s * h_loc[lt] + s_b[lt, rows, :]
            decay[lt] = a_s * decay[lt]
            s_h[lt, rows, :] = h_loc[lt]
            s_p[lt, rows, :] = decay[lt]
    cols = []
    for lt, lanes in enumerate(lane_tiles):
        carry = hstate[0:1, lanes]
        chunks = []
        for ch in range(SCAN_CHUNKS):
            src = pl.ds(ch * SCAN_PITCH, SCAN_LEN)
            chunks.append(s_h[lt, src, :] + s_p[lt, src, :] * carry)
            carry = decay[lt][ch:ch + 1, :] * carry + h_loc[lt][ch:ch + 1, :]
        hstate[0:1, lanes] = carry
        cols.append(jnp.concatenate(chunks, axis=0))
    h_rec = jnp.concatenate(cols, axis=1)
    y_rec = h_rec * _gelu(gr)

    y_mix = jnp.concatenate([y_pool, y_rec], axis=1).astype(_BF16)
    y = _dot(y_mix, wout_ref[...]) + bout_ref[...]
    first = jnp.logical_and(b == 0, s == 0)
    _finish_sublayer([(0, x, y)], m1_ref, ln_ref, m2_ref, wr_hi_ref, wr_lo_ref, br_ref, tri_ref,
                     first, x1_ref, h2_ref, idx_ref, gate_ref, rank_ref, cnt_ref)


def _block_diag(w):
    n, k, _ = w.shape
    eye = jnp.eye(n, dtype=w.dtype)
    return (eye[:, None, :, None] * w[:, :, None, :]).reshape(n * k, n * k)


def _source(src, mb0):
    if src[0] == 'raw':
        _, x, xb0 = src
        return False, x.shape[1], [pl.BlockSpec((None, TS, D), lambda b, s: (b + xb0, s, 0))], [x]
    seq = src[1].shape[1]
    return True, seq, _moe_output_specs(seq // TS, mb0), list(src[1:])


def _even_layer(src, span, dep, m1, m2, ln, router, tri, w_in, pool_w, pool_scale, conv_w, conv_b,
                w_a, b_a, w_x, b_x, lam, w_out, b_out):
    mb0, nb = span
    from_moe, seq, src_specs, src_ops = _source(src, mb0)
    n_s = seq // TS
    wr_hi, wr_lo, br = router
    wg = jnp.concatenate([_block_diag(w_a), _block_diag(w_x)], axis=1).astype(_BF16)
    bg = jnp.concatenate([b_a, b_x])[None, :]
    ep_in, ep_out = _epilogue_specs(n_s, mb0)
    const = lambda shape: pl.BlockSpec(shape, lambda b, s: (0,) * len(shape))
    in_specs = src_specs + [
        pl.BlockSpec((None, 3, D), lambda b, s: (b + mb0, 0, 0)),
        const((D, D_IN_AB)), const((D_POOL, D_POOL)), const((1, D_POOL)),
        const((CONV_WIDTH, D_REC)), const((1, D_REC)), const((D_REC, 2 * D_REC)),
        const((1, 2 * D_REC)), const((1, D_REC)), const((D, D)), const((1, D)),
    ] + ep_in
    scratch = [
        pltpu.VMEM((TS + POOL_PAD, D_POOL), _F32),
        pltpu.VMEM((TS + POOL_PAD, D_POOL), _F32),
        pltpu.VMEM((TS + POOL_PAD, 3 * POOL_GROUP), _F32),
        pltpu.VMEM((TS + POOL_PAD, 2 * POOL_GROUP), _F32),
        pltpu.VMEM((TS + CONV_PAD, D_REC), _F32),
    ] + [pltpu.VMEM((D_REC // LANES, SCAN_CHUNKS * SCAN_PITCH, LANES), _F32)] * 4 + [
        pltpu.VMEM((8, D_REC), _F32),
    ]
    return pl.pallas_call(
        functools.partial(_even_kernel, from_moe=from_moe),
        grid=(nb, n_s),
        in_specs=in_specs,
        out_specs=ep_out,
        out_shape=_epilogue_out_shapes(nb, seq),
        scratch_shapes=scratch,
        compiler_params=pltpu.CompilerParams(
            dimension_semantics=("arbitrary", "arbitrary"), vmem_limit_bytes=VMEM_LIMIT),
        name="even_mixer",
    )(*src_ops, m1, w_in.astype(_BF16), _block_diag(pool_w).astype(_BF16), pool_scale[None, :],
      conv_w, conv_b[None, :], wg, bg, lam[None, :], w_out.astype(_BF16), b_out[None, :],
      m2, ln, wr_hi, wr_lo, br, tri, dep)


def _odd_kernel(*refs, from_moe):
    x, refs = _mixer_input(refs, from_moe)
    (m1_ref, win_ref, bin_ref, sln_ref, ws_ref, bs_ref, wout_ref, bout_ref,
     m2_ref, ln_ref, wr_hi_ref, wr_lo_ref, br_ref, tri_ref, dep_ref,
     x1_ref, h2_ref, idx_ref, gate_ref, rank_ref, cnt_ref,
     sv_buf) = refs
    del dep_ref
    b = pl.program_id(0)
    s = pl.program_id(1)
    h = x * (1.0 + m1_ref[1:2, :]) + m1_ref[0:1, :]
    z = _gelu(_dot(h.astype(_BF16), win_ref[...]) + bin_ref[...])
    u = z[:, :D]
    v = _layer_norm(z[:, D:], sln_ref[0:1, :], sln_ref[1:2, :]).astype(_BF16)
    n_chunks = TS // CHUNK
    row = lax.broadcasted_iota(jnp.int32, (CHUNK, CHUNK), 0)
    col = lax.broadcasted_iota(jnp.int32, (CHUNK, CHUNK), 1)
    causal = col <= row
    for hh in range(N_SGU_HEADS):
        lanes = slice(hh * CHUNK, (hh + 1) * CHUNK)
        w_h = jnp.where(causal, ws_ref[hh], 0.0).astype(_BF16)
        v_h = jnp.concatenate(
            [v[n * CHUNK:(n + 1) * CHUNK, lanes] for n in range(n_chunks)], axis=1)
        sv_h = _dot(w_h, v_h)
        for n in range(n_chunks):
            sv_buf[n * CHUNK:(n + 1) * CHUNK, lanes] = (
                sv_h[:, n * CHUNK:(n + 1) * CHUNK] + bs_ref[hh])
    y = _dot((u * sv_buf[...]).astype(_BF16), wout_ref[...]) + bout_ref[...]
    first = jnp.logical_and(b == 0, s == 0)
    _finish_sublayer([(0, x, y)], m1_ref, ln_ref, m2_ref, wr_hi_ref, wr_lo_ref, br_ref, tri_ref,
                     first, x1_ref, h2_ref, idx_ref, gate_ref, rank_ref, cnt_ref)


def _odd_layer(src, span, dep, m1, m2, ln, router, tri, w_in, b_in, ln_g, ln_b, w_s, b_s, w_out,
               b_out):
    mb0, nb = span
    from_moe, seq, src_specs, src_ops = _source(src, mb0)
    n_s = seq // TS
    wr_hi, wr_lo, br = router
    sln = jnp.stack([ln_g, ln_b])
    bs = jnp.broadcast_to(b_s[:, :, None], (N_SGU_HEADS, CHUNK, CHUNK))
    ep_in, ep_out = _epilogue_specs(n_s, mb0)
    const = lambda shape: pl.BlockSpec(shape, lambda b, s: (0,) * len(shape))
    in_specs = src_specs + [
        pl.BlockSpec((None, 3, D), lambda b, s: (b + mb0, 0, 0)),
        const((D, 2 * D)), const((1, 2 * D)), const((2, D)),
        const((N_SGU_HEADS, CHUNK, CHUNK)), const((N_SGU_HEADS, CHUNK, CHUNK)),
        const((D, D)), const((1, D)),
    ] + ep_in
    return pl.pallas_call(
        functools.partial(_odd_kernel, from_moe=from_moe),
        grid=(nb, n_s),
        in_specs=in_specs,
        out_specs=ep_out,
        out_shape=_epilogue_out_shapes(nb, seq),
        scratch_shapes=[pltpu.VMEM((TS, D), _F32)],
        compiler_params=pltpu.CompilerParams(
            dimension_semantics=("arbitrary", "arbitrary"), vmem_limit_bytes=VMEM_LIMIT),
        name="odd_mixer",
    )(*src_ops, m1, w_in.astype(_BF16), b_in[None, :], sln, w_s, bs, w_out.astype(_BF16),
      b_out[None, :], m2, ln, wr_hi, wr_lo, br, tri, dep)


_BLOCK_SHIFT = EXPERT_BLOCK.bit_length() - 1
_TAB_ROWS = 8


def _blocks_kernel(cnt_ref, tab_ref, pstart_ref):
    n_lanes = tab_ref.shape[1]
    cnt = cnt_ref[:, 0:1].astype(jnp.int32)
    padded = ((cnt + (EXPERT_BLOCK - 1)) >> _BLOCK_SHIFT) << _BLOCK_SHIFT
    r = lax.broadcasted_iota(jnp.int32, (N_EXPERTS, N_EXPERTS), 0)
    c = lax.broadcasted_iota(jnp.int32, (N_EXPERTS, N_EXPERTS), 1)
    as_row = lambda col: jnp.sum(jnp.where(r == c, col, 0), axis=0, keepdims=True)
    pends = jnp.sum(jnp.where(c <= r, as_row(padded), 0), axis=1, keepdims=True)
    pstart = pends - padded
    total = jnp.sum(padded, axis=0, keepdims=True)
    later = jnp.where(c > r, as_row(cnt), 0) > 0
    nxt = jnp.min(jnp.where(later, c, N_EXPERTS), axis=1, keepdims=True)
    nxt = jnp.where(nxt < N_EXPERTS, nxt, -1)
    e_iota = lax.broadcasted_iota(jnp.int32, (N_EXPERTS, n_lanes), 0)
    start = lax.broadcasted_iota(jnp.int32, (N_EXPERTS, n_lanes), 1) << _BLOCK_SHIFT
    block_e = jnp.minimum(
        jnp.sum(jnp.where(start >= pends, 1, 0), axis=0, keepdims=True), N_EXPERTS - 1)
    mine = e_iota == block_e
    seg_end = jnp.sum(jnp.where(mine, pstart + cnt, 0), axis=0, keepdims=True)
    n_valid = jnp.clip(seg_end - start[0:1, :], 0, EXPERT_BLOCK)
    n_valid = jnp.where(start[0:1, :] < total, n_valid, 0)
    next_e = jnp.sum(jnp.where(mine, nxt, 0), axis=0, keepdims=True)
    n_used = jnp.broadcast_to(total >> _BLOCK_SHIFT, (1, n_lanes))
    zeros = jnp.zeros((_TAB_ROWS - 4, n_lanes), jnp.int32)
    tab_ref[...] = jnp.concatenate([block_e, n_valid, next_e, n_used, zeros], axis=0)
    pstart_ref[...] = jnp.broadcast_to(pstart, pstart_ref.shape)


def _dest_kernel(pstart_ref, idx_ref, rank_ref, gate_ref, dest_ref, gate3_ref):
    idx = idx_ref[...]
    dest = rank_ref[...]
    for e in range(N_EXPERTS):
        dest = dest + jnp.where(idx == e, pstart_ref[e], 0)
    gates = gate_ref[...]
    for ch in range(dest_ref.shape[0]):
        dest_ref[ch] = dest[:, ch * SC_CHUNK:(ch + 1) * SC_CHUNK]
        gate3_ref[ch] = gates[:, ch * SC_CHUNK:(ch + 1) * SC_CHUNK]


def _plan(idx, rank, gates, counts):
    t = idx.shape[1]
    n_blocks = (t * TOP_K) // EXPERT_BLOCK + N_EXPERTS
    n_lanes = -(-n_blocks // LANES) * LANES
    tab, pstart = pl.pallas_call(
        _blocks_kernel,
        out_shape=[jax.ShapeDtypeStruct((_TAB_ROWS, n_lanes), jnp.int32),
                   jax.ShapeDtypeStruct((N_EXPERTS, LANES), jnp.int32)],
        name="moe_blocks",
    )(counts)
    tl = min(t, 8192)
    by_token = pl.BlockSpec((TOP_K, tl), lambda i, p: (0, i))
    by_chunk = pl.BlockSpec((tl // SC_CHUNK, TOP_K, SC_CHUNK), lambda i, p: (i, 0, 0))
    chunked = (t // SC_CHUNK, TOP_K, SC_CHUNK)
    dest3, gate3 = pl.pallas_call(
        _dest_kernel,
        grid_spec=pltpu.PrefetchScalarGridSpec(
            num_scalar_prefetch=1,
            grid=(t // tl,),
            in_specs=[by_token, by_token, by_token],
            out_specs=[by_chunk, by_chunk],
        ),
        out_shape=[jax.ShapeDtypeStruct(chunked, jnp.int32), jax.ShapeDtypeStruct(chunked, _F32)],
        compiler_params=pltpu.CompilerParams(dimension_semantics=("arbitrary",)),
        name="moe_dest",
    )(pstart[:, 0], idx, rank, gates)
    return dest3, gate3, (tab[0, :n_blocks], tab[1, :n_blocks], tab[3, :1], tab[2, :n_blocks])


def _sc_mesh():
    return plsc.VectorSubcoreMesh(core_axis_name="c", subcore_axis_name="s")


def _sc_dispatch(h2_rows, dest3, cap):
    t = h2_rows.shape[0]
    per_w = (t // SC_CHUNK) // (SC_CORES * SC_SUBCORES)

    @functools.partial(
        pl.kernel, mesh=_sc_mesh(),
        out_type=jax.ShapeDtypeStruct((cap, PACK_TILES, LANES), _ROWS),
        scratch_types=[pltpu.VMEM((TOP_K, SC_CHUNK), jnp.int32),
                       pltpu.VMEM((SC_CHUNK, PACK_TILES, LANES), _ROWS)],
        name="moe_dispatch",
    )
    def run(x_hbm, dest_hbm, out_hbm, idx_v, rows_v):
        wid = lax.axis_index("s") * SC_CORES + lax.axis_index("c")

        @pl.loop(0, per_w)
        def _(i):
            c = wid * per_w + i
            pltpu.sync_copy(dest_hbm.at[c], idx_v)
            pltpu.sync_copy(x_hbm.at[pl.ds(c * SC_CHUNK, SC_CHUNK)], rows_v)
            for k in range(TOP_K):
                pltpu.sync_copy(rows_v, out_hbm.at[idx_v.at[k]])

    return run(h2_rows, dest3)


def _sc_combine(ys_rows, dest3, gate3, t):
    per_w = (t // SC_CHUNK) // (SC_CORES * SC_SUBCORES)
    n_sub = SC_CHUNK // SC_SUB
    params = pltpu.CompilerParams(needs_layout_passes=False)

    @functools.partial(
        pl.kernel, mesh=_sc_mesh(), compiler_params=params,
        out_type=jax.ShapeDtypeStruct((t, F32_TILES, LANES), _F32),
        scratch_types=[pltpu.VMEM((TOP_K, SC_CHUNK), jnp.int32),
                       pltpu.VMEM((TOP_K, SC_CHUNK), _F32)]
        + [pltpu.VMEM((SC_SUB, PACK_TILES, LANES), _ROWS)] * (2 * TOP_K)
        + [pltpu.VMEM((SC_SUB, F32_TILES, LANES), _F32), pltpu.SemaphoreType.DMA((2,))],
        name="moe_combine",
    )
    def run(ys_hbm, dest_hbm, gate_hbm, out_hbm, idx_v, g_v, *rest):
        sets = (rest[:TOP_K], rest[TOP_K:2 * TOP_K])
        acc_v, sems = rest[2 * TOP_K:]
        wid = lax.axis_index("s") * SC_CORES + lax.axis_index("c")

        def gathers(sub, which):
            off = pl.multiple_of(sub * SC_SUB, SC_SUB)
            return [pltpu.make_async_copy(ys_hbm.at[idx_v.at[k, pl.ds(off, SC_SUB)]],
                                          sets[which][k], sems.at[which]) for k in range(TOP_K)]

        def weighted_sum(c, sub, which):
            rows = sets[which]

            @pl.loop(0, SC_SUB)
            def _(tok):
                pos = jnp.full((SC_LANES,), sub * SC_SUB + tok, jnp.int32)
                gk = [plsc.load_gather(g_v, [jnp.full((SC_LANES,), k, jnp.int32), pos])
                      for k in range(TOP_K)]
                for q in range(PACK_TILES):
                    for v in range(LANES // SC_LANES):
                        lanes = pl.ds(v * SC_LANES, SC_LANES)
                        lo = hi = None
                        for k in range(TOP_K):
                            word = rows[k][tok, q, lanes]
                            a = plsc.bitcast(word << 16, _F32) * gk[k]
                            b = plsc.bitcast(word & jnp.uint32(0xFFFF0000), _F32) * gk[k]
                            lo = a if lo is None else lo + a
                            hi = b if hi is None else hi + b
                        acc_v[tok, q, lanes] = lo
                        acc_v[tok, PACK_TILES + q, lanes] = hi
            first = pl.multiple_of(c * SC_CHUNK + sub * SC_SUB, SC_SUB)
            pltpu.sync_copy(acc_v, out_hbm.at[pl.ds(first, SC_SUB)])

        @pl.loop(0, per_w)
        def _(i):
            c = wid * per_w + i
            pltpu.sync_copy(dest_hbm.at[c], idx_v)
            pltpu.sync_copy(gate_hbm.at[c], g_v)
            for cp in gathers(0, 0):
                cp.start()

            @pl.loop(0, n_sub // 2)
            def _(p):
                sub = 2 * p
                for cp in gathers(sub, 0):
                    cp.wait()
                for cp in gathers(sub + 1, 1):
                    cp.start()
                weighted_sum(c, sub, 0)
                for cp in gathers(sub + 1, 1):
                    cp.wait()

                @pl.when(p < n_sub // 2 - 1)
                def _():
                    for cp in gathers(sub + 2, 0):
                        cp.start()
                weighted_sum(c, sub + 1, 1)

    return run(ys_rows, dest3, gate3)


def _ffn_kernel(be_ref, nv_ref, nu_ref, nx_ref, xs_ref, b1_ref, b2_ref, perm_ref, w1_hbm, w2_hbm,
                dep_ref, ys_ref, w1_f, w2_f, w1_s, w2_s, sems, *, layer):
    del dep_ref
    i = pl.program_id(0)
    active = i < nu_ref[0]
    e = be_ref[i]
    new_expert = jnp.logical_or(i == 0, e != be_ref[jnp.maximum(i - 1, 0)])

    def weight_copies(expert):
        return (pltpu.make_async_copy(w1_hbm.at[layer, expert], w1_f, sems.at[0]),
                pltpu.make_async_copy(w2_hbm.at[layer, expert], w2_f, sems.at[1]))

    @pl.when(jnp.logical_and(active, i == 0))
    def _():
        for cp in weight_copies(e):
            cp.start()

    @pl.when(jnp.logical_and(active, new_expert))
    def _():
        for cp in weight_copies(e):
            cp.wait()
        half = PERM // 2
        for n in range(2 * D_FF // PERM):
            blk = w1_f[:, n * PERM:(n + 1) * PERM].astype(_BF16)
            moved = _dot(blk, perm_ref[...]).astype(_BF16)
            w1_s[:, n * half:(n + 1) * half] = moved[:, :half]
            w1_s[:, D_FF + n * half:D_FF + (n + 1) * half] = moved[:, half:]
        w2_s[...] = w2_f[...].astype(_BF16)

        @pl.when(nx_ref[i] >= 0)
        def _():
            for cp in weight_copies(nx_ref[i]):
                cp.start()

    def ffn_rows(n_rows):
        x = _load_rows(xs_ref, n_rows)
        rows = lax.broadcasted_iota(jnp.int32, (n_rows, 1), 0)
        x = jnp.where(rows < nv_ref[i], x, 0.0).astype(_BF16)
        hb = _dot(x, w1_s[...]) + b1_ref[...]
        x_glu = jnp.minimum(hb[:, :D_FF], SWIGLU_LIMIT)
        x_lin = jnp.clip(hb[:, D_FF:], -SWIGLU_LIMIT, SWIGLU_LIMIT)
        act = x_glu * jax.nn.sigmoid(SWIGLU_ALPHA * x_glu) * (x_lin + 1.0)
        y = _dot(act.astype(_BF16), w2_s[...]) + b2_ref[...]
        _store_rows(ys_ref, 0, _round_bf16(y))

    half_full = nv_ref[i] <= EXPERT_BLOCK // 2
    pl.when(jnp.logical_and(active, jnp.logical_not(half_full)))(
        functools.partial(ffn_rows, EXPERT_BLOCK))
    pl.when(jnp.logical_and(active, half_full))(
        functools.partial(ffn_rows, EXPERT_BLOCK // 2))


def _expert_ffn(xs_rows, blocks, w1, b1, w2, b2, layer, dep):
    block_e, n_valid, n_used, next_e = blocks
    n_blocks = block_e.shape[0]
    half = PERM // 2
    src = jnp.arange(PERM)
    dst = jnp.where(src % 2 == 0, src // 2, half + src // 2)
    perm = (dst[:, None] == jnp.arange(PERM)[None, :]).astype(_BF16)
    blk = lambda i, be, nv, nu, nx: (jnp.minimum(i, nu[0] - 1), 0)
    bsel = lambda i, be, nv, nu, nx: (layer, be[jnp.minimum(i, nu[0] - 1)], 0, 0)
    return pl.pallas_call(
        functools.partial(_ffn_kernel, layer=layer),
        grid_spec=pltpu.PrefetchScalarGridSpec(
            num_scalar_prefetch=4,
            grid=(n_blocks,),
            in_specs=[
                pl.BlockSpec((EXPERT_BLOCK * PACK_TILES, LANES), blk),
                pl.BlockSpec((None, None, 1, 2 * D_FF), bsel),
                pl.BlockSpec((None, None, 1, D), bsel),
                pl.BlockSpec((PERM, PERM), lambda i, be, nv, nu, nx: (0, 0)),
                pl.BlockSpec(memory_space=pl.ANY),
                pl.BlockSpec(memory_space=pl.ANY),
                pl.BlockSpec(memory_space=pl.ANY),
            ],
            out_specs=pl.BlockSpec((EXPERT_BLOCK * PACK_TILES, LANES), blk),
            scratch_shapes=[
                pltpu.VMEM((D, 2 * D_FF), _F32), pltpu.VMEM((D_FF, D), _F32),
                pltpu.VMEM((D, 2 * D_FF), _BF16), pltpu.VMEM((D_FF, D), _BF16),
                pltpu.SemaphoreType.DMA((2,)),
            ],
        ),
        out_shape=jax.ShapeDtypeStruct(xs_rows.shape, _ROWS),
        compiler_params=pltpu.CompilerParams(
            dimension_semantics=("arbitrary",), vmem_limit_bytes=VMEM_LIMIT),
        name="expert_ffn",
    )(block_e, n_valid, n_used, next_e, xs_rows, b1, b2, perm, w1, w2, dep)


def _post_kernel(x1_ref, y_ref, m2_ref, ln_ref, prev_ref, o_ref):
    del prev_ref
    o_ref[...] = _moe_output(x1_ref, y_ref, m2_ref, ln_ref)


def _moe_post(src, mb0, total_nb, prev, keep_prev):
    nb, seq, _ = src[1].shape
    n_s = seq // TS
    out_shape = jax.ShapeDtypeStruct((total_nb, seq, D), _F32)
    aliases = {_N_MOE_OPERANDS: 0} if keep_prev else {}
    return pl.pallas_call(
        _post_kernel,
        grid=(nb, n_s),
        in_specs=_moe_output_specs(n_s, mb0) + [pl.BlockSpec(memory_space=pl.ANY)],
        out_specs=pl.BlockSpec((None, TS, D), lambda b, s: (b + mb0, s, 0)),
        out_shape=out_shape,
        input_output_aliases=aliases,
        compiler_params=pltpu.CompilerParams(
            dimension_semantics=("arbitrary", "arbitrary"), vmem_limit_bytes=VMEM_LIMIT),
        name="moe_post",
    )(*src[1:], prev)


def _moe_layer(mixed, m2, ln, w1, b1, w2, b2, layer):
    n = len(mixed)
    nb, seq, _ = mixed[0][0].shape
    t = nb * seq
    cap = (t * TOP_K // EXPERT_BLOCK + N_EXPERTS) * EXPERT_BLOCK
    routes, yss = [], []
    staged = []
    for x1, h2_rows, idx, gates, rank, counts in mixed:
        dest3, gate3, blocks = _plan(idx, rank, gates, counts)
        xs = _sc_dispatch(h2_rows.reshape(t, PACK_TILES, LANES), dest3, cap)
        staged.append((xs, blocks))
        routes.append((dest3, gate3))
    for g, (xs, blocks) in enumerate(staged):
        dep = mixed[n - 1][5] if g == 0 else yss[g - 1]
        yss.append(_expert_ffn(xs.reshape(cap * PACK_TILES, LANES), blocks, w1, b1, w2, b2,
                               layer, dep))
    srcs = []
    for g, (ys, (dest3, gate3)) in enumerate(zip(yss, routes)):
        y = _sc_combine(ys.reshape(cap, PACK_TILES, LANES), dest3, gate3, t)
        srcs.append(('moe', mixed[g][0], y.reshape(t * F32_TILES, LANES), m2, ln))
    return srcs, yss[n - 1]


def kernel(x, c, mod_w, mod_b, ln_g, ln_b, ab_w_in, pool_w, pool_scale, conv_w, conv_b, lru_w_a, lru_b_a, lru_w_x, lru_b_x, lru_lambda, ab_w_out, ab_b_out, sgu_w_in, sgu_b_in, sgu_ln_g, sgu_ln_b, sgu_w_s, sgu_b_s, sgu_w_out, sgu_b_out, router_w, router_b, moe_w1, moe_b1, moe_w2, moe_b2):
    assert x.shape[1] % TS == 0 and x.shape[2] == D
    mods = _modulation(c, mod_w, mod_b)
    tri = (lax.broadcasted_iota(jnp.int32, (TS, TS), 0)
           < lax.broadcasted_iota(jnp.int32, (TS, TS), 1)).astype(_BF16)
    b1 = jnp.concatenate([moe_b1[..., 0::2], moe_b1[..., 1::2]], axis=-1)[:, :, None, :]
    b2 = moe_b2[:, :, None, :]
    nb = x.shape[0]
    n_groups = 2 if nb % 2 == 0 else 1
    gb = nb // n_groups
    srcs = [('raw', x, g * gb) for g in range(n_groups)]
    order = tri
    for layer in range(DEPTH):
        j = layer // 2
        m1, m2 = mods[layer, 0], mods[layer, 1]
        ln1 = jnp.stack([ln_g[layer, 0], ln_b[layer, 0]])
        ln2 = jnp.stack([ln_g[layer, 1], ln_b[layer, 1]])
        wr_hi, wr_lo = _split(router_w[layer].T)
        router = (wr_hi, wr_lo, router_b[layer][:, None])
        mixed = []
        for g in range(n_groups):
            span = (g * gb, gb)
            if layer % 2 == 0:
                outs = _even_layer(srcs[g], span, order, m1, m2, ln1, router, tri, ab_w_in[j],
                                   pool_w[j], pool_scale[j], conv_w[j], conv_b[j], lru_w_a[j],
                                   lru_b_a[j], lru_w_x[j], lru_b_x[j], lru_lambda[j], ab_w_out[j],
                                   ab_b_out[j])
            else:
                outs = _odd_layer(srcs[g], span, order, m1, m2, ln1, router, tri, sgu_w_in[j],
                                  sgu_b_in[j], sgu_ln_g[j], sgu_ln_b[j], sgu_w_s[j], sgu_b_s[j],
                                  sgu_w_out[j], sgu_b_out[j])
            mixed.append(outs)
            order = outs[5]
        srcs, order = _moe_layer(mixed, m2, ln2, moe_w1, b1, moe_w2, b2, layer)
    out = order
    for g in range(n_groups):
        out = _moe_post(srcs[g], g * gb, nb, out, keep_prev=g > 0)
    return out
```

```python
import functools
import math

import jax
import jax.numpy as jnp
from jax import lax
from jax.experimental import pallas as pl
from jax.experimental.pallas import tpu as pltpu
from jax.experimental.pallas import tpu_sc as plsc

D = 1024
DEPTH = 4
POOL_WINDOWS = (2, 4, 8, 16)
POOL_GROUP = 128
D_POOL = 512
D_REC = 512
N_REC_HEADS = 8
REC_HEAD = 64
CONV_WIDTH = 4
LRU_C = 8.0
D_IN_AB = 1536
CHUNK = 128
N_SGU_HEADS = 8
N_EXPERTS = 32
TOP_K = 4
D_FF = 1024
SWIGLU_LIMIT = 7.0
SWIGLU_ALPHA = 1.702
EXPERT_BLOCK = 512
LN_EPS = 1e-5
ALPHA = (2 * DEPTH) ** 0.25

TS = 512
LANES = 128
PACK_TILES = D // (2 * LANES)
_ROWS = jnp.uint32
POOL_PAD = 32
CONV_PAD = 8
SCAN_CHUNKS = 8
SCAN_LEN = TS // SCAN_CHUNKS
SCAN_PITCH = SCAN_LEN + 8
SC_CORES = 2
SC_SUBCORES = 16
F32_TILES = D // LANES
SC_CHUNK = 128
SC_SUB = 16
SC_LANES = 16
PERM = 256
DEST_LANES = 8192
VMEM_LIMIT = 56 * 1024 * 1024

_F32 = jnp.float32
_BF16 = jnp.bfloat16
_NT = (((1,), (1,)), ((), ()))


def _split(a):
    hi = a.astype(_BF16)
    lo = (a - hi.astype(_F32)).astype(_BF16)
    return hi, lo


def _dot(a, b):
    return jnp.dot(a, b, preferred_element_type=_F32)


def _layer_norm(x, g, b):
    mu = jnp.mean(x, axis=-1, keepdims=True)
    xc = x - mu
    var = jnp.mean(xc * xc, axis=-1, keepdims=True)
    return xc * lax.rsqrt(var + LN_EPS) * g + b


def _gelu(x):
    return jax.nn.gelu(x)


def _mod_kernel(c_ref, w_ref, b_ref, o_ref):
    c = c_ref[...]
    ca = c * jax.nn.sigmoid(c)
    a_hi, a_lo = _split(ca)
    w_hi, w_lo = _split(w_ref[...])
    o_ref[...] = _dot(a_hi, w_hi) + _dot(a_lo, w_hi) + _dot(a_hi, w_lo) + b_ref[...]


def _modulation(c, mod_w, mod_b):
    nb = c.shape[0]
    n = DEPTH * 2
    w = mod_w.reshape(n, D, 3 * D)
    b = mod_b.reshape(n, 1, 3 * D)
    out = pl.pallas_call(
        _mod_kernel,
        grid=(n, 3),
        in_specs=[
            pl.BlockSpec((nb, D), lambda i, j: (0, 0)),
            pl.BlockSpec((None, D, D), lambda i, j: (i, 0, j)),
            pl.BlockSpec((None, 1, D), lambda i, j: (i, 0, j)),
        ],
        out_specs=pl.BlockSpec((None, nb, D), lambda i, j: (i, 0, j)),
        out_shape=jax.ShapeDtypeStruct((n, nb, 3 * D), _F32),
        compiler_params=pltpu.CompilerParams(
            dimension_semantics=("arbitrary", "arbitrary"), vmem_limit_bytes=VMEM_LIMIT),
        name="modulation",
    )(c, w, b)
    return out.reshape(DEPTH, 2, nb, 3, D)


def _round_bf16(val):
    return val.astype(_BF16).astype(_F32)


def _store_rows(ref, r0, rounded):
    rows = rounded.shape[0]
    for q in range(PACK_TILES):
        lo = rounded[:, q * LANES:(q + 1) * LANES]
        hi = rounded[:, D // 2 + q * LANES:D // 2 + (q + 1) * LANES]
        word = (pltpu.bitcast(lo, jnp.uint32) >> 16) | pltpu.bitcast(hi, jnp.uint32)
        ref[pl.ds(r0 * PACK_TILES + q, rows, stride=PACK_TILES), :] = word


def _load_rows(ref, rows):
    lo, hi = [], []
    for q in range(PACK_TILES):
        word = ref[pl.ds(q, rows, stride=PACK_TILES), :]
        lo.append(pltpu.bitcast(word << 16, _F32))
        hi.append(pltpu.bitcast(word & jnp.uint32(0xFFFF0000), _F32))
    return jnp.concatenate(lo + hi, axis=1)


def _moe_output(x1_ref, y_ref, m2_ref, ln_ref):
    y = jnp.concatenate(
        [y_ref[pl.ds(j, TS, stride=F32_TILES), :] for j in range(F32_TILES)], axis=1)
    gate2 = 1.0 + m2_ref[2:3, :]
    return _layer_norm(ALPHA * x1_ref[...] + gate2 * y, ln_ref[0:1, :], ln_ref[1:2, :])


_N_MOE_OPERANDS = 4


def _moe_output_specs(n_seq_tiles, mb0):
    tile = lambda b, s: b * n_seq_tiles + s
    return [
        pl.BlockSpec((None, TS, D), lambda b, s: (b, s, 0)),
        pl.BlockSpec((TS * F32_TILES, LANES), lambda b, s: (tile(b, s), 0)),
        pl.BlockSpec((None, 3, D), lambda b, s: (b + mb0, 0, 0)),
        pl.BlockSpec((2, D), lambda b, s: (0, 0)),
    ]


def _mixer_input(refs, from_moe):
    if from_moe:
        return _moe_output(*refs[:_N_MOE_OPERANDS]), refs[_N_MOE_OPERANDS:]
    return refs[0][...], refs[1:]


def _finish_sublayer(parts, m1_ref, ln_ref, m2_ref, wr_hi_ref, wr_lo_ref, br_ref, tri_ref, first,
                     x1_ref, h2_ref, idx_ref, gate_ref, rank_ref, cnt_ref):
    @pl.when(first)
    def _():
        cnt_ref[...] = jnp.zeros_like(cnt_ref)

    gate1 = 1.0 + m1_ref[2:3, :]
    w_hi = wr_hi_ref[...]
    w_both = jnp.concatenate([w_hi, wr_lo_ref[...]], axis=0)
    seen = cnt_ref[:, 0:1]
    for r0, x, y in parts:
        n = x.shape[0]
        x1 = _layer_norm(ALPHA * x + gate1 * y, ln_ref[0:1, :], ln_ref[1:2, :])
        x1_ref[r0:r0 + n, :] = x1
        h2 = x1 * (1.0 + m2_ref[1:2, :]) + m2_ref[0:1, :]
        h2_r = _round_bf16(h2)
        _store_rows(h2_ref, r0, h2_r)

        h_hi = h2_r.astype(_BF16)
        h_lo = (h2 - h2_r).astype(_BF16)
        both = lax.dot_general(w_both, h_hi, _NT, preferred_element_type=_F32)
        logits = (both[:N_EXPERTS] + both[N_EXPERTS:]
                  + lax.dot_general(w_hi, h_lo, _NT, preferred_element_type=_F32)
                  + br_ref[...])
        iota_e = lax.broadcasted_iota(jnp.int32, (N_EXPERTS, n), 0)
        work = logits
        vals, idxs, hots = [], [], []
        for _ in range(TOP_K):
            m = jnp.max(work, axis=0, keepdims=True)
            idx = jnp.min(jnp.where(work == m, iota_e, N_EXPERTS), axis=0, keepdims=True)
            hot = iota_e == idx
            work = jnp.where(hot, -jnp.inf, work)
            vals.append(m)
            idxs.append(idx)
            hots.append(hot)
        exps = [jnp.exp(v - vals[0]) for v in vals]
        denom = exps[0] + exps[1] + exps[2] + exps[3]
        gate_ref[:, r0:r0 + n] = jnp.concatenate([e / denom for e in exps], axis=0)
        idx_ref[:, r0:r0 + n] = jnp.concatenate(idxs, axis=0)

        sel = jnp.zeros((N_EXPERTS, n), _F32)
        for hot in hots:
            sel = sel + jnp.where(hot, 1.0, 0.0)
        before = _dot(sel.astype(_BF16), tri_ref[0:n, 0:n]) + seen
        ranks = [jnp.sum(jnp.where(hot, before, 0.0), axis=0, keepdims=True) for hot in hots]
        rank_ref[:, r0:r0 + n] = jnp.concatenate(ranks, axis=0).astype(jnp.int32)
        seen = seen + jnp.sum(sel, axis=1, keepdims=True)
    cnt_ref[...] = jnp.broadcast_to(seen, cnt_ref.shape)


def _epilogue_specs(n_seq_tiles, mb0):
    tile = lambda b, s: b * n_seq_tiles + s
    in_specs = [
        pl.BlockSpec((None, 3, D), lambda b, s: (b + mb0, 0, 0)),
        pl.BlockSpec((2, D), lambda b, s: (0, 0)),
        pl.BlockSpec((N_EXPERTS, D), lambda b, s: (0, 0)),
        pl.BlockSpec((N_EXPERTS, D), lambda b, s: (0, 0)),
        pl.BlockSpec((N_EXPERTS, 1), lambda b, s: (0, 0)),
        pl.BlockSpec((TS, TS), lambda b, s: (0, 0)),
        pl.BlockSpec(memory_space=pl.ANY),
    ]
    out_specs = [
        pl.BlockSpec((None, TS, D), lambda b, s: (b, s, 0)),
        pl.BlockSpec((TS * PACK_TILES, LANES), lambda b, s: (tile(b, s), 0)),
        pl.BlockSpec((TOP_K, TS), lambda b, s: (0, tile(b, s))),
        pl.BlockSpec((TOP_K, TS), lambda b, s: (0, tile(b, s))),
        pl.BlockSpec((TOP_K, TS), lambda b, s: (0, tile(b, s))),
        pl.BlockSpec((N_EXPERTS, LANES), lambda b, s: (0, 0)),
    ]
    return in_specs, out_specs


def _epilogue_out_shapes(nb, seq):
    t = nb * seq
    return [
        jax.ShapeDtypeStruct((nb, seq, D), _F32),
        jax.ShapeDtypeStruct((t * PACK_TILES, LANES), _ROWS),
        jax.ShapeDtypeStruct((TOP_K, t), jnp.int32),
        jax.ShapeDtypeStruct((TOP_K, t), _F32),
        jax.ShapeDtypeStruct((TOP_K, t), jnp.int32),
        jax.ShapeDtypeStruct((N_EXPERTS, LANES), _F32),
    ]


def _even_kernel(*refs, from_moe):
    x, refs = _mixer_input(refs, from_moe)
    (m1_ref, win_ref, poolw_ref, pscale_ref, convw_ref, convb_ref, wg_ref, bg_ref, lam_ref,
     wout_ref, bout_ref,
     m2_ref, ln_ref, wr_hi_ref, wr_lo_ref, br_ref, tri_ref, dep_ref,
     x1_ref, h2_ref, idx_ref, gate_ref, rank_ref, cnt_ref,
     p1, p2, p4, p8, cbuf, s_a, s_b, s_h, s_p, hstate) = refs
    del dep_ref
    b = pl.program_id(0)
    s = pl.program_id(1)
    h = x * (1.0 + m1_ref[1:2, :]) + m1_ref[0:1, :]
    z = _dot(h.astype(_BF16), win_ref[...])
    xp = z[:, :D_POOL]
    xr = z[:, D_POOL:D_POOL + D_REC]
    gr = z[:, D_POOL + D_REC:]

    @pl.when(s == 0)
    def _():
        p1[0:POOL_PAD, :] = jnp.zeros((POOL_PAD, D_POOL), _F32)
        p2[0:POOL_PAD, :] = jnp.zeros((POOL_PAD, D_POOL), _F32)
        p4[0:POOL_PAD, :] = jnp.zeros((POOL_PAD, 3 * POOL_GROUP), _F32)
        p8[0:POOL_PAD, :] = jnp.zeros((POOL_PAD, 2 * POOL_GROUP), _F32)
        cbuf[0:CONV_PAD, :] = jnp.zeros((CONV_PAD, D_REC), _F32)
        hstate[...] = jnp.zeros_like(hstate)

    rr = TS + 16
    p1[POOL_PAD:, :] = xp
    p2[16:, :] = p1[16:16 + rr, :] + p1[15:15 + rr, :]
    p4[16:, :] = p2[16:16 + rr, POOL_GROUP:] + p2[14:14 + rr, POOL_GROUP:]
    p8[16:, :] = p4[16:16 + rr, POOL_GROUP:] + p4[12:12 + rr, POOL_GROUP:]
    s16 = p8[POOL_PAD:, POOL_GROUP:] + p8[POOL_PAD - 8:POOL_PAD - 8 + TS, POOL_GROUP:]
    sums = (p2[POOL_PAD:, 0:POOL_GROUP], p4[POOL_PAD:, 0:POOL_GROUP],
            p8[POOL_PAD:, 0:POOL_GROUP], s16)
    pos = (s * TS + lax.broadcasted_iota(jnp.int32, (TS, 1), 0) + 1).astype(_F32)
    pooled = []
    for g, w in enumerate(POOL_WINDOWS):
        cnt = jnp.minimum(pos, float(w))
        pooled.append(sums[g] / cnt - xp[:, g * POOL_GROUP:(g + 1) * POOL_GROUP])
    pooled = jnp.concatenate(pooled, axis=1).astype(_BF16)
    y_pool = _dot(pooled, poolw_ref[...]) * pscale_ref[...]
    p1[16:POOL_PAD, :] = p1[TS + 16:TS + POOL_PAD, :]

    cbuf[CONV_PAD:, :] = xr
    xc = convb_ref[...]
    for k in range(CONV_WIDTH):
        off = CONV_PAD - (CONV_WIDTH - 1) + k
        xc = xc + cbuf[off:off + TS, :] * convw_ref[k:k + 1, :]
    cbuf[0:CONV_PAD, :] = cbuf[TS:TS + CONV_PAD, :]
    gates = _dot(xc.astype(_BF16), wg_ref[...]) + bg_ref[...]
    r_gate = jax.nn.sigmoid(gates[:, :D_REC])
    i_gate = jax.nn.sigmoid(gates[:, D_REC:])
    log_a = -LRU_C * r_gate * jax.nn.softplus(-lam_ref[...])
    a = jnp.exp(log_a)
    mult = jnp.sqrt(-jnp.tanh(log_a) * (a * a + 1.0))
    b_in = mult * (i_gate * xc)

    lane_tiles = [slice(lt * LANES, (lt + 1) * LANES) for lt in range(D_REC // LANES)]
    for lt, lanes in enumerate(lane_tiles):
        for ch in range(SCAN_CHUNKS):
            dst = pl.ds(ch * SCAN_PITCH, SCAN_LEN)
            s_a[lt, dst, :] = a[ch * SCAN_LEN:(ch + 1) * SCAN_LEN, lanes]
            s_b[lt, dst, :] = b_in[ch * SCAN_LEN:(ch + 1) * SCAN_LEN, lanes]
    h_loc = [jnp.zeros((SCAN_CHUNKS, LANES), _F32) for _ in lane_tiles]
    decay = [jnp.ones((SCAN_CHUNKS, LANES), _F32) for _ in lane_tiles]
    for step in range(SCAN_LEN):
        rows = pl.ds(step, SCAN_CHUNKS, stride=SCAN_PITCH)
        for lt in range(len(lane_tiles)):
            a_s = s_a[lt, rows, :]
            h_loc[lt] = a_s * h_loc[lt] + s_b[lt, rows, :]
            decay[lt] = a_s * decay[lt]
            s_h[lt, rows, :] = h_loc[lt]
            s_p[lt, rows, :] = decay[lt]
    cols = []
    for lt, lanes in enumerate(lane_tiles):
        carry = hstate[0:1, lanes]
        chunks = []
        for ch in range(SCAN_CHUNKS):
            src = pl.ds(ch * SCAN_PITCH, SCAN_LEN)
            chunks.append(s_h[lt, src, :] + s_p[lt, src, :] * carry)
            carry = decay[lt][ch:ch + 1, :] * carry + h_loc[lt][ch:ch + 1, :]
        hstate[0:1, lanes] = carry
        cols.append(jnp.concatenate(chunks, axis=0))
    h_rec = jnp.concatenate(cols, axis=1)
    y_rec = h_rec * _gelu(gr)

    y_mix = jnp.concatenate([y_pool, y_rec], axis=1).astype(_BF16)
    y = _dot(y_mix, wout_ref[...]) + bout_ref[...]
    first = jnp.logical_and(b == 0, s == 0)
    _finish_sublayer([(0, x, y)], m1_ref, ln_ref, m2_ref, wr_hi_ref, wr_lo_ref, br_ref, tri_ref,
                     first, x1_ref, h2_ref, idx_ref, gate_ref, rank_ref, cnt_ref)


def _block_diag(w):
    n, k, _ = w.shape
    eye = jnp.eye(n, dtype=w.dtype)
    return (eye[:, None, :, None] * w[:, :, None, :]).reshape(n * k, n * k)


def _source(src, mb0):
    if src[0] == 'raw':
        _, x, xb0 = src
        return False, x.shape[1], [pl.BlockSpec((None, TS, D), lambda b, s: (b + xb0, s, 0))], [x]
    seq = src[1].shape[1]
    return True, seq, _moe_output_specs(seq // TS, mb0), list(src[1:])


def _even_layer(src, span, dep, m1, m2, ln, router, tri, w_in, pool_w, pool_scale, conv_w, conv_b,
                w_a, b_a, w_x, b_x, lam, w_out, b_out):
    mb0, nb = span
    from_moe, seq, src_specs, src_ops = _source(src, mb0)
    n_s = seq // TS
    wr_hi, wr_lo, br = router
    wg = jnp.concatenate([_block_diag(w_a), _block_diag(w_x)], axis=1).astype(_BF16)
    bg = jnp.concatenate([b_a, b_x])[None, :]
    ep_in, ep_out = _epilogue_specs(n_s, mb0)
    const = lambda shape: pl.BlockSpec(shape, lambda b, s: (0,) * len(shape))
    in_specs = src_specs + [
        pl.BlockSpec((None, 3, D), lambda b, s: (b + mb0, 0, 0)),
        const((D, D_IN_AB)), const((D_POOL, D_POOL)), const((1, D_POOL)),
        const((CONV_WIDTH, D_REC)), const((1, D_REC)), const((D_REC, 2 * D_REC)),
        const((1, 2 * D_REC)), const((1, D_REC)), const((D, D)), const((1, D)),
    ] + ep_in
    scratch = [
        pltpu.VMEM((TS + POOL_PAD, D_POOL), _F32),
        pltpu.VMEM((TS + POOL_PAD, D_POOL), _F32),
        pltpu.VMEM((TS + POOL_PAD, 3 * POOL_GROUP), _F32),
        pltpu.VMEM((TS + POOL_PAD, 2 * POOL_GROUP), _F32),
        pltpu.VMEM((TS + CONV_PAD, D_REC), _F32),
    ] + [pltpu.VMEM((D_REC // LANES, SCAN_CHUNKS * SCAN_PITCH, LANES), _F32)] * 4 + [
        pltpu.VMEM((8, D_REC), _F32),
    ]
    return pl.pallas_call(
        functools.partial(_even_kernel, from_moe=from_moe),
        grid=(nb, n_s),
        in_specs=in_specs,
        out_specs=ep_out,
        out_shape=_epilogue_out_shapes(nb, seq),
        scratch_shapes=scratch,
        compiler_params=pltpu.CompilerParams(
            dimension_semantics=("arbitrary", "arbitrary"), vmem_limit_bytes=VMEM_LIMIT),
        name="even_mixer",
    )(*src_ops, m1, w_in.astype(_BF16), _block_diag(pool_w).astype(_BF16), pool_scale[None, :],
      conv_w, conv_b[None, :], wg, bg, lam[None, :], w_out.astype(_BF16), b_out[None, :],
      m2, ln, wr_hi, wr_lo, br, tri, dep)


def _odd_kernel(*refs, from_moe):
    x, refs = _mixer_input(refs, from_moe)
    (m1_ref, win_ref, bin_ref, sln_ref, ws_ref, bs_ref, wout_ref, bout_ref,
     m2_ref, ln_ref, wr_hi_ref, wr_lo_ref, br_ref, tri_ref, dep_ref,
     x1_ref, h2_ref, idx_ref, gate_ref, rank_ref, cnt_ref,
     sv_buf) = refs
    del dep_ref
    b = pl.program_id(0)
    s = pl.program_id(1)
    h = x * (1.0 + m1_ref[1:2, :]) + m1_ref[0:1, :]
    z = _gelu(_dot(h.astype(_BF16), win_ref[...]) + bin_ref[...])
    u = z[:, :D]
    v = _layer_norm(z[:, D:], sln_ref[0:1, :], sln_ref[1:2, :]).astype(_BF16)
    n_chunks = TS // CHUNK
    row = lax.broadcasted_iota(jnp.int32, (CHUNK, CHUNK), 0)
    col = lax.broadcasted_iota(jnp.int32, (CHUNK, CHUNK), 1)
    causal = col <= row
    for hh in range(N_SGU_HEADS):
        lanes = slice(hh * CHUNK, (hh + 1) * CHUNK)
        w_h = jnp.where(causal, ws_ref[hh], 0.0).astype(_BF16)
        v_h = jnp.concatenate(
            [v[n * CHUNK:(n + 1) * CHUNK, lanes] for n in range(n_chunks)], axis=1)
        sv_h = _dot(w_h, v_h)
        for n in range(n_chunks):
            sv_buf[n * CHUNK:(n + 1) * CHUNK, lanes] = (
                sv_h[:, n * CHUNK:(n + 1) * CHUNK] + bs_ref[hh])
    y = _dot((u * sv_buf[...]).astype(_BF16), wout_ref[...]) + bout_ref[...]
    first = jnp.logical_and(b == 0, s == 0)
    _finish_sublayer([(0, x, y)], m1_ref, ln_ref, m2_ref, wr_hi_ref, wr_lo_ref, br_ref, tri_ref,
                     first, x1_ref, h2_ref, idx_ref, gate_ref, rank_ref, cnt_ref)


def _odd_layer(src, span, dep, m1, m2, ln, router, tri, w_in, b_in, ln_g, ln_b, w_s, b_s, w_out,
               b_out):
    mb0, nb = span
    from_moe, seq, src_specs, src_ops = _source(src, mb0)
    n_s = seq // TS
    wr_hi, wr_lo, br = router
    sln = jnp.stack([ln_g, ln_b])
    bs = jnp.broadcast_to(b_s[:, :, None], (N_SGU_HEADS, CHUNK, CHUNK))
    ep_in, ep_out = _epilogue_specs(n_s, mb0)
    const = lambda shape: pl.BlockSpec(shape, lambda b, s: (0,) * len(shape))
    in_specs = src_specs + [
        pl.BlockSpec((None, 3, D), lambda b, s: (b + mb0, 0, 0)),
        const((D, 2 * D)), const((1, 2 * D)), const((2, D)),
        const((N_SGU_HEADS, CHUNK, CHUNK)), const((N_SGU_HEADS, CHUNK, CHUNK)),
        const((D, D)), const((1, D)),
    ] + ep_in
    return pl.pallas_call(
        functools.partial(_odd_kernel, from_moe=from_moe),
        grid=(nb, n_s),
        in_specs=in_specs,
        out_specs=ep_out,
        out_shape=_epilogue_out_shapes(nb, seq),
        scratch_shapes=[pltpu.VMEM((TS, D), _F32)],
        compiler_params=pltpu.CompilerParams(
            dimension_semantics=("arbitrary", "arbitrary"), vmem_limit_bytes=VMEM_LIMIT),
        name="odd_mixer",
    )(*src_ops, m1, w_in.astype(_BF16), b_in[None, :], sln, w_s, bs, w_out.astype(_BF16),
      b_out[None, :], m2, ln, wr_hi, wr_lo, br, tri, dep)


_BLOCK_SHIFT = EXPERT_BLOCK.bit_length() - 1
_TAB_ROWS = 8


def _blocks_kernel(cnt_ref, tab_ref, pstart_ref):
    n_lanes = tab_ref.shape[1]
    cnt = cnt_ref[:, 0:1].astype(jnp.int32)
    padded = ((cnt + (EXPERT_BLOCK - 1)) >> _BLOCK_SHIFT) << _BLOCK_SHIFT
    r = lax.broadcasted_iota(jnp.int32, (N_EXPERTS, N_EXPERTS), 0)
    c = lax.broadcasted_iota(jnp.int32, (N_EXPERTS, N_EXPERTS), 1)
    as_row = lambda col: jnp.sum(jnp.where(r == c, col, 0), axis=0, keepdims=True)
    pends = jnp.sum(jnp.where(c <= r, as_row(padded), 0), axis=1, keepdims=True)
    pstart = pends - padded
    total = jnp.sum(padded, axis=0, keepdims=True)
    later = jnp.where(c > r, as_row(cnt), 0) > 0
    nxt = jnp.min(jnp.where(later, c, N_EXPERTS), axis=1, keepdims=True)
    nxt = jnp.where(nxt < N_EXPERTS, nxt, -1)
    e_iota = lax.broadcasted_iota(jnp.int32, (N_EXPERTS, n_lanes), 0)
    start = lax.broadcasted_iota(jnp.int32, (N_EXPERTS, n_lanes), 1) << _BLOCK_SHIFT
    block_e = jnp.minimum(
        jnp.sum(jnp.where(start >= pends, 1, 0), axis=0, keepdims=True), N_EXPERTS - 1)
    mine = e_iota == block_e
    seg_end = jnp.sum(jnp.where(mine, pstart + cnt, 0), axis=0, keepdims=True)
    n_valid = jnp.clip(seg_end - start[0:1, :], 0, EXPERT_BLOCK)
    n_valid = jnp.where(start[0:1, :] < total, n_valid, 0)
    next_e = jnp.sum(jnp.where(mine, nxt, 0), axis=0, keepdims=True)
    n_used = jnp.broadcast_to(total >> _BLOCK_SHIFT, (1, n_lanes))
    zeros = jnp.zeros((_TAB_ROWS - 4, n_lanes), jnp.int32)
    tab_ref[...] = jnp.concatenate([block_e, n_valid, next_e, n_used, zeros], axis=0)
    pstart_ref[...] = jnp.broadcast_to(pstart, pstart_ref.shape)


def _dest_kernel(pstart_ref, idx_ref, rank_ref, gate_ref, dest_ref, gate3_ref):
    idx = idx_ref[...]
    dest = rank_ref[...]
    for e in range(N_EXPERTS):
        dest = dest + jnp.where(idx == e, pstart_ref[e], 0)
    gates = gate_ref[...]
    for ch in range(dest_ref.shape[0]):
        dest_ref[ch] = dest[:, ch * SC_CHUNK:(ch + 1) * SC_CHUNK]
        gate3_ref[ch] = gates[:, ch * SC_CHUNK:(ch + 1) * SC_CHUNK]


def _plan(idx, rank, gates, counts):
    t = idx.shape[1]
    n_blocks = (t * TOP_K) // EXPERT_BLOCK + N_EXPERTS
    n_lanes = -(-n_blocks // LANES) * LANES
    tab, pstart = pl.pallas_call(
        _blocks_kernel,
        out_shape=[jax.ShapeDtypeStruct((_TAB_ROWS, n_lanes), jnp.int32),
                   jax.ShapeDtypeStruct((N_EXPERTS, LANES), jnp.int32)],
        name="moe_blocks",
    )(counts)
    tl = math.gcd(t, DEST_LANES)
    by_token = pl.BlockSpec((TOP_K, tl), lambda i, p: (0, i))
    by_chunk = pl.BlockSpec((tl // SC_CHUNK, TOP_K, SC_CHUNK), lambda i, p: (i, 0, 0))
    chunked = (t // SC_CHUNK, TOP_K, SC_CHUNK)
    dest3, gate3 = pl.pallas_call(
        _dest_kernel,
        grid_spec=pltpu.PrefetchScalarGridSpec(
            num_scalar_prefetch=1,
            grid=(t // tl,),
            in_specs=[by_token, by_token, by_token],
            out_specs=[by_chunk, by_chunk],
        ),
        out_shape=[jax.ShapeDtypeStruct(chunked, jnp.int32), jax.ShapeDtypeStruct(chunked, _F32)],
        compiler_params=pltpu.CompilerParams(dimension_semantics=("arbitrary",)),
        name="moe_dest",
    )(pstart[:, 0], idx, rank, gates)
    return dest3, gate3, (tab[0, :n_blocks], tab[1, :n_blocks], tab[3, :1], tab[2, :n_blocks])


def _sc_mesh():
    return plsc.VectorSubcoreMesh(core_axis_name="c", subcore_axis_name="s")


def _sc_dispatch(h2_rows, dest3, cap):
    t = h2_rows.shape[0]
    per_w = (t // SC_CHUNK) // (SC_CORES * SC_SUBCORES)

    @functools.partial(
        pl.kernel, mesh=_sc_mesh(),
        out_type=jax.ShapeDtypeStruct((cap, PACK_TILES, LANES), _ROWS),
        scratch_types=[pltpu.VMEM((TOP_K, SC_CHUNK), jnp.int32),
                       pltpu.VMEM((SC_CHUNK, PACK_TILES, LANES), _ROWS)],
        name="moe_dispatch",
    )
    def run(x_hbm, dest_hbm, out_hbm, idx_v, rows_v):
        wid = lax.axis_index("s") * SC_CORES + lax.axis_index("c")

        @pl.loop(0, per_w)
        def _(i):
            c = wid * per_w + i
            pltpu.sync_copy(dest_hbm.at[c], idx_v)
            pltpu.sync_copy(x_hbm.at[pl.ds(c * SC_CHUNK, SC_CHUNK)], rows_v)
            for k in range(TOP_K):
                pltpu.sync_copy(rows_v, out_hbm.at[idx_v.at[k]])

    return run(h2_rows, dest3)


def _sc_combine(ys_rows, dest3, gate3, t):
    per_w = (t // SC_CHUNK) // (SC_CORES * SC_SUBCORES)
    n_sub = SC_CHUNK // SC_SUB
    params = pltpu.CompilerParams(needs_layout_passes=False)

    @functools.partial(
        pl.kernel, mesh=_sc_mesh(), compiler_params=params,
        out_type=jax.ShapeDtypeStruct((t, F32_TILES, LANES), _F32),
        scratch_types=[pltpu.VMEM((TOP_K, SC_CHUNK), jnp.int32),
                       pltpu.VMEM((TOP_K, SC_CHUNK), _F32)]
        + [pltpu.VMEM((SC_SUB, PACK_TILES, LANES), _ROWS)] * (2 * TOP_K)
        + [pltpu.VMEM((SC_SUB, F32_TILES, LANES), _F32), pltpu.SemaphoreType.DMA((2,))],
        name="moe_combine",
    )
    def run(ys_hbm, dest_hbm, gate_hbm, out_hbm, idx_v, g_v, *rest):
        sets = (rest[:TOP_K], rest[TOP_K:2 * TOP_K])
        acc_v, sems = rest[2 * TOP_K:]
        wid = lax.axis_index("s") * SC_CORES + lax.axis_index("c")

        def gathers(sub, which):
            off = pl.multiple_of(sub * SC_SUB, SC_SUB)
            return [pltpu.make_async_copy(ys_hbm.at[idx_v.at[k, pl.ds(off, SC_SUB)]],
                                          sets[which][k], sems.at[which]) for k in range(TOP_K)]

        def weighted_sum(c, sub, which):
            rows = sets[which]

            @pl.loop(0, SC_SUB)
            def _(tok):
                pos = jnp.full((SC_LANES,), sub * SC_SUB + tok, jnp.int32)
                gk = [plsc.load_gather(g_v, [jnp.full((SC_LANES,), k, jnp.int32), pos])
                      for k in range(TOP_K)]
                for q in range(PACK_TILES):
                    for v in range(LANES // SC_LANES):
                        lanes = pl.ds(v * SC_LANES, SC_LANES)
                        lo = hi = None
                        for k in range(TOP_K):
                            word = rows[k][tok, q, lanes]
                            a = plsc.bitcast(word << 16, _F32) * gk[k]
                            b = plsc.bitcast(word & jnp.uint32(0xFFFF0000), _F32) * gk[k]
                            lo = a if lo is None else lo + a
                            hi = b if hi is None else hi + b
                        acc_v[tok, q, lanes] = lo
                        acc_v[tok, PACK_TILES + q, lanes] = hi
            first = pl.multiple_of(c * SC_CHUNK + sub * SC_SUB, SC_SUB)
            pltpu.sync_copy(acc_v, out_hbm.at[pl.ds(first, SC_SUB)])

        @pl.loop(0, per_w)
        def _(i):
            c = wid * per_w + i
            pltpu.sync_copy(dest_hbm.at[c], idx_v)
            pltpu.sync_copy(gate_hbm.at[c], g_v)
            for cp in gathers(0, 0):
                cp.start()

            @pl.loop(0, n_sub // 2)
            def _(p):
                sub = 2 * p
                for cp in gathers(sub, 0):
                    cp.wait()
                for cp in gathers(sub + 1, 1):
                    cp.start()
                weighted_sum(c, sub, 0)
                for cp in gathers(sub + 1, 1):
                    cp.wait()

                @pl.when(p < n_sub // 2 - 1)
                def _():
                    for cp in gathers(sub + 2, 0):
                        cp.start()
                weighted_sum(c, sub + 1, 1)

    return run(ys_rows, dest3, gate3)


def _ffn_kernel(be_ref, nv_ref, nu_ref, nx_ref, xs_ref, b1_ref, b2_ref, perm_ref, w1_hbm, w2_hbm,
                dep_ref, ys_ref, w1_f, w2_f, w1_s, w2_s, sems, *, layer):
    del dep_ref
    i = pl.program_id(0)
    active = i < nu_ref[0]
    e = be_ref[i]
    new_expert = jnp.logical_or(i == 0, e != be_ref[jnp.maximum(i - 1, 0)])

    def weight_copies(expert):
        return (pltpu.make_async_copy(w1_hbm.at[layer, expert], w1_f, sems.at[0]),
                pltpu.make_async_copy(w2_hbm.at[layer, expert], w2_f, sems.at[1]))

    @pl.when(jnp.logical_and(active, i == 0))
    def _():
        for cp in weight_copies(e):
            cp.start()

    @pl.when(jnp.logical_and(active, new_expert))
    def _():
        for cp in weight_copies(e):
            cp.wait()
        half = PERM // 2
        for n in range(2 * D_FF // PERM):
            blk = w1_f[:, n * PERM:(n + 1) * PERM].astype(_BF16)
            moved = _dot(blk, perm_ref[...]).astype(_BF16)
            w1_s[:, n * half:(n + 1) * half] = moved[:, :half]
            w1_s[:, D_FF + n * half:D_FF + (n + 1) * half] = moved[:, half:]
        w2_s[...] = w2_f[...].astype(_BF16)

        @pl.when(nx_ref[i] >= 0)
        def _():
            for cp in weight_copies(nx_ref[i]):
                cp.start()

    def ffn_rows(n_rows):
        x = _load_rows(xs_ref, n_rows)
        rows = lax.broadcasted_iota(jnp.int32, (n_rows, 1), 0)
        x = jnp.where(rows < nv_ref[i], x, 0.0).astype(_BF16)
        hb = _dot(x, w1_s[...]) + b1_ref[...]
        x_glu = jnp.minimum(hb[:, :D_FF], SWIGLU_LIMIT)
        x_lin = jnp.clip(hb[:, D_FF:], -SWIGLU_LIMIT, SWIGLU_LIMIT)
        act = x_glu * jax.nn.sigmoid(SWIGLU_ALPHA * x_glu) * (x_lin + 1.0)
        y = _dot(act.astype(_BF16), w2_s[...]) + b2_ref[...]
        _store_rows(ys_ref, 0, _round_bf16(y))

    half_full = nv_ref[i] <= EXPERT_BLOCK // 2
    pl.when(jnp.logical_and(active, jnp.logical_not(half_full)))(
        functools.partial(ffn_rows, EXPERT_BLOCK))
    pl.when(jnp.logical_and(active, half_full))(
        functools.partial(ffn_rows, EXPERT_BLOCK // 2))


def _expert_ffn(xs_rows, blocks, w1, b1, w2, b2, layer, dep):
    block_e, n_valid, n_used, next_e = blocks
    n_blocks = block_e.shape[0]
    half = PERM // 2
    src = jnp.arange(PERM)
    dst = jnp.where(src % 2 == 0, src // 2, half + src // 2)
    perm = (dst[:, None] == jnp.arange(PERM)[None, :]).astype(_BF16)
    blk = lambda i, be, nv, nu, nx: (jnp.minimum(i, nu[0] - 1), 0)
    bsel = lambda i, be, nv, nu, nx: (layer, be[jnp.minimum(i, nu[0] - 1)], 0, 0)
    return pl.pallas_call(
        functools.partial(_ffn_kernel, layer=layer),
        grid_spec=pltpu.PrefetchScalarGridSpec(
            num_scalar_prefetch=4,
            grid=(n_blocks,),
            in_specs=[
                pl.BlockSpec((EXPERT_BLOCK * PACK_TILES, LANES), blk),
                pl.BlockSpec((None, None, 1, 2 * D_FF), bsel),
                pl.BlockSpec((None, None, 1, D), bsel),
                pl.BlockSpec((PERM, PERM), lambda i, be, nv, nu, nx: (0, 0)),
                pl.BlockSpec(memory_space=pl.ANY),
                pl.BlockSpec(memory_space=pl.ANY),
                pl.BlockSpec(memory_space=pl.ANY),
            ],
            out_specs=pl.BlockSpec((EXPERT_BLOCK * PACK_TILES, LANES), blk),
            scratch_shapes=[
                pltpu.VMEM((D, 2 * D_FF), _F32), pltpu.VMEM((D_FF, D), _F32),
                pltpu.VMEM((D, 2 * D_FF), _BF16), pltpu.VMEM((D_FF, D), _BF16),
                pltpu.SemaphoreType.DMA((2,)),
            ],
        ),
        out_shape=jax.ShapeDtypeStruct(xs_rows.shape, _ROWS),
        compiler_params=pltpu.CompilerParams(
            dimension_semantics=("arbitrary",), vmem_limit_bytes=VMEM_LIMIT),
        name="expert_ffn",
    )(block_e, n_valid, n_used, next_e, xs_rows, b1, b2, perm, w1, w2, dep)


def _post_kernel(x1_ref, y_ref, m2_ref, ln_ref, prev_ref, o_ref):
    del prev_ref
    o_ref[...] = _moe_output(x1_ref, y_ref, m2_ref, ln_ref)


def _moe_post(src, mb0, total_nb, prev, keep_prev):
    nb, seq, _ = src[1].shape
    n_s = seq // TS
    out_shape = jax.ShapeDtypeStruct((total_nb, seq, D), _F32)
    aliases = {_N_MOE_OPERANDS: 0} if keep_prev else {}
    return pl.pallas_call(
        _post_kernel,
        grid=(nb, n_s),
        in_specs=_moe_output_specs(n_s, mb0) + [pl.BlockSpec(memory_space=pl.ANY)],
        out_specs=pl.BlockSpec((None, TS, D), lambda b, s: (b + mb0, s, 0)),
        out_shape=out_shape,
        input_output_aliases=aliases,
        compiler_params=pltpu.CompilerParams(
            dimension_semantics=("arbitrary", "arbitrary"), vmem_limit_bytes=VMEM_LIMIT),
        name="moe_post",
    )(*src[1:], prev)


def _moe_layer(mixed, m2, ln, w1, b1, w2, b2, layer):
    n = len(mixed)
    routes, yss = [], []
    staged = []
    for x1, h2_rows, idx, gates, rank, counts in mixed:
        t = idx.shape[1]
        cap = (t * TOP_K // EXPERT_BLOCK + N_EXPERTS) * EXPERT_BLOCK
        dest3, gate3, blocks = _plan(idx, rank, gates, counts)
        xs = _sc_dispatch(h2_rows.reshape(t, PACK_TILES, LANES), dest3, cap)
        staged.append((xs.reshape(cap * PACK_TILES, LANES), blocks))
        routes.append((dest3, gate3, t, cap))
    for g, (xs, blocks) in enumerate(staged):
        dep = mixed[n - 1][5] if g == 0 else yss[g - 1]
        yss.append(_expert_ffn(xs, blocks, w1, b1, w2, b2, layer, dep))
    srcs = []
    for g, (ys, (dest3, gate3, t, cap)) in enumerate(zip(yss, routes)):
        y = _sc_combine(ys.reshape(cap, PACK_TILES, LANES), dest3, gate3, t)
        srcs.append(('moe', mixed[g][0], y.reshape(t * F32_TILES, LANES), m2, ln))
    return srcs, yss[n - 1]


def kernel(x, c, mod_w, mod_b, ln_g, ln_b, ab_w_in, pool_w, pool_scale, conv_w, conv_b, lru_w_a, lru_b_a, lru_w_x, lru_b_x, lru_lambda, ab_w_out, ab_b_out, sgu_w_in, sgu_b_in, sgu_ln_g, sgu_ln_b, sgu_w_s, sgu_b_s, sgu_w_out, sgu_b_out, router_w, router_b, moe_w1, moe_b1, moe_w2, moe_b2):
    assert x.shape[1] % TS == 0 and x.shape[2] == D
    mods = _modulation(c, mod_w, mod_b)
    tri = (lax.broadcasted_iota(jnp.int32, (TS, TS), 0)
           < lax.broadcasted_iota(jnp.int32, (TS, TS), 1)).astype(_BF16)
    b1 = jnp.concatenate([moe_b1[..., 0::2], moe_b1[..., 1::2]], axis=-1)[:, :, None, :]
    b2 = moe_b2[:, :, None, :]
    nb = x.shape[0]
    sizes = [nb] if nb == 1 else [max(1, (5 * nb) // 8), nb - max(1, (5 * nb) // 8)]
    starts = [sum(sizes[:g]) for g in range(len(sizes))]
    n_groups = len(sizes)
    srcs = [('raw', x, starts[g]) for g in range(n_groups)]
    order = tri
    for layer in range(DEPTH):
        j = layer // 2
        m1, m2 = mods[layer, 0], mods[layer, 1]
        ln1 = jnp.stack([ln_g[layer, 0], ln_b[layer, 0]])
        ln2 = jnp.stack([ln_g[layer, 1], ln_b[layer, 1]])
        wr_hi, wr_lo = _split(router_w[layer].T)
        router = (wr_hi, wr_lo, router_b[layer][:, None])
        mixed = []
        for g in range(n_groups):
            span = (starts[g], sizes[g])
            if layer % 2 == 0:
                outs = _even_layer(srcs[g], span, order, m1, m2, ln1, router, tri, ab_w_in[j],
                                   pool_w[j], pool_scale[j], conv_w[j], conv_b[j], lru_w_a[j],
                                   lru_b_a[j], lru_w_x[j], lru_b_x[j], lru_lambda[j], ab_w_out[j],
                                   ab_b_out[j])
            else:
                outs = _odd_layer(srcs[g], span, order, m1, m2, ln1, router, tri, sgu_w_in[j],
                                  sgu_b_in[j], sgu_ln_g[j], sgu_ln_b[j], sgu_w_s[j], sgu_b_s[j],
                                  sgu_w_out[j], sgu_b_out[j])
            mixed.append(outs)
            order = outs[5]
        srcs, order = _moe_layer(mixed, m2, ln2, moe_w1, b1, moe_w2, b2, layer)
    out = order
    for g in range(n_groups):
        out = _moe_post(srcs[g], starts[g], nb, out, keep_prev=g > 0)
    return out
```

```python
import functools
import math

import jax
import jax.numpy as jnp
from jax import lax
from jax.experimental import pallas as pl
from jax.experimental.pallas import tpu as pltpu
from jax.experimental.pallas import tpu_sc as plsc

D = 1024
DEPTH = 4
POOL_WINDOWS = (2, 4, 8, 16)
POOL_GROUP = 128
D_POOL = 512
D_REC = 512
N_REC_HEADS = 8
REC_HEAD = 64
CONV_WIDTH = 4
LRU_C = 8.0
D_IN_AB = 1536
CHUNK = 128
N_SGU_HEADS = 8
N_EXPERTS = 32
TOP_K = 4
D_FF = 1024
SWIGLU_LIMIT = 7.0
SWIGLU_ALPHA = 1.702
EXPERT_BLOCK = 512
LN_EPS = 1e-5
ALPHA = (2 * DEPTH) ** 0.25

TS = 512
LANES = 128
PACK_TILES = D // (2 * LANES)
_ROWS = jnp.uint32
POOL_PAD = 32
CONV_PAD = 8
SCAN_CHUNKS = 8
SCAN_LEN = TS // SCAN_CHUNKS
SCAN_PITCH = SCAN_LEN + 8
SC_CORES = 2
SC_SUBCORES = 16
F32_TILES = D // LANES
SC_CHUNK = 128
SC_SUB = 16
SC_LANES = 16
PERM = 256
DEST_LANES = 8192
VMEM_LIMIT = 56 * 1024 * 1024

_F32 = jnp.float32
_BF16 = jnp.bfloat16
_NT = (((1,), (1,)), ((), ()))


def _split(a):
    hi = a.astype(_BF16)
    lo = (a - hi.astype(_F32)).astype(_BF16)
    return hi, lo


def _dot(a, b):
    return jnp.dot(a, b, preferred_element_type=_F32)


def _layer_norm(x, g, b):
    mu = jnp.mean(x, axis=-1, keepdims=True)
    xc = x - mu
    var = jnp.mean(xc * xc, axis=-1, keepdims=True)
    return xc * lax.rsqrt(var + LN_EPS) * g + b


def _gelu(x):
    return jax.nn.gelu(x)


def _mod_kernel(c_ref, w_ref, b_ref, o_ref):
    c = c_ref[...]
    ca = c * jax.nn.sigmoid(c)
    a_hi, a_lo = _split(ca)
    w_hi, w_lo = _split(w_ref[...])
    o_ref[...] = _dot(a_hi, w_hi) + _dot(a_lo, w_hi) + _dot(a_hi, w_lo) + b_ref[...]


def _modulation(c, mod_w, mod_b):
    nb = c.shape[0]
    n = DEPTH * 2
    w = mod_w.reshape(n, D, 3 * D)
    b = mod_b.reshape(n, 1, 3 * D)
    out = pl.pallas_call(
        _mod_kernel,
        grid=(n, 3),
        in_specs=[
            pl.BlockSpec((nb, D), lambda i, j: (0, 0)),
            pl.BlockSpec((None, D, D), lambda i, j: (i, 0, j)),
            pl.BlockSpec((None, 1, D), lambda i, j: (i, 0, j)),
        ],
        out_specs=pl.BlockSpec((None, nb, D), lambda i, j: (i, 0, j)),
        out_shape=jax.ShapeDtypeStruct((n, nb, 3 * D), _F32),
        compiler_params=pltpu.CompilerParams(
            dimension_semantics=("arbitrary", "arbitrary"), vmem_limit_bytes=VMEM_LIMIT),
        name="modulation",
    )(c, w, b)
    return out.reshape(DEPTH, 2, nb, 3, D)


def _round_bf16(val):
    return val.astype(_BF16).astype(_F32)


def _store_rows(ref, r0, rounded):
    rows = rounded.shape[0]
    for q in range(PACK_TILES):
        lo = rounded[:, q * LANES:(q + 1) * LANES]
        hi = rounded[:, D // 2 + q * LANES:D // 2 + (q + 1) * LANES]
        word = (pltpu.bitcast(lo, jnp.uint32) >> 16) | pltpu.bitcast(hi, jnp.uint32)
        ref[pl.ds(r0 * PACK_TILES + q, rows, stride=PACK_TILES), :] = word


def _load_rows(ref, rows):
    lo, hi = [], []
    for q in range(PACK_TILES):
        word = ref[pl.ds(q, rows, stride=PACK_TILES), :]
        lo.append(pltpu.bitcast(word << 16, _F32))
        hi.append(pltpu.bitcast(word & jnp.uint32(0xFFFF0000), _F32))
    return jnp.concatenate(lo + hi, axis=1)


def _moe_output(x1_ref, y_ref, m2_ref, ln_ref):
    y = jnp.concatenate(
        [y_ref[pl.ds(j, TS, stride=F32_TILES), :] for j in range(F32_TILES)], axis=1)
    gate2 = 1.0 + m2_ref[2:3, :]
    return _layer_norm(ALPHA * x1_ref[...] + gate2 * y, ln_ref[0:1, :], ln_ref[1:2, :])


_N_MOE_OPERANDS = 4


def _moe_output_specs(n_seq_tiles, mb0):
    tile = lambda b, s: b * n_seq_tiles + s
    return [
        pl.BlockSpec((None, TS, D), lambda b, s: (b, s, 0)),
        pl.BlockSpec((TS * F32_TILES, LANES), lambda b, s: (tile(b, s), 0)),
        pl.BlockSpec((None, 3, D), lambda b, s: (b + mb0, 0, 0)),
        pl.BlockSpec((2, D), lambda b, s: (0, 0)),
    ]


def _mixer_input(refs, from_moe):
    if from_moe:
        return _moe_output(*refs[:_N_MOE_OPERANDS]), refs[_N_MOE_OPERANDS:]
    return refs[0][...], refs[1:]


def _finish_sublayer(parts, m1_ref, ln_ref, m2_ref, wr_hi_ref, wr_lo_ref, br_ref, tri_ref, first,
                     x1_ref, h2_ref, idx_ref, gate_ref, rank_ref, cnt_ref):
    @pl.when(first)
    def _():
        cnt_ref[...] = jnp.zeros_like(cnt_ref)

    gate1 = 1.0 + m1_ref[2:3, :]
    w_hi = wr_hi_ref[...]
    w_both = jnp.concatenate([w_hi, wr_lo_ref[...]], axis=0)
    seen = cnt_ref[:, 0:1]
    for r0, x, y in parts:
        n = x.shape[0]
        x1 = _layer_norm(ALPHA * x + gate1 * y, ln_ref[0:1, :], ln_ref[1:2, :])
        x1_ref[r0:r0 + n, :] = x1
        h2 = x1 * (1.0 + m2_ref[1:2, :]) + m2_ref[0:1, :]
        h2_r = _round_bf16(h2)
        _store_rows(h2_ref, r0, h2_r)

        h_hi = h2_r.astype(_BF16)
        h_lo = (h2 - h2_r).astype(_BF16)
        both = lax.dot_general(w_both, h_hi, _NT, preferred_element_type=_F32)
        logits = (both[:N_EXPERTS] + both[N_EXPERTS:]
                  + lax.dot_general(w_hi, h_lo, _NT, preferred_element_type=_F32)
                  + br_ref[...])
        iota_e = lax.broadcasted_iota(jnp.int32, (N_EXPERTS, n), 0)
        work = logits
        vals, idxs, hots = [], [], []
        for _ in range(TOP_K):
            m = jnp.max(work, axis=0, keepdims=True)
            idx = jnp.min(jnp.where(work == m, iota_e, N_EXPERTS), axis=0, keepdims=True)
            hot = iota_e == idx
            work = jnp.where(hot, -jnp.inf, work)
            vals.append(m)
            idxs.append(idx)
            hots.append(hot)
        exps = [jnp.exp(v - vals[0]) for v in vals]
        denom = exps[0] + exps[1] + exps[2] + exps[3]
        gate_ref[:, r0:r0 + n] = jnp.concatenate([e / denom for e in exps], axis=0)
        idx_ref[:, r0:r0 + n] = jnp.concatenate(idxs, axis=0)

        sel = jnp.zeros((N_EXPERTS, n), _F32)
        for hot in hots:
            sel = sel + jnp.where(hot, 1.0, 0.0)
        before = _dot(sel.astype(_BF16), tri_ref[0:n, 0:n]) + seen
        ranks = [jnp.sum(jnp.where(hot, before, 0.0), axis=0, keepdims=True) for hot in hots]
        rank_ref[:, r0:r0 + n] = jnp.concatenate(ranks, axis=0).astype(jnp.int32)
        seen = seen + jnp.sum(sel, axis=1, keepdims=True)
    cnt_ref[...] = jnp.broadcast_to(seen, cnt_ref.shape)


def _epilogue_specs(n_seq_tiles, mb0):
    tile = lambda b, s: b * n_seq_tiles + s
    in_specs = [
        pl.BlockSpec((None, 3, D), lambda b, s: (b + mb0, 0, 0)),
        pl.BlockSpec((2, D), lambda b, s: (0, 0)),
        pl.BlockSpec((N_EXPERTS, D), lambda b, s: (0, 0)),
        pl.BlockSpec((N_EXPERTS, D), lambda b, s: (0, 0)),
        pl.BlockSpec((N_EXPERTS, 1), lambda b, s: (0, 0)),
        pl.BlockSpec((TS, TS), lambda b, s: (0, 0)),
        pl.BlockSpec(memory_space=pl.ANY),
    ]
    out_specs = [
        pl.BlockSpec((None, TS, D), lambda b, s: (b, s, 0)),
        pl.BlockSpec((TS * PACK_TILES, LANES), lambda b, s: (tile(b, s), 0)),
        pl.BlockSpec((TOP_K, TS), lambda b, s: (0, tile(b, s))),
        pl.BlockSpec((TOP_K, TS), lambda b, s: (0, tile(b, s))),
        pl.BlockSpec((TOP_K, TS), lambda b, s: (0, tile(b, s))),
        pl.BlockSpec((N_EXPERTS, LANES), lambda b, s: (0, 0)),
    ]
    return in_specs, out_specs


def _epilogue_out_shapes(nb, seq):
    t = nb * seq
    return [
        jax.ShapeDtypeStruct((nb, seq, D), _F32),
        jax.ShapeDtypeStruct((t * PACK_TILES, LANES), _ROWS),
        jax.ShapeDtypeStruct((TOP_K, t), jnp.int32),
        jax.ShapeDtypeStruct((TOP_K, t), _F32),
        jax.ShapeDtypeStruct((TOP_K, t), jnp.int32),
        jax.ShapeDtypeStruct((N_EXPERTS, LANES), _F32),
    ]


def _even_kernel(*refs, from_moe):
    x, refs = _mixer_input(refs, from_moe)
    (m1_ref, win_ref, poolw_ref, pscale_ref, convw_ref, convb_ref, wg_ref, bg_ref, lam_ref,
     wout_ref, bout_ref,
     m2_ref, ln_ref, wr_hi_ref, wr_lo_ref, br_ref, tri_ref, dep_ref,
     x1_ref, h2_ref, idx_ref, gate_ref, rank_ref, cnt_ref,
     p1, p2, p4, p8, cbuf, s_a, s_b, s_h, s_p, hstate) = refs
    del dep_ref
    b = pl.program_id(0)
    s = pl.program_id(1)
    h = x * (1.0 + m1_ref[1:2, :]) + m1_ref[0:1, :]
    z = _dot(h.astype(_BF16), win_ref[...])
    xp = z[:, :D_POOL]
    xr = z[:, D_POOL:D_POOL + D_REC]
    gr = z[:, D_POOL + D_REC:]

    @pl.when(s == 0)
    def _():
        p1[0:POOL_PAD, :] = jnp.zeros((POOL_PAD, D_POOL), _F32)
        p2[0:POOL_PAD, :] = jnp.zeros((POOL_PAD, D_POOL), _F32)
        p4[0:POOL_PAD, :] = jnp.zeros((POOL_PAD, 3 * POOL_GROUP), _F32)
        p8[0:POOL_PAD, :] = jnp.zeros((POOL_PAD, 2 * POOL_GROUP), _F32)
        cbuf[0:CONV_PAD, :] = jnp.zeros((CONV_PAD, D_REC), _F32)
        hstate[...] = jnp.zeros_like(hstate)

    rr = TS + 16
    p1[POOL_PAD:, :] = xp
    p2[16:, :] = p1[16:16 + rr, :] + p1[15:15 + rr, :]
    p4[16:, :] = p2[16:16 + rr, POOL_GROUP:] + p2[14:14 + rr, POOL_GROUP:]
    p8[16:, :] = p4[16:16 + rr, POOL_GROUP:] + p4[12:12 + rr, POOL_GROUP:]
    s16 = p8[POOL_PAD:, POOL_GROUP:] + p8[POOL_PAD - 8:POOL_PAD - 8 + TS, POOL_GROUP:]
    sums = (p2[POOL_PAD:, 0:POOL_GROUP], p4[POOL_PAD:, 0:POOL_GROUP],
            p8[POOL_PAD:, 0:POOL_GROUP], s16)
    pos = (s * TS + lax.broadcasted_iota(jnp.int32, (TS, 1), 0) + 1).astype(_F32)
    pooled = []
    for g, w in enumerate(POOL_WINDOWS):
        cnt = jnp.minimum(pos, float(w))
        pooled.append(sums[g] / cnt - xp[:, g * POOL_GROUP:(g + 1) * POOL_GROUP])
    pooled = jnp.concatenate(pooled, axis=1).astype(_BF16)
    y_pool = _dot(pooled, poolw_ref[...]) * pscale_ref[...]
    p1[16:POOL_PAD, :] = p1[TS + 16:TS + POOL_PAD, :]

    cbuf[CONV_PAD:, :] = xr
    xc = convb_ref[...]
    for k in range(CONV_WIDTH):
        off = CONV_PAD - (CONV_WIDTH - 1) + k
        xc = xc + cbuf[off:off + TS, :] * convw_ref[k:k + 1, :]
    cbuf[0:CONV_PAD, :] = cbuf[TS:TS + CONV_PAD, :]
    gates = _dot(xc.astype(_BF16), wg_ref[...]) + bg_ref[...]
    r_gate = jax.nn.sigmoid(gates[:, :D_REC])
    i_gate = jax.nn.sigmoid(gates[:, D_REC:])
    log_a = -LRU_C * r_gate * jax.nn.softplus(-lam_ref[...])
    a = jnp.exp(log_a)
    mult = jnp.sqrt(-jnp.tanh(log_a) * (a * a + 1.0))
    b_in = mult * (i_gate * xc)

    lane_tiles = [slice(lt * LANES, (lt + 1) * LANES) for lt in range(D_REC // LANES)]
    for lt, lanes in enumerate(lane_tiles):
        for ch in range(SCAN_CHUNKS):
            dst = pl.ds(ch * SCAN_PITCH, SCAN_LEN)
            s_a[lt, dst, :] = a[ch * SCAN_LEN:(ch + 1) * SCAN_LEN, lanes]
            s_b[lt, dst, :] = b_in[ch * SCAN_LEN:(ch + 1) * SCAN_LEN, lanes]
    h_loc = [jnp.zeros((SCAN_CHUNKS, LANES), _F32) for _ in lane_tiles]
    decay = [jnp.ones((SCAN_CHUNKS, LANES), _F32) for _ in lane_tiles]
    for step in range(SCAN_LEN):
        rows = pl.ds(step, SCAN_CHUNKS, stride=SCAN_PITCH)
        for lt in range(len(lane_tiles)):
            a_s = s_a[lt, rows, :]
            h_loc[lt] = a_s * h_loc[lt] + s_b[lt, rows, :]
            decay[lt] = a_s * decay[lt]
            s_h[lt, rows, :] = h_loc[lt]
            s_p[lt, rows, :] = decay[lt]
    cols = []
    for lt, lanes in enumerate(lane_tiles):
        carry = hstate[0:1, lanes]
        chunks = []
        for ch in range(SCAN_CHUNKS):
            src = pl.ds(ch * SCAN_PITCH, SCAN_LEN)
            chunks.append(s_h[lt, src, :] + s_p[lt, src, :] * carry)
            carry = decay[lt][ch:ch + 1, :] * carry + h_loc[lt][ch:ch + 1, :]
        hstate[0:1, lanes] = carry
        cols.append(jnp.concatenate(chunks, axis=0))
    h_rec = jnp.concatenate(cols, axis=1)
    y_rec = h_rec * _gelu(gr)

    y_mix = jnp.concatenate([y_pool, y_rec], axis=1).astype(_BF16)
    y = _dot(y_mix, wout_ref[...]) + bout_ref[...]
    first = jnp.logical_and(b == 0, s == 0)
    _finish_sublayer([(0, x, y)], m1_ref, ln_ref, m2_ref, wr_hi_ref, wr_lo_ref, br_ref, tri_ref,
                     first, x1_ref, h2_ref, idx_ref, gate_ref, rank_ref, cnt_ref)


def _block_diag(w):
    n, k, _ = w.shape
    eye = jnp.eye(n, dtype=w.dtype)
    return (eye[:, None, :, None] * w[:, :, None, :]).reshape(n * k, n * k)


def _source(src, mb0):
    if src[0] == 'raw':
        _, x, xb0 = src
        return False, x.shape[1], [pl.BlockSpec((None, TS, D), lambda b, s: (b + xb0, s, 0))], [x]
    seq = src[1].shape[1]
    return True, seq, _moe_output_specs(seq // TS, mb0), list(src[1:])


def _even_layer(src, span, dep, m1, m2, ln, router, tri, w_in, pool_w, pool_scale, conv_w, conv_b,
                w_a, b_a, w_x, b_x, lam, w_out, b_out):
    mb0, nb = span
    from_moe, seq, src_specs, src_ops = _source(src, mb0)
    n_s = seq // TS
    wr_hi, wr_lo, br = router
    wg = jnp.concatenate([_block_diag(w_a), _block_diag(w_x)], axis=1).astype(_BF16)
    bg = jnp.concatenate([b_a, b_x])[None, :]
    ep_in, ep_out = _epilogue_specs(n_s, mb0)
    const = lambda shape: pl.BlockSpec(shape, lambda b, s: (0,) * len(shape))
    in_specs = src_specs + [
        pl.BlockSpec((None, 3, D), lambda b, s: (b + mb0, 0, 0)),
        const((D, D_IN_AB)), const((D_POOL, D_POOL)), const((1, D_POOL)),
        const((CONV_WIDTH, D_REC)), const((1, D_REC)), const((D_REC, 2 * D_REC)),
        const((1, 2 * D_REC)), const((1, D_REC)), const((D, D)), const((1, D)),
    ] + ep_in
    scratch = [
        pltpu.VMEM((TS + POOL_PAD, D_POOL), _F32),
        pltpu.VMEM((TS + POOL_PAD, D_POOL), _F32),
        pltpu.VMEM((TS + POOL_PAD, 3 * POOL_GROUP), _F32),
        pltpu.VMEM((TS + POOL_PAD, 2 * POOL_GROUP), _F32),
        pltpu.VMEM((TS + CONV_PAD, D_REC), _F32),
    ] + [pltpu.VMEM((D_REC // LANES, SCAN_CHUNKS * SCAN_PITCH, LANES), _F32)] * 4 + [
        pltpu.VMEM((8, D_REC), _F32),
    ]
    return pl.pallas_call(
        functools.partial(_even_kernel, from_moe=from_moe),
        grid=(nb, n_s),
        in_specs=in_specs,
        out_specs=ep_out,
        out_shape=_epilogue_out_shapes(nb, seq),
        scratch_shapes=scratch,
        compiler_params=pltpu.CompilerParams(
            dimension_semantics=("arbitrary", "arbitrary"), vmem_limit_bytes=VMEM_LIMIT),
        name="even_mixer",
    )(*src_ops, m1, w_in.astype(_BF16), _block_diag(pool_w).astype(_BF16), pool_scale[None, :],
      conv_w, conv_b[None, :], wg, bg, lam[None, :], w_out.astype(_BF16), b_out[None, :],
      m2, ln, wr_hi, wr_lo, br, tri, dep)


def _odd_kernel(*refs, from_moe):
    x, refs = _mixer_input(refs, from_moe)
    (m1_ref, win_ref, bin_ref, sln_ref, ws_ref, bs_ref, wout_ref, bout_ref,
     m2_ref, ln_ref, wr_hi_ref, wr_lo_ref, br_ref, tri_ref, dep_ref,
     x1_ref, h2_ref, idx_ref, gate_ref, rank_ref, cnt_ref,
     sv_buf) = refs
    del dep_ref
    b = pl.program_id(0)
    s = pl.program_id(1)
    h = x * (1.0 + m1_ref[1:2, :]) + m1_ref[0:1, :]
    z = _gelu(_dot(h.astype(_BF16), win_ref[...]) + bin_ref[...])
    u = z[:, :D]
    v = _layer_norm(z[:, D:], sln_ref[0:1, :], sln_ref[1:2, :]).astype(_BF16)
    n_chunks = TS // CHUNK
    row = lax.broadcasted_iota(jnp.int32, (CHUNK, CHUNK), 0)
    col = lax.broadcasted_iota(jnp.int32, (CHUNK, CHUNK), 1)
    causal = col <= row
    for hh in range(N_SGU_HEADS):
        lanes = slice(hh * CHUNK, (hh + 1) * CHUNK)
        w_h = jnp.where(causal, ws_ref[hh], 0.0).astype(_BF16)
        v_h = jnp.concatenate(
            [v[n * CHUNK:(n + 1) * CHUNK, lanes] for n in range(n_chunks)], axis=1)
        sv_h = _dot(w_h, v_h)
        for n in range(n_chunks):
            sv_buf[n * CHUNK:(n + 1) * CHUNK, lanes] = (
                sv_h[:, n * CHUNK:(n + 1) * CHUNK] + bs_ref[hh])
    y = _dot((u * sv_buf[...]).astype(_BF16), wout_ref[...]) + bout_ref[...]
    first = jnp.logical_and(b == 0, s == 0)
    _finish_sublayer([(0, x, y)], m1_ref, ln_ref, m2_ref, wr_hi_ref, wr_lo_ref, br_ref, tri_ref,
                     first, x1_ref, h2_ref, idx_ref, gate_ref, rank_ref, cnt_ref)


def _odd_layer(src, span, dep, m1, m2, ln, router, tri, w_in, b_in, ln_g, ln_b, w_s, b_s, w_out,
               b_out):
    mb0, nb = span
    from_moe, seq, src_specs, src_ops = _source(src, mb0)
    n_s = seq // TS
    wr_hi, wr_lo, br = router
    sln = jnp.stack([ln_g, ln_b])
    bs = jnp.broadcast_to(b_s[:, :, None], (N_SGU_HEADS, CHUNK, CHUNK))
    ep_in, ep_out = _epilogue_specs(n_s, mb0)
    const = lambda shape: pl.BlockSpec(shape, lambda b, s: (0,) * len(shape))
    in_specs = src_specs + [
        pl.BlockSpec((None, 3, D), lambda b, s: (b + mb0, 0, 0)),
        const((D, 2 * D)), const((1, 2 * D)), const((2, D)),
        const((N_SGU_HEADS, CHUNK, CHUNK)), const((N_SGU_HEADS, CHUNK, CHUNK)),
        const((D, D)), const((1, D)),
    ] + ep_in
    return pl.pallas_call(
        functools.partial(_odd_kernel, from_moe=from_moe),
        grid=(nb, n_s),
        in_specs=in_specs,
        out_specs=ep_out,
        out_shape=_epilogue_out_shapes(nb, seq),
        scratch_shapes=[pltpu.VMEM((TS, D), _F32)],
        compiler_params=pltpu.CompilerParams(
            dimension_semantics=("arbitrary", "arbitrary"), vmem_limit_bytes=VMEM_LIMIT),
        name="odd_mixer",
    )(*src_ops, m1, w_in.astype(_BF16), b_in[None, :], sln, w_s, bs, w_out.astype(_BF16),
      b_out[None, :], m2, ln, wr_hi, wr_lo, br, tri, dep)


_BLOCK_SHIFT = EXPERT_BLOCK.bit_length() - 1
_TAB_ROWS = 8


def _blocks_kernel(cnt_ref, tab_ref, pstart_ref):
    n_lanes = tab_ref.shape[1]
    cnt = cnt_ref[:, 0:1].astype(jnp.int32)
    padded = ((cnt + (EXPERT_BLOCK - 1)) >> _BLOCK_SHIFT) << _BLOCK_SHIFT
    r = lax.broadcasted_iota(jnp.int32, (N_EXPERTS, N_EXPERTS), 0)
    c = lax.broadcasted_iota(jnp.int32, (N_EXPERTS, N_EXPERTS), 1)
    as_row = lambda col: jnp.sum(jnp.where(r == c, col, 0), axis=0, keepdims=True)
    pends = jnp.sum(jnp.where(c <= r, as_row(padded), 0), axis=1, keepdims=True)
    pstart = pends - padded
    total = jnp.sum(padded, axis=0, keepdims=True)
    later = jnp.where(c > r, as_row(cnt), 0) > 0
    nxt = jnp.min(jnp.where(later, c, N_EXPERTS), axis=1, keepdims=True)
    nxt = jnp.where(nxt < N_EXPERTS, nxt, -1)
    e_iota = lax.broadcasted_iota(jnp.int32, (N_EXPERTS, n_lanes), 0)
    start = lax.broadcasted_iota(jnp.int32, (N_EXPERTS, n_lanes), 1) << _BLOCK_SHIFT
    block_e = jnp.minimum(
        jnp.sum(jnp.where(start >= pends, 1, 0), axis=0, keepdims=True), N_EXPERTS - 1)
    mine = e_iota == block_e
    seg_end = jnp.sum(jnp.where(mine, pstart + cnt, 0), axis=0, keepdims=True)
    n_valid = jnp.clip(seg_end - start[0:1, :], 0, EXPERT_BLOCK)
    n_valid = jnp.where(start[0:1, :] < total, n_valid, 0)
    next_e = jnp.sum(jnp.where(mine, nxt, 0), axis=0, keepdims=True)
    n_used = jnp.broadcast_to(total >> _BLOCK_SHIFT, (1, n_lanes))
    zeros = jnp.zeros((_TAB_ROWS - 4, n_lanes), jnp.int32)
    tab_ref[...] = jnp.concatenate([block_e, n_valid, next_e, n_used, zeros], axis=0)
    pstart_ref[...] = jnp.broadcast_to(pstart, pstart_ref.shape)


def _dest_kernel(pstart_ref, idx_ref, rank_ref, gate_ref, dest_ref, gate3_ref):
    idx = idx_ref[...]
    dest = rank_ref[...]
    for e in range(N_EXPERTS):
        dest = dest + jnp.where(idx == e, pstart_ref[e], 0)
    gates = gate_ref[...]
    for ch in range(dest_ref.shape[0]):
        dest_ref[ch] = dest[:, ch * SC_CHUNK:(ch + 1) * SC_CHUNK]
        gate3_ref[ch] = gates[:, ch * SC_CHUNK:(ch + 1) * SC_CHUNK]


def _plan(idx, rank, gates, counts):
    t = idx.shape[1]
    n_blocks = (t * TOP_K) // EXPERT_BLOCK + N_EXPERTS
    n_lanes = -(-n_blocks // LANES) * LANES
    tab, pstart = pl.pallas_call(
        _blocks_kernel,
        out_shape=[jax.ShapeDtypeStruct((_TAB_ROWS, n_lanes), jnp.int32),
                   jax.ShapeDtypeStruct((N_EXPERTS, LANES), jnp.int32)],
        name="moe_blocks",
    )(counts)
    tl = math.gcd(t, DEST_LANES)
    by_token = pl.BlockSpec((TOP_K, tl), lambda i, p: (0, i))
    by_chunk = pl.BlockSpec((tl // SC_CHUNK, TOP_K, SC_CHUNK), lambda i, p: (i, 0, 0))
    chunked = (t // SC_CHUNK, TOP_K, SC_CHUNK)
    dest3, gate3 = pl.pallas_call(
        _dest_kernel,
        grid_spec=pltpu.PrefetchScalarGridSpec(
            num_scalar_prefetch=1,
            grid=(t // tl,),
            in_specs=[by_token, by_token, by_token],
            out_specs=[by_chunk, by_chunk],
        ),
        out_shape=[jax.ShapeDtypeStruct(chunked, jnp.int32), jax.ShapeDtypeStruct(chunked, _F32)],
        compiler_params=pltpu.CompilerParams(dimension_semantics=("arbitrary",)),
        name="moe_dest",
    )(pstart[:, 0], idx, rank, gates)
    return dest3, gate3, (tab[0, :n_blocks], tab[1, :n_blocks], tab[3, :1], tab[2, :n_blocks])


def _sc_mesh():
    return plsc.VectorSubcoreMesh(core_axis_name="c", subcore_axis_name="s")


def _sc_dispatch(h2_rows, dest3, cap):
    t = h2_rows.shape[0]
    per_w = (t // SC_CHUNK) // (SC_CORES * SC_SUBCORES)

    @functools.partial(
        pl.kernel, mesh=_sc_mesh(),
        out_type=jax.ShapeDtypeStruct((cap, PACK_TILES, LANES), _ROWS),
        scratch_types=[pltpu.VMEM((TOP_K, SC_CHUNK), jnp.int32),
                       pltpu.VMEM((SC_CHUNK, PACK_TILES, LANES), _ROWS)],
        name="moe_dispatch",
    )
    def run(x_hbm, dest_hbm, out_hbm, idx_v, rows_v):
        wid = lax.axis_index("s") * SC_CORES + lax.axis_index("c")

        @pl.loop(0, per_w)
        def _(i):
            c = wid * per_w + i
            pltpu.sync_copy(dest_hbm.at[c], idx_v)
            pltpu.sync_copy(x_hbm.at[pl.ds(c * SC_CHUNK, SC_CHUNK)], rows_v)
            for k in range(TOP_K):
                pltpu.sync_copy(rows_v, out_hbm.at[idx_v.at[k]])

    return run(h2_rows, dest3)


def _sc_combine(ys_rows, dest3, gate3, t):
    per_w = (t // SC_CHUNK) // (SC_CORES * SC_SUBCORES)
    n_sub = SC_CHUNK // SC_SUB
    params = pltpu.CompilerParams(needs_layout_passes=False)

    @functools.partial(
        pl.kernel, mesh=_sc_mesh(), compiler_params=params,
        out_type=jax.ShapeDtypeStruct((t, F32_TILES, LANES), _F32),
        scratch_types=[pltpu.VMEM((TOP_K, SC_CHUNK), jnp.int32),
                       pltpu.VMEM((TOP_K, SC_CHUNK), _F32)]
        + [pltpu.VMEM((SC_SUB, PACK_TILES, LANES), _ROWS)] * (2 * TOP_K)
        + [pltpu.VMEM((SC_SUB, F32_TILES, LANES), _F32), pltpu.SemaphoreType.DMA((2,))],
        name="moe_combine",
    )
    def run(ys_hbm, dest_hbm, gate_hbm, out_hbm, idx_v, g_v, *rest):
        sets = (rest[:TOP_K], rest[TOP_K:2 * TOP_K])
        acc_v, sems = rest[2 * TOP_K:]
        wid = lax.axis_index("s") * SC_CORES + lax.axis_index("c")

        def gathers(sub, which):
            off = pl.multiple_of(sub * SC_SUB, SC_SUB)
            return [pltpu.make_async_copy(ys_hbm.at[idx_v.at[k, pl.ds(off, SC_SUB)]],
                                          sets[which][k], sems.at[which]) for k in range(TOP_K)]

        def weighted_sum(c, sub, which):
            rows = sets[which]

            @pl.loop(0, SC_SUB)
            def _(tok):
                pos = jnp.full((SC_LANES,), sub * SC_SUB + tok, jnp.int32)
                gk = [plsc.load_gather(g_v, [jnp.full((SC_LANES,), k, jnp.int32), pos])
                      for k in range(TOP_K)]
                for q in range(PACK_TILES):
                    for v in range(LANES // SC_LANES):
                        lanes = pl.ds(v * SC_LANES, SC_LANES)
                        lo = hi = None
                        for k in range(TOP_K):
                            word = rows[k][tok, q, lanes]
                            a = plsc.bitcast(word << 16, _F32) * gk[k]
                            b = plsc.bitcast(word & jnp.uint32(0xFFFF0000), _F32) * gk[k]
                            lo = a if lo is None else lo + a
                            hi = b if hi is None else hi + b
                        acc_v[tok, q, lanes] = lo
                        acc_v[tok, PACK_TILES + q, lanes] = hi
            first = pl.multiple_of(c * SC_CHUNK + sub * SC_SUB, SC_SUB)
            pltpu.sync_copy(acc_v, out_hbm.at[pl.ds(first, SC_SUB)])

        @pl.loop(0, per_w)
        def _(i):
            c = wid * per_w + i
            pltpu.sync_copy(dest_hbm.at[c], idx_v)
            pltpu.sync_copy(gate_hbm.at[c], g_v)
            for cp in gathers(0, 0):
                cp.start()

            @pl.loop(0, n_sub // 2)
            def _(p):
                sub = 2 * p
                for cp in gathers(sub, 0):
                    cp.wait()
                for cp in gathers(sub + 1, 1):
                    cp.start()
                weighted_sum(c, sub, 0)
                for cp in gathers(sub + 1, 1):
                    cp.wait()

                @pl.when(p < n_sub // 2 - 1)
                def _():
                    for cp in gathers(sub + 2, 0):
                        cp.start()
                weighted_sum(c, sub + 1, 1)

    return run(ys_rows, dest3, gate3)


def _ffn_kernel(be_ref, nv_ref, nu_ref, nx_ref, xs_ref, b1_ref, b2_ref, perm_ref, w1_hbm, w2_hbm,
                dep_ref, ys_ref, w1_f, w2_f, w1_s, w2_s, sems, *, layer):
    del dep_ref
    i = pl.program_id(0)
    active = i < nu_ref[0]
    e = be_ref[i]
    new_expert = jnp.logical_or(i == 0, e != be_ref[jnp.maximum(i - 1, 0)])

    def weight_copies(expert):
        return (pltpu.make_async_copy(w1_hbm.at[layer, expert], w1_f, sems.at[0]),
                pltpu.make_async_copy(w2_hbm.at[layer, expert], w2_f, sems.at[1]))

    @pl.when(jnp.logical_and(active, i == 0))
    def _():
        for cp in weight_copies(e):
            cp.start()

    @pl.when(jnp.logical_and(active, new_expert))
    def _():
        for cp in weight_copies(e):
            cp.wait()
        half = PERM // 2
        for n in range(2 * D_FF // PERM):
            blk = w1_f[:, n * PERM:(n + 1) * PERM].astype(_BF16)
            moved = _dot(blk, perm_ref[...]).astype(_BF16)
            w1_s[:, n * half:(n + 1) * half] = moved[:, :half]
            w1_s[:, D_FF + n * half:D_FF + (n + 1) * half] = moved[:, half:]
        w2_s[...] = w2_f[...].astype(_BF16)

        @pl.when(nx_ref[i] >= 0)
        def _():
            for cp in weight_copies(nx_ref[i]):
                cp.start()

    def ffn_rows(n_rows):
        x = _load_rows(xs_ref, n_rows)
        rows = lax.broadcasted_iota(jnp.int32, (n_rows, 1), 0)
        x = jnp.where(rows < nv_ref[i], x, 0.0).astype(_BF16)
        hb = _dot(x, w1_s[...]) + b1_ref[...]
        x_glu = jnp.minimum(hb[:, :D_FF], SWIGLU_LIMIT)
        x_lin = jnp.clip(hb[:, D_FF:], -SWIGLU_LIMIT, SWIGLU_LIMIT)
        act = x_glu * jax.nn.sigmoid(SWIGLU_ALPHA * x_glu) * (x_lin + 1.0)
        y = _dot(act.astype(_BF16), w2_s[...]) + b2_ref[...]
        _store_rows(ys_ref, 0, _round_bf16(y))

    half_full = nv_ref[i] <= EXPERT_BLOCK // 2
    pl.when(jnp.logical_and(active, jnp.logical_not(half_full)))(
        functools.partial(ffn_rows, EXPERT_BLOCK))
    pl.when(jnp.logical_and(active, half_full))(
        functools.partial(ffn_rows, EXPERT_BLOCK // 2))


def _expert_ffn(xs_rows, blocks, w1, b1, w2, b2, layer, dep):
    block_e, n_valid, n_used, next_e = blocks
    n_blocks = block_e.shape[0]
    half = PERM // 2
    src = jnp.arange(PERM)
    dst = jnp.where(src % 2 == 0, src // 2, half + src // 2)
    perm = (dst[:, None] == jnp.arange(PERM)[None, :]).astype(_BF16)
    blk = lambda i, be, nv, nu, nx: (jnp.minimum(i, nu[0] - 1), 0)
    bsel = lambda i, be, nv, nu, nx: (layer, be[jnp.minimum(i, nu[0] - 1)], 0, 0)
    return pl.pallas_call(
        functools.partial(_ffn_kernel, layer=layer),
        grid_spec=pltpu.PrefetchScalarGridSpec(
            num_scalar_prefetch=4,
            grid=(n_blocks,),
            in_specs=[
                pl.BlockSpec((EXPERT_BLOCK * PACK_TILES, LANES), blk),
                pl.BlockSpec((None, None, 1, 2 * D_FF), bsel),
                pl.BlockSpec((None, None, 1, D), bsel),
                pl.BlockSpec((PERM, PERM), lambda i, be, nv, nu, nx: (0, 0)),
                pl.BlockSpec(memory_space=pl.ANY),
                pl.BlockSpec(memory_space=pl.ANY),
                pl.BlockSpec(memory_space=pl.ANY),
            ],
            out_specs=pl.BlockSpec((EXPERT_BLOCK * PACK_TILES, LANES), blk),
            scratch_shapes=[
                pltpu.VMEM((D, 2 * D_FF), _F32), pltpu.VMEM((D_FF, D), _F32),
                pltpu.VMEM((D, 2 * D_FF), _BF16), pltpu.VMEM((D_FF, D), _BF16),
                pltpu.SemaphoreType.DMA((2,)),
            ],
        ),
        out_shape=jax.ShapeDtypeStruct(xs_rows.shape, _ROWS),
        compiler_params=pltpu.CompilerParams(
            dimension_semantics=("arbitrary",), vmem_limit_bytes=VMEM_LIMIT),
        name="expert_ffn",
    )(block_e, n_valid, n_used, next_e, xs_rows, b1, b2, perm, w1, w2, dep)


def _post_kernel(x1_ref, y_ref, m2_ref, ln_ref, prev_ref, o_ref):
    del prev_ref
    o_ref[...] = _moe_output(x1_ref, y_ref, m2_ref, ln_ref)


def _moe_post(src, mb0, total_nb, prev, keep_prev):
    nb, seq, _ = src[1].shape
    n_s = seq // TS
    out_shape = jax.ShapeDtypeStruct((total_nb, seq, D), _F32)
    aliases = {_N_MOE_OPERANDS: 0} if keep_prev else {}
    return pl.pallas_call(
        _post_kernel,
        grid=(nb, n_s),
        in_specs=_moe_output_specs(n_s, mb0) + [pl.BlockSpec(memory_space=pl.ANY)],
        out_specs=pl.BlockSpec((None, TS, D), lambda b, s: (b + mb0, s, 0)),
        out_shape=out_shape,
        input_output_aliases=aliases,
        compiler_params=pltpu.CompilerParams(
            dimension_semantics=("arbitrary", "arbitrary"), vmem_limit_bytes=VMEM_LIMIT),
        name="moe_post",
    )(*src[1:], prev)


def _moe_layer(mixed, m2, ln, w1, b1, w2, b2, layer):
    n = len(mixed)
    routes, yss = [], []
    staged = []
    for x1, h2_rows, idx, gates, rank, counts in mixed:
        t = idx.shape[1]
        cap = (t * TOP_K // EXPERT_BLOCK + N_EXPERTS) * EXPERT_BLOCK
        dest3, gate3, blocks = _plan(idx, rank, gates, counts)
        xs = _sc_dispatch(h2_rows.reshape(t, PACK_TILES, LANES), dest3, cap)
        staged.append((xs.reshape(cap * PACK_TILES, LANES), blocks))
        routes.append((dest3, gate3, t, cap))
    for g, (xs, blocks) in enumerate(staged):
        dep = mixed[n - 1][5] if g == 0 else yss[g - 1]
        yss.append(_expert_ffn(xs, blocks, w1, b1, w2, b2, layer, dep))
    srcs = []
    for g, (ys, (dest3, gate3, t, cap)) in enumerate(zip(yss, routes)):
        y = _sc_combine(ys.reshape(cap, PACK_TILES, LANES), dest3, gate3, t)
        srcs.append(('moe', mixed[g][0], y.reshape(t * F32_TILES, LANES), m2, ln))
    return srcs, yss[n - 1]


def kernel(x, c, mod_w, mod_b, ln_g, ln_b, ab_w_in, pool_w, pool_scale, conv_w, conv_b, lru_w_a, lru_b_a, lru_w_x, lru_b_x, lru_lambda, ab_w_out, ab_b_out, sgu_w_in, sgu_b_in, sgu_ln_g, sgu_ln_b, sgu_w_s, sgu_b_s, sgu_w_out, sgu_b_out, router_w, router_b, moe_w1, moe_b1, moe_w2, moe_b2):
    assert x.shape[1] % TS == 0 and x.shape[2] == D
    mods = _modulation(c, mod_w, mod_b)
    tri = (lax.broadcasted_iota(jnp.int32, (TS, TS), 0)
           < lax.broadcasted_iota(jnp.int32, (TS, TS), 1)).astype(_BF16)
    b1 = jnp.concatenate([moe_b1[..., 0::2], moe_b1[..., 1::2]], axis=-1)[:, :, None, :]
    b2 = moe_b2[:, :, None, :]
    nb = x.shape[0]
    first = max(1, (11 * nb) // 16)
    sizes = [nb] if nb == 1 else [first, nb - first]
    starts = [sum(sizes[:g]) for g in range(len(sizes))]
    n_groups = len(sizes)
    srcs = [('raw', x, starts[g]) for g in range(n_groups)]
    order = tri
    for layer in range(DEPTH):
        j = layer // 2
        m1, m2 = mods[layer, 0], mods[layer, 1]
        ln1 = jnp.stack([ln_g[layer, 0], ln_b[layer, 0]])
        ln2 = jnp.stack([ln_g[layer, 1], ln_b[layer, 1]])
        wr_hi, wr_lo = _split(router_w[layer].T)
        router = (wr_hi, wr_lo, router_b[layer][:, None])
        mixed = []
        for g in range(n_groups):
            span = (starts[g], sizes[g])
            if layer % 2 == 0:
                outs = _even_layer(srcs[g], span, order, m1, m2, ln1, router, tri, ab_w_in[j],
                                   pool_w[j], pool_scale[j], conv_w[j], conv_b[j], lru_w_a[j],
                                   lru_b_a[j], lru_w_x[j], lru_b_x[j], lru_lambda[j], ab_w_out[j],
                                   ab_b_out[j])
            else:
                outs = _odd_layer(srcs[g], span, order, m1, m2, ln1, router, tri, sgu_w_in[j],
                                  sgu_b_in[j], sgu_ln_g[j], sgu_ln_b[j], sgu_w_s[j], sgu_b_s[j],
                                  sgu_w_out[j], sgu_b_out[j])
            mixed.append(outs)
            order = outs[5]
        srcs, order = _moe_layer(mixed, m2, ln2, moe_w1, b1, moe_w2, b2, layer)
    out = order
    for g in range(n_groups):
        out = _moe_post(srcs[g], starts[g], nb, out, keep_prev=g > 0)
    return out
```

```python
import functools
import math

import jax
import jax.numpy as jnp
from jax import lax
from jax.experimental import pallas as pl
from jax.experimental.pallas import tpu as pltpu
from jax.experimental.pallas import tpu_sc as plsc

D = 1024
DEPTH = 4
POOL_WINDOWS = (2, 4, 8, 16)
POOL_GROUP = 128
D_POOL = 512
D_REC = 512
N_REC_HEADS = 8
REC_HEAD = 64
CONV_WIDTH = 4
LRU_C = 8.0
D_IN_AB = 1536
CHUNK = 128
N_SGU_HEADS = 8
N_EXPERTS = 32
TOP_K = 4
D_FF = 1024
SWIGLU_LIMIT = 7.0
SWIGLU_ALPHA = 1.702
EXPERT_BLOCK = 512
LN_EPS = 1e-5
ALPHA = (2 * DEPTH) ** 0.25

TS = 512
LANES = 128
PACK_TILES = D // (2 * LANES)
_ROWS = jnp.uint32
POOL_PAD = 32
CONV_PAD = 8
SCAN_CHUNKS = 8
SCAN_LEN = TS // SCAN_CHUNKS
SCAN_PITCH = SCAN_LEN + 8
SC_CORES = 2
SC_SUBCORES = 16
F32_TILES = D // LANES
SC_CHUNK = 128
SC_SUB = 16
SC_LANES = 16
PERM = 256
DEST_LANES = 8192
VMEM_LIMIT = 56 * 1024 * 1024

_F32 = jnp.float32
_BF16 = jnp.bfloat16
_NT = (((1,), (1,)), ((), ()))


def _split(a):
    hi = a.astype(_BF16)
    lo = (a - hi.astype(_F32)).astype(_BF16)
    return hi, lo


def _dot(a, b):
    return jnp.dot(a, b, preferred_element_type=_F32)


def _layer_norm(x, g, b):
    mu = jnp.mean(x, axis=-1, keepdims=True)
    xc = x - mu
    var = jnp.mean(xc * xc, axis=-1, keepdims=True)
    return xc * lax.rsqrt(var + LN_EPS) * g + b


def _gelu(x):
    return jax.nn.gelu(x)


def _mod_kernel(c_ref, w_ref, b_ref, o_ref):
    c = c_ref[...]
    ca = c * jax.nn.sigmoid(c)
    a_hi, a_lo = _split(ca)
    w_hi, w_lo = _split(w_ref[...])
    o_ref[...] = _dot(a_hi, w_hi) + _dot(a_lo, w_hi) + _dot(a_hi, w_lo) + b_ref[...]


def _modulation(c, mod_w, mod_b):
    nb = c.shape[0]
    n = DEPTH * 2
    w = mod_w.reshape(n, D, 3 * D)
    b = mod_b.reshape(n, 1, 3 * D)
    out = pl.pallas_call(
        _mod_kernel,
        grid=(n, 3),
        in_specs=[
            pl.BlockSpec((nb, D), lambda i, j: (0, 0)),
            pl.BlockSpec((None, D, D), lambda i, j: (i, 0, j)),
            pl.BlockSpec((None, 1, D), lambda i, j: (i, 0, j)),
        ],
        out_specs=pl.BlockSpec((None, nb, D), lambda i, j: (i, 0, j)),
        out_shape=jax.ShapeDtypeStruct((n, nb, 3 * D), _F32),
        compiler_params=pltpu.CompilerParams(
            dimension_semantics=("arbitrary", "arbitrary"), vmem_limit_bytes=VMEM_LIMIT),
        name="modulation",
    )(c, w, b)
    return out.reshape(DEPTH, 2, nb, 3, D)


def _round_bf16(val):
    return val.astype(_BF16).astype(_F32)


def _store_rows(ref, r0, rounded):
    rows = rounded.shape[0]
    for q in range(PACK_TILES):
        lo = rounded[:, q * LANES:(q + 1) * LANES]
        hi = rounded[:, D // 2 + q * LANES:D // 2 + (q + 1) * LANES]
        word = (pltpu.bitcast(lo, jnp.uint32) >> 16) | pltpu.bitcast(hi, jnp.uint32)
        ref[pl.ds(r0 * PACK_TILES + q, rows, stride=PACK_TILES), :] = word


def _load_rows(ref, rows):
    lo, hi = [], []
    for q in range(PACK_TILES):
        word = ref[pl.ds(q, rows, stride=PACK_TILES), :]
        lo.append(pltpu.bitcast(word << 16, _F32))
        hi.append(pltpu.bitcast(word & jnp.uint32(0xFFFF0000), _F32))
    return jnp.concatenate(lo + hi, axis=1)


def _moe_output(x1_ref, y_ref, m2_ref, ln_ref):
    y = jnp.concatenate(
        [y_ref[pl.ds(j, TS, stride=F32_TILES), :] for j in range(F32_TILES)], axis=1)
    gate2 = 1.0 + m2_ref[2:3, :]
    return _layer_norm(ALPHA * x1_ref[...] + gate2 * y, ln_ref[0:1, :], ln_ref[1:2, :])


_N_MOE_OPERANDS = 4


def _moe_output_specs(n_seq_tiles, mb0):
    tile = lambda b, s: b * n_seq_tiles + s
    return [
        pl.BlockSpec((None, TS, D), lambda b, s: (b, s, 0)),
        pl.BlockSpec((TS * F32_TILES, LANES), lambda b, s: (tile(b, s), 0)),
        pl.BlockSpec((None, 3, D), lambda b, s: (b + mb0, 0, 0)),
        pl.BlockSpec((2, D), lambda b, s: (0, 0)),
    ]


def _mixer_input(refs, from_moe):
    if from_moe:
        return _moe_output(*refs[:_N_MOE_OPERANDS]), refs[_N_MOE_OPERANDS:]
    return refs[0][...], refs[1:]


def _finish_sublayer(parts, m1_ref, ln_ref, m2_ref, wr_hi_ref, wr_lo_ref, br_ref, tri_ref, first,
                     x1_ref, h2_ref, idx_ref, gate_ref, rank_ref, cnt_ref):
    @pl.when(first)
    def _():
        cnt_ref[...] = jnp.zeros_like(cnt_ref)

    gate1 = 1.0 + m1_ref[2:3, :]
    w_hi = wr_hi_ref[...]
    w_both = jnp.concatenate([w_hi, wr_lo_ref[...]], axis=0)
    seen = cnt_ref[:, 0:1]
    for r0, x, y in parts:
        n = x.shape[0]
        x1 = _layer_norm(ALPHA * x + gate1 * y, ln_ref[0:1, :], ln_ref[1:2, :])
        x1_ref[r0:r0 + n, :] = x1
        h2 = x1 * (1.0 + m2_ref[1:2, :]) + m2_ref[0:1, :]
        h2_r = _round_bf16(h2)
        _store_rows(h2_ref, r0, h2_r)

        h_hi = h2_r.astype(_BF16)
        h_lo = (h2 - h2_r).astype(_BF16)
        both = lax.dot_general(w_both, h_hi, _NT, preferred_element_type=_F32)
        logits = (both[:N_EXPERTS] + both[N_EXPERTS:]
                  + lax.dot_general(w_hi, h_lo, _NT, preferred_element_type=_F32)
                  + br_ref[...])
        iota_e = lax.broadcasted_iota(jnp.int32, (N_EXPERTS, n), 0)
        work = logits
        vals, idxs, hots = [], [], []
        for _ in range(TOP_K):
            m = jnp.max(work, axis=0, keepdims=True)
            idx = jnp.min(jnp.where(work == m, iota_e, N_EXPERTS), axis=0, keepdims=True)
            hot = iota_e == idx
            work = jnp.where(hot, -jnp.inf, work)
            vals.append(m)
            idxs.append(idx)
            hots.append(hot)
        exps = [jnp.exp(v - vals[0]) for v in vals]
        denom = exps[0] + exps[1] + exps[2] + exps[3]
        gate_ref[:, r0:r0 + n] = jnp.concatenate([e / denom for e in exps], axis=0)
        idx_ref[:, r0:r0 + n] = jnp.concatenate(idxs, axis=0)

        sel = jnp.zeros((N_EXPERTS, n), _F32)
        for hot in hots:
            sel = sel + jnp.where(hot, 1.0, 0.0)
        before = _dot(sel.astype(_BF16), tri_ref[0:n, 0:n]) + seen
        ranks = [jnp.sum(jnp.where(hot, before, 0.0), axis=0, keepdims=True) for hot in hots]
        rank_ref[:, r0:r0 + n] = jnp.concatenate(ranks, axis=0).astype(jnp.int32)
        seen = seen + jnp.sum(sel, axis=1, keepdims=True)
    cnt_ref[...] = jnp.broadcast_to(seen, cnt_ref.shape)


def _epilogue_specs(n_seq_tiles, mb0):
    tile = lambda b, s: b * n_seq_tiles + s
    in_specs = [
        pl.BlockSpec((None, 3, D), lambda b, s: (b + mb0, 0, 0)),
        pl.BlockSpec((2, D), lambda b, s: (0, 0)),
        pl.BlockSpec((N_EXPERTS, D), lambda b, s: (0, 0)),
        pl.BlockSpec((N_EXPERTS, D), lambda b, s: (0, 0)),
        pl.BlockSpec((N_EXPERTS, 1), lambda b, s: (0, 0)),
        pl.BlockSpec((TS, TS), lambda b, s: (0, 0)),
        pl.BlockSpec(memory_space=pl.ANY),
    ]
    out_specs = [
        pl.BlockSpec((None, TS, D), lambda b, s: (b, s, 0)),
        pl.BlockSpec((TS * PACK_TILES, LANES), lambda b, s: (tile(b, s), 0)),
        pl.BlockSpec((TOP_K, TS), lambda b, s: (0, tile(b, s))),
        pl.BlockSpec((TOP_K, TS), lambda b, s: (0, tile(b, s))),
        pl.BlockSpec((TOP_K, TS), lambda b, s: (0, tile(b, s))),
        pl.BlockSpec((N_EXPERTS, LANES), lambda b, s: (0, 0)),
    ]
    return in_specs, out_specs


def _epilogue_out_shapes(nb, seq):
    t = nb * seq
    return [
        jax.ShapeDtypeStruct((nb, seq, D), _F32),
        jax.ShapeDtypeStruct((t * PACK_TILES, LANES), _ROWS),
        jax.ShapeDtypeStruct((TOP_K, t), jnp.int32),
        jax.ShapeDtypeStruct((TOP_K, t), _F32),
        jax.ShapeDtypeStruct((TOP_K, t), jnp.int32),
        jax.ShapeDtypeStruct((N_EXPERTS, LANES), _F32),
    ]


def _even_kernel(*refs, from_moe):
    x, refs = _mixer_input(refs, from_moe)
    (m1_ref, win_ref, poolw_ref, pscale_ref, convw_ref, convb_ref, wg_ref, bg_ref, lam_ref,
     wout_ref, bout_ref,
     m2_ref, ln_ref, wr_hi_ref, wr_lo_ref, br_ref, tri_ref, dep_ref,
     x1_ref, h2_ref, idx_ref, gate_ref, rank_ref, cnt_ref,
     p1, p2, p4, p8, cbuf, s_a, s_b, s_h, s_p, hstate) = refs
    del dep_ref
    b = pl.program_id(0)
    s = pl.program_id(1)
    h = x * (1.0 + m1_ref[1:2, :]) + m1_ref[0:1, :]
    z = _dot(h.astype(_BF16), win_ref[...])
    xp = z[:, :D_POOL]
    xr = z[:, D_POOL:D_POOL + D_REC]
    gr = z[:, D_POOL + D_REC:]

    @pl.when(s == 0)
    def _():
        p1[0:POOL_PAD, :] = jnp.zeros((POOL_PAD, D_POOL), _F32)
        p2[0:POOL_PAD, :] = jnp.zeros((POOL_PAD, D_POOL), _F32)
        p4[0:POOL_PAD, :] = jnp.zeros((POOL_PAD, 3 * POOL_GROUP), _F32)
        p8[0:POOL_PAD, :] = jnp.zeros((POOL_PAD, 2 * POOL_GROUP), _F32)
        cbuf[0:CONV_PAD, :] = jnp.zeros((CONV_PAD, D_REC), _F32)
        hstate[...] = jnp.zeros_like(hstate)

    rr = TS + 16
    p1[POOL_PAD:, :] = xp
    p2[16:, :] = p1[16:16 + rr, :] + p1[15:15 + rr, :]
    p4[16:, :] = p2[16:16 + rr, POOL_GROUP:] + p2[14:14 + rr, POOL_GROUP:]
    p8[16:, :] = p4[16:16 + rr, POOL_GROUP:] + p4[12:12 + rr, POOL_GROUP:]
    s16 = p8[POOL_PAD:, POOL_GROUP:] + p8[POOL_PAD - 8:POOL_PAD - 8 + TS, POOL_GROUP:]
    sums = (p2[POOL_PAD:, 0:POOL_GROUP], p4[POOL_PAD:, 0:POOL_GROUP],
            p8[POOL_PAD:, 0:POOL_GROUP], s16)
    pos = (s * TS + lax.broadcasted_iota(jnp.int32, (TS, 1), 0) + 1).astype(_F32)
    pooled = []
    for g, w in enumerate(POOL_WINDOWS):
        cnt = jnp.minimum(pos, float(w))
        pooled.append(sums[g] / cnt - xp[:, g * POOL_GROUP:(g + 1) * POOL_GROUP])
    pooled = jnp.concatenate(pooled, axis=1).astype(_BF16)
    y_pool = _dot(pooled, poolw_ref[...]) * pscale_ref[...]
    p1[16:POOL_PAD, :] = p1[TS + 16:TS + POOL_PAD, :]

    cbuf[CONV_PAD:, :] = xr
    xc = convb_ref[...]
    for k in range(CONV_WIDTH):
        off = CONV_PAD - (CONV_WIDTH - 1) + k
        xc = xc + cbuf[off:off + TS, :] * convw_ref[k:k + 1, :]
    cbuf[0:CONV_PAD, :] = cbuf[TS:TS + CONV_PAD, :]
    gates = _dot(xc.astype(_BF16), wg_ref[...]) + bg_ref[...]
    r_gate = jax.nn.sigmoid(gates[:, :D_REC])
    i_gate = jax.nn.sigmoid(gates[:, D_REC:])
    log_a = -LRU_C * r_gate * jax.nn.softplus(-lam_ref[...])
    a = jnp.exp(log_a)
    mult = jnp.sqrt(-jnp.tanh(log_a) * (a * a + 1.0))
    b_in = mult * (i_gate * xc)

    lane_tiles = [slice(lt * LANES, (lt + 1) * LANES) for lt in range(D_REC // LANES)]
    for lt, lanes in enumerate(lane_tiles):
        for ch in range(SCAN_CHUNKS):
            dst = pl.ds(ch * SCAN_PITCH, SCAN_LEN)
            s_a[lt, dst, :] = a[ch * SCAN_LEN:(ch + 1) * SCAN_LEN, lanes]
            s_b[lt, dst, :] = b_in[ch * SCAN_LEN:(ch + 1) * SCAN_LEN, lanes]
    h_loc = [jnp.zeros((SCAN_CHUNKS, LANES), _F32) for _ in lane_tiles]
    decay = [jnp.ones((SCAN_CHUNKS, LANES), _F32) for _ in lane_tiles]
    for step in range(SCAN_LEN):
        rows = pl.ds(step, SCAN_CHUNKS, stride=SCAN_PITCH)
        for lt in range(len(lane_tiles)):
            a_s = s_a[lt, rows, :]
            h_loc[lt] = a_s * h_loc[lt] + s_b[lt, rows, :]
            decay[lt] = a_s * decay[lt]
            s_h[lt, rows, :] = h_loc[lt]
            s_p[lt, rows, :] = decay[lt]
    cols = []
    for lt, lanes in enumerate(lane_tiles):
        carry = hstate[0:1, lanes]
        chunks = []
        for ch in range(SCAN_CHUNKS):
            src = pl.ds(ch * SCAN_PITCH, SCAN_LEN)
            chunks.append(s_h[lt, src, :] + s_p[lt, src, :] * carry)
            carry = decay[lt][ch:ch + 1, :] * carry + h_loc[lt][ch:ch + 1, :]
        hstate[0:1, lanes] = carry
        cols.append(jnp.concatenate(chunks, axis=0))
    h_rec = jnp.concatenate(cols, axis=1)
    y_rec = h_rec * _gelu(gr)

    y_mix = jnp.concatenate([y_pool, y_rec], axis=1).astype(_BF16)
    y = _dot(y_mix, wout_ref[...]) + bout_ref[...]
    first = jnp.logical_and(b == 0, s == 0)
    _finish_sublayer([(0, x, y)], m1_ref, ln_ref, m2_ref, wr_hi_ref, wr_lo_ref, br_ref, tri_ref,
                     first, x1_ref, h2_ref, idx_ref, gate_ref, rank_ref, cnt_ref)


def _block_diag(w):
    n, k, _ = w.shape
    eye = jnp.eye(n, dtype=w.dtype)
    return (eye[:, None, :, None] * w[:, :, None, :]).reshape(n * k, n * k)


def _source(src, mb0):
    if src[0] == 'raw':
        _, x, xb0 = src
        return False, x.shape[1], [pl.BlockSpec((None, TS, D), lambda b, s: (b + xb0, s, 0))], [x]
    seq = src[1].shape[1]
    return True, seq, _moe_output_specs(seq // TS, mb0), list(src[1:])


def _even_layer(src, span, dep, m1, m2, ln, router, tri, w_in, pool_w, pool_scale, conv_w, conv_b,
                w_a, b_a, w_x, b_x, lam, w_out, b_out):
    mb0, nb = span
    from_moe, seq, src_specs, src_ops = _source(src, mb0)
    n_s = seq // TS
    wr_hi, wr_lo, br = router
    wg = jnp.concatenate([_block_diag(w_a), _block_diag(w_x)], axis=1).astype(_BF16)
    bg = jnp.concatenate([b_a, b_x])[None, :]
    ep_in, ep_out = _epilogue_specs(n_s, mb0)
    const = lambda shape: pl.BlockSpec(shape, lambda b, s: (0,) * len(shape))
    in_specs = src_specs + [
        pl.BlockSpec((None, 3, D), lambda b, s: (b + mb0, 0, 0)),
        const((D, D_IN_AB)), const((D_POOL, D_POOL)), const((1, D_POOL)),
        const((CONV_WIDTH, D_REC)), const((1, D_REC)), const((D_REC, 2 * D_REC)),
        const((1, 2 * D_REC)), const((1, D_REC)), const((D, D)), const((1, D)),
    ] + ep_in
    scratch = [
        pltpu.VMEM((TS + POOL_PAD, D_POOL), _F32),
        pltpu.VMEM((TS + POOL_PAD, D_POOL), _F32),
        pltpu.VMEM((TS + POOL_PAD, 3 * POOL_GROUP), _F32),
        pltpu.VMEM((TS + POOL_PAD, 2 * POOL_GROUP), _F32),
        pltpu.VMEM((TS + CONV_PAD, D_REC), _F32),
    ] + [pltpu.VMEM((D_REC // LANES, SCAN_CHUNKS * SCAN_PITCH, LANES), _F32)] * 4 + [
        pltpu.VMEM((8, D_REC), _F32),
    ]
    return pl.pallas_call(
        functools.partial(_even_kernel, from_moe=from_moe),
        grid=(nb, n_s),
        in_specs=in_specs,
        out_specs=ep_out,
        out_shape=_epilogue_out_shapes(nb, seq),
        scratch_shapes=scratch,
        compiler_params=pltpu.CompilerParams(
            dimension_semantics=("arbitrary", "arbitrary"), vmem_limit_bytes=VMEM_LIMIT),
        name="even_mixer",
    )(*src_ops, m1, w_in.astype(_BF16), _block_diag(pool_w).astype(_BF16), pool_scale[None, :],
      conv_w, conv_b[None, :], wg, bg, lam[None, :], w_out.astype(_BF16), b_out[None, :],
      m2, ln, wr_hi, wr_lo, br, tri, dep)


def _odd_kernel(*refs, from_moe):
    x, refs = _mixer_input(refs, from_moe)
    (m1_ref, win_ref, bin_ref, sln_ref, ws_ref, bs_ref, wout_ref, bout_ref,
     m2_ref, ln_ref, wr_hi_ref, wr_lo_ref, br_ref, tri_ref, dep_ref,
     x1_ref, h2_ref, idx_ref, gate_ref, rank_ref, cnt_ref,
     sv_buf) = refs
    del dep_ref
    b = pl.program_id(0)
    s = pl.program_id(1)
    h = x * (1.0 + m1_ref[1:2, :]) + m1_ref[0:1, :]
    z = _gelu(_dot(h.astype(_BF16), win_ref[...]) + bin_ref[...])
    u = z[:, :D]
    v = _layer_norm(z[:, D:], sln_ref[0:1, :], sln_ref[1:2, :]).astype(_BF16)
    n_chunks = TS // CHUNK
    row = lax.broadcasted_iota(jnp.int32, (CHUNK, CHUNK), 0)
    col = lax.broadcasted_iota(jnp.int32, (CHUNK, CHUNK), 1)
    causal = col <= row
    for hh in range(N_SGU_HEADS):
        lanes = slice(hh * CHUNK, (hh + 1) * CHUNK)
        w_h = jnp.where(causal, ws_ref[hh], 0.0).astype(_BF16)
        v_h = jnp.concatenate(
            [v[n * CHUNK:(n + 1) * CHUNK, lanes] for n in range(n_chunks)], axis=1)
        sv_h = _dot(w_h, v_h)
        for n in range(n_chunks):
            sv_buf[n * CHUNK:(n + 1) * CHUNK, lanes] = (
                sv_h[:, n * CHUNK:(n + 1) * CHUNK] + bs_ref[hh])
    y = _dot((u * sv_buf[...]).astype(_BF16), wout_ref[...]) + bout_ref[...]
    first = jnp.logical_and(b == 0, s == 0)
    _finish_sublayer([(0, x, y)], m1_ref, ln_ref, m2_ref, wr_hi_ref, wr_lo_ref, br_ref, tri_ref,
                     first, x1_ref, h2_ref, idx_ref, gate_ref, rank_ref, cnt_ref)


def _odd_layer(src, span, dep, m1, m2, ln, router, tri, w_in, b_in, ln_g, ln_b, w_s, b_s, w_out,
               b_out):
    mb0, nb = span
    from_moe, seq, src_specs, src_ops = _source(src, mb0)
    n_s = seq // TS
    wr_hi, wr_lo, br = router
    sln = jnp.stack([ln_g, ln_b])
    bs = jnp.broadcast_to(b_s[:, :, None], (N_SGU_HEADS, CHUNK, CHUNK))
    ep_in, ep_out = _epilogue_specs(n_s, mb0)
    const = lambda shape: pl.BlockSpec(shape, lambda b, s: (0,) * len(shape))
    in_specs = src_specs + [
        pl.BlockSpec((None, 3, D), lambda b, s: (b + mb0, 0, 0)),
        const((D, 2 * D)), const((1, 2 * D)), const((2, D)),
        const((N_SGU_HEADS, CHUNK, CHUNK)), const((N_SGU_HEADS, CHUNK, CHUNK)),
        const((D, D)), const((1, D)),
    ] + ep_in
    return pl.pallas_call(
        functools.partial(_odd_kernel, from_moe=from_moe),
        grid=(nb, n_s),
        in_specs=in_specs,
        out_specs=ep_out,
        out_shape=_epilogue_out_shapes(nb, seq),
        scratch_shapes=[pltpu.VMEM((TS, D), _F32)],
        compiler_params=pltpu.CompilerParams(
            dimension_semantics=("arbitrary", "arbitrary"), vmem_limit_bytes=VMEM_LIMIT),
        name="odd_mixer",
    )(*src_ops, m1, w_in.astype(_BF16), b_in[None, :], sln, w_s, bs, w_out.astype(_BF16),
      b_out[None, :], m2, ln, wr_hi, wr_lo, br, tri, dep)


_BLOCK_SHIFT = EXPERT_BLOCK.bit_length() - 1
_TAB_ROWS = 8


def _blocks_kernel(cnt_ref, tab_ref, pstart_ref):
    n_lanes = tab_ref.shape[1]
    cnt = cnt_ref[:, 0:1].astype(jnp.int32)
    padded = ((cnt + (EXPERT_BLOCK - 1)) >> _BLOCK_SHIFT) << _BLOCK_SHIFT
    r = lax.broadcasted_iota(jnp.int32, (N_EXPERTS, N_EXPERTS), 0)
    c = lax.broadcasted_iota(jnp.int32, (N_EXPERTS, N_EXPERTS), 1)
    as_row = lambda col: jnp.sum(jnp.where(r == c, col, 0), axis=0, keepdims=True)
    pends = jnp.sum(jnp.where(c <= r, as_row(padded), 0), axis=1, keepdims=True)
    pstart = pends - padded
    total = jnp.sum(padded, axis=0, keepdims=True)
    later = jnp.where(c > r, as_row(cnt), 0) > 0
    nxt = jnp.min(jnp.where(later, c, N_EXPERTS), axis=1, keepdims=True)
    nxt = jnp.where(nxt < N_EXPERTS, nxt, -1)
    e_iota = lax.broadcasted_iota(jnp.int32, (N_EXPERTS, n_lanes), 0)
    start = lax.broadcasted_iota(jnp.int32, (N_EXPERTS, n_lanes), 1) << _BLOCK_SHIFT
    block_e = jnp.minimum(
        jnp.sum(jnp.where(start >= pends, 1, 0), axis=0, keepdims=True), N_EXPERTS - 1)
    mine = e_iota == block_e
    seg_end = jnp.sum(jnp.where(mine, pstart + cnt, 0), axis=0, keepdims=True)
    n_valid = jnp.clip(seg_end - start[0:1, :], 0, EXPERT_BLOCK)
    n_valid = jnp.where(start[0:1, :] < total, n_valid, 0)
    next_e = jnp.sum(jnp.where(mine, nxt, 0), axis=0, keepdims=True)
    n_used = jnp.broadcast_to(total >> _BLOCK_SHIFT, (1, n_lanes))
    zeros = jnp.zeros((_TAB_ROWS - 4, n_lanes), jnp.int32)
    tab_ref[...] = jnp.concatenate([block_e, n_valid, next_e, n_used, zeros], axis=0)
    pstart_ref[...] = jnp.broadcast_to(pstart, pstart_ref.shape)


def _dest_kernel(pstart_ref, idx_ref, rank_ref, gate_ref, dest_ref, gate3_ref):
    idx = idx_ref[...]
    dest = rank_ref[...]
    for e in range(N_EXPERTS):
        dest = dest + jnp.where(idx == e, pstart_ref[e], 0)
    gates = gate_ref[...]
    for ch in range(dest_ref.shape[0]):
        dest_ref[ch] = dest[:, ch * SC_CHUNK:(ch + 1) * SC_CHUNK]
        gate3_ref[ch] = gates[:, ch * SC_CHUNK:(ch + 1) * SC_CHUNK]


def _plan(idx, rank, gates, counts):
    t = idx.shape[1]
    n_blocks = (t * TOP_K) // EXPERT_BLOCK + N_EXPERTS
    n_lanes = -(-n_blocks // LANES) * LANES
    tab, pstart = pl.pallas_call(
        _blocks_kernel,
        out_shape=[jax.ShapeDtypeStruct((_TAB_ROWS, n_lanes), jnp.int32),
                   jax.ShapeDtypeStruct((N_EXPERTS, LANES), jnp.int32)],
        name="moe_blocks",
    )(counts)
    tl = math.gcd(t, DEST_LANES)
    by_token = pl.BlockSpec((TOP_K, tl), lambda i, p: (0, i))
    by_chunk = pl.BlockSpec((tl // SC_CHUNK, TOP_K, SC_CHUNK), lambda i, p: (i, 0, 0))
    chunked = (t // SC_CHUNK, TOP_K, SC_CHUNK)
    dest3, gate3 = pl.pallas_call(
        _dest_kernel,
        grid_spec=pltpu.PrefetchScalarGridSpec(
            num_scalar_prefetch=1,
            grid=(t // tl,),
            in_specs=[by_token, by_token, by_token],
            out_specs=[by_chunk, by_chunk],
        ),
        out_shape=[jax.ShapeDtypeStruct(chunked, jnp.int32), jax.ShapeDtypeStruct(chunked, _F32)],
        compiler_params=pltpu.CompilerParams(dimension_semantics=("arbitrary",)),
        name="moe_dest",
    )(pstart[:, 0], idx, rank, gates)
    return dest3, gate3, (tab[0, :n_blocks], tab[1, :n_blocks], tab[3, :1], tab[2, :n_blocks])


def _sc_mesh():
    return plsc.VectorSubcoreMesh(core_axis_name="c", subcore_axis_name="s")


def _sc_dispatch(h2_rows, dest3, cap):
    t = h2_rows.shape[0]
    per_w = (t // SC_CHUNK) // (SC_CORES * SC_SUBCORES)

    @functools.partial(
        pl.kernel, mesh=_sc_mesh(),
        out_type=jax.ShapeDtypeStruct((cap, PACK_TILES, LANES), _ROWS),
        scratch_types=[pltpu.VMEM((TOP_K, SC_CHUNK), jnp.int32),
                       pltpu.VMEM((SC_CHUNK, PACK_TILES, LANES), _ROWS)],
        name="moe_dispatch",
    )
    def run(x_hbm, dest_hbm, out_hbm, idx_v, rows_v):
        wid = lax.axis_index("s") * SC_CORES + lax.axis_index("c")

        @pl.loop(0, per_w)
        def _(i):
            c = wid * per_w + i
            pltpu.sync_copy(dest_hbm.at[c], idx_v)
            pltpu.sync_copy(x_hbm.at[pl.ds(c * SC_CHUNK, SC_CHUNK)], rows_v)
            for k in range(TOP_K):
                pltpu.sync_copy(rows_v, out_hbm.at[idx_v.at[k]])

    return run(h2_rows, dest3)


def _sc_combine(ys_rows, dest3, gate3, t):
    per_w = (t // SC_CHUNK) // (SC_CORES * SC_SUBCORES)
    n_sub = SC_CHUNK // SC_SUB
    params = pltpu.CompilerParams(needs_layout_passes=False)

    @functools.partial(
        pl.kernel, mesh=_sc_mesh(), compiler_params=params,
        out_type=jax.ShapeDtypeStruct((t, F32_TILES, LANES), _F32),
        scratch_types=[pltpu.VMEM((TOP_K, SC_CHUNK), jnp.int32),
                       pltpu.VMEM((TOP_K, SC_CHUNK), _F32)]
        + [pltpu.VMEM((SC_SUB, PACK_TILES, LANES), _ROWS)] * (2 * TOP_K)
        + [pltpu.VMEM((SC_SUB, F32_TILES, LANES), _F32), pltpu.SemaphoreType.DMA((2,))],
        name="moe_combine",
    )
    def run(ys_hbm, dest_hbm, gate_hbm, out_hbm, idx_v, g_v, *rest):
        sets = (rest[:TOP_K], rest[TOP_K:2 * TOP_K])
        acc_v, sems = rest[2 * TOP_K:]
        wid = lax.axis_index("s") * SC_CORES + lax.axis_index("c")

        def gathers(sub, which):
            off = pl.multiple_of(sub * SC_SUB, SC_SUB)
            return [pltpu.make_async_copy(ys_hbm.at[idx_v.at[k, pl.ds(off, SC_SUB)]],
                                          sets[which][k], sems.at[which]) for k in range(TOP_K)]

        def weighted_sum(c, sub, which):
            rows = sets[which]

            @pl.loop(0, SC_SUB)
            def _(tok):
                pos = jnp.full((SC_LANES,), sub * SC_SUB + tok, jnp.int32)
                gk = [plsc.load_gather(g_v, [jnp.full((SC_LANES,), k, jnp.int32), pos])
                      for k in range(TOP_K)]
                for q in range(PACK_TILES):
                    for v in range(LANES // SC_LANES):
                        lanes = pl.ds(v * SC_LANES, SC_LANES)
                        lo = hi = None
                        for k in range(TOP_K):
                            word = rows[k][tok, q, lanes]
                            a = plsc.bitcast(word << 16, _F32) * gk[k]
                            b = plsc.bitcast(word & jnp.uint32(0xFFFF0000), _F32) * gk[k]
                            lo = a if lo is None else lo + a
                            hi = b if hi is None else hi + b
                        acc_v[tok, q, lanes] = lo
                        acc_v[tok, PACK_TILES + q, lanes] = hi
            first = pl.multiple_of(c * SC_CHUNK + sub * SC_SUB, SC_SUB)
            pltpu.sync_copy(acc_v, out_hbm.at[pl.ds(first, SC_SUB)])

        @pl.loop(0, per_w)
        def _(i):
            c = wid * per_w + i
            pltpu.sync_copy(dest_hbm.at[c], idx_v)
            pltpu.sync_copy(gate_hbm.at[c], g_v)
            for cp in gathers(0, 0):
                cp.start()

            @pl.loop(0, n_sub // 2)
            def _(p):
                sub = 2 * p
                for cp in gathers(sub, 0):
                    cp.wait()
                for cp in gathers(sub + 1, 1):
                    cp.start()
                weighted_sum(c, sub, 0)
                for cp in gathers(sub + 1, 1):
                    cp.wait()

                @pl.when(p < n_sub // 2 - 1)
                def _():
                    for cp in gathers(sub + 2, 0):
                        cp.start()
                weighted_sum(c, sub + 1, 1)

    return run(ys_rows, dest3, gate3)


def _ffn_kernel(be_ref, nv_ref, nu_ref, nx_ref, xs_ref, b1_ref, b2_ref, perm_ref, w1_hbm, w2_hbm,
                dep_ref, ys_ref, w1_f, w2_f, w1_s, w2_s, sems, *, layer):
    del dep_ref
    i = pl.program_id(0)
    active = i < nu_ref[0]
    e = be_ref[i]
    new_expert = jnp.logical_or(i == 0, e != be_ref[jnp.maximum(i - 1, 0)])

    def weight_copies(expert):
        return (pltpu.make_async_copy(w1_hbm.at[layer, expert], w1_f, sems.at[0]),
                pltpu.make_async_copy(w2_hbm.at[layer, expert], w2_f, sems.at[1]))

    @pl.when(jnp.logical_and(active, i == 0))
    def _():
        for cp in weight_copies(e):
            cp.start()

    @pl.when(jnp.logical_and(active, new_expert))
    def _():
        for cp in weight_copies(e):
            cp.wait()
        half = PERM // 2
        for n in range(2 * D_FF // PERM):
            blk = w1_f[:, n * PERM:(n + 1) * PERM].astype(_BF16)
            moved = _dot(blk, perm_ref[...]).astype(_BF16)
            w1_s[:, n * half:(n + 1) * half] = moved[:, :half]
            w1_s[:, D_FF + n * half:D_FF + (n + 1) * half] = moved[:, half:]
        w2_s[...] = w2_f[...].astype(_BF16)

        @pl.when(nx_ref[i] >= 0)
        def _():
            for cp in weight_copies(nx_ref[i]):
                cp.start(priority=1)

    def ffn_rows(n_rows):
        x = _load_rows(xs_ref, n_rows)
        rows = lax.broadcasted_iota(jnp.int32, (n_rows, 1), 0)
        x = jnp.where(rows < nv_ref[i], x, 0.0).astype(_BF16)
        hb = _dot(x, w1_s[...]) + b1_ref[...]
        x_glu = jnp.minimum(hb[:, :D_FF], SWIGLU_LIMIT)
        x_lin = jnp.clip(hb[:, D_FF:], -SWIGLU_LIMIT, SWIGLU_LIMIT)
        act = x_glu * jax.nn.sigmoid(SWIGLU_ALPHA * x_glu) * (x_lin + 1.0)
        y = _dot(act.astype(_BF16), w2_s[...]) + b2_ref[...]
        _store_rows(ys_ref, 0, _round_bf16(y))

    half_full = nv_ref[i] <= EXPERT_BLOCK // 2
    pl.when(jnp.logical_and(active, jnp.logical_not(half_full)))(
        functools.partial(ffn_rows, EXPERT_BLOCK))
    pl.when(jnp.logical_and(active, half_full))(
        functools.partial(ffn_rows, EXPERT_BLOCK // 2))


def _expert_ffn(xs_rows, blocks, w1, b1, w2, b2, layer, dep):
    block_e, n_valid, n_used, next_e = blocks
    n_blocks = block_e.shape[0]
    half = PERM // 2
    src = jnp.arange(PERM)
    dst = jnp.where(src % 2 == 0, src // 2, half + src // 2)
    perm = (dst[:, None] == jnp.arange(PERM)[None, :]).astype(_BF16)
    blk = lambda i, be, nv, nu, nx: (jnp.minimum(i, nu[0] - 1), 0)
    bsel = lambda i, be, nv, nu, nx: (layer, be[jnp.minimum(i, nu[0] - 1)], 0, 0)
    return pl.pallas_call(
        functools.partial(_ffn_kernel, layer=layer),
        grid_spec=pltpu.PrefetchScalarGridSpec(
            num_scalar_prefetch=4,
            grid=(n_blocks,),
            in_specs=[
                pl.BlockSpec((EXPERT_BLOCK * PACK_TILES, LANES), blk),
                pl.BlockSpec((None, None, 1, 2 * D_FF), bsel),
                pl.BlockSpec((None, None, 1, D), bsel),
                pl.BlockSpec((PERM, PERM), lambda i, be, nv, nu, nx: (0, 0)),
                pl.BlockSpec(memory_space=pl.ANY),
                pl.BlockSpec(memory_space=pl.ANY),
                pl.BlockSpec(memory_space=pl.ANY),
            ],
            out_specs=pl.BlockSpec((EXPERT_BLOCK * PACK_TILES, LANES), blk),
            scratch_shapes=[
                pltpu.VMEM((D, 2 * D_FF), _F32), pltpu.VMEM((D_FF, D), _F32),
                pltpu.VMEM((D, 2 * D_FF), _BF16), pltpu.VMEM((D_FF, D), _BF16),
                pltpu.SemaphoreType.DMA((2,)),
            ],
        ),
        out_shape=jax.ShapeDtypeStruct(xs_rows.shape, _ROWS),
        compiler_params=pltpu.CompilerParams(
            dimension_semantics=("arbitrary",), vmem_limit_bytes=VMEM_LIMIT),
        name="expert_ffn",
    )(block_e, n_valid, n_used, next_e, xs_rows, b1, b2, perm, w1, w2, dep)


def _post_kernel(x1_ref, y_ref, m2_ref, ln_ref, prev_ref, o_ref):
    del prev_ref
    o_ref[...] = _moe_output(x1_ref, y_ref, m2_ref, ln_ref)


def _moe_post(src, mb0, total_nb, prev, keep_prev):
    nb, seq, _ = src[1].shape
    n_s = seq // TS
    out_shape = jax.ShapeDtypeStruct((total_nb, seq, D), _F32)
    aliases = {_N_MOE_OPERANDS: 0} if keep_prev else {}
    return pl.pallas_call(
        _post_kernel,
        grid=(nb, n_s),
        in_specs=_moe_output_specs(n_s, mb0) + [pl.BlockSpec(memory_space=pl.ANY)],
        out_specs=pl.BlockSpec((None, TS, D), lambda b, s: (b + mb0, s, 0)),
        out_shape=out_shape,
        input_output_aliases=aliases,
        compiler_params=pltpu.CompilerParams(
            dimension_semantics=("arbitrary", "arbitrary"), vmem_limit_bytes=VMEM_LIMIT),
        name="moe_post",
    )(*src[1:], prev)


def _moe_layer(mixed, m2, ln, w1, b1, w2, b2, layer):
    n = len(mixed)
    routes, yss = [], []
    staged = []
    for x1, h2_rows, idx, gates, rank, counts in mixed:
        t = idx.shape[1]
        cap = (t * TOP_K // EXPERT_BLOCK + N_EXPERTS) * EXPERT_BLOCK
        dest3, gate3, blocks = _plan(idx, rank, gates, counts)
        xs = _sc_dispatch(h2_rows.reshape(t, PACK_TILES, LANES), dest3, cap)
        staged.append((xs.reshape(cap * PACK_TILES, LANES), blocks))
        routes.append((dest3, gate3, t, cap))
    for g, (xs, blocks) in enumerate(staged):
        dep = mixed[n - 1][5] if g == 0 else yss[g - 1]
        yss.append(_expert_ffn(xs, blocks, w1, b1, w2, b2, layer, dep))
    srcs = []
    for g, (ys, (dest3, gate3, t, cap)) in enumerate(zip(yss, routes)):
        y = _sc_combine(ys.reshape(cap, PACK_TILES, LANES), dest3, gate3, t)
        srcs.append(('moe', mixed[g][0], y.reshape(t * F32_TILES, LANES), m2, ln))
    return srcs, yss[n - 1]


def kernel(x, c, mod_w, mod_b, ln_g, ln_b, ab_w_in, pool_w, pool_scale, conv_w, conv_b, lru_w_a, lru_b_a, lru_w_x, lru_b_x, lru_lambda, ab_w_out, ab_b_out, sgu_w_in, sgu_b_in, sgu_ln_g, sgu_ln_b, sgu_w_s, sgu_b_s, sgu_w_out, sgu_b_out, router_w, router_b, moe_w1, moe_b1, moe_w2, moe_b2):
    assert x.shape[1] % TS == 0 and x.shape[2] == D
    mods = _modulation(c, mod_w, mod_b)
    tri = (lax.broadcasted_iota(jnp.int32, (TS, TS), 0)
           < lax.broadcasted_iota(jnp.int32, (TS, TS), 1)).astype(_BF16)
    b1 = jnp.concatenate([moe_b1[..., 0::2], moe_b1[..., 1::2]], axis=-1)[:, :, None, :]
    b2 = moe_b2[:, :, None, :]
    nb = x.shape[0]
    first = max(1, (11 * nb) // 16)
    sizes = [nb] if nb == 1 else [first, nb - first]
    starts = [sum(sizes[:g]) for g in range(len(sizes))]
    n_groups = len(sizes)
    srcs = [('raw', x, starts[g]) for g in range(n_groups)]
    order = tri
    for layer in range(DEPTH):
        j = layer // 2
        m1, m2 = mods[layer, 0], mods[layer, 1]
        ln1 = jnp.stack([ln_g[layer, 0], ln_b[layer, 0]])
        ln2 = jnp.stack([ln_g[layer, 1], ln_b[layer, 1]])
        wr_hi, wr_lo = _split(router_w[layer].T)
        router = (wr_hi, wr_lo, router_b[layer][:, None])
        mixed = []
        for g in range(n_groups):
            span = (starts[g], sizes[g])
            if layer % 2 == 0:
                outs = _even_layer(srcs[g], span, order, m1, m2, ln1, router, tri, ab_w_in[j],
                                   pool_w[j], pool_scale[j], conv_w[j], conv_b[j], lru_w_a[j],
                                   lru_b_a[j], lru_w_x[j], lru_b_x[j], lru_lambda[j], ab_w_out[j],
                                   ab_b_out[j])
            else:
                outs = _odd_layer(srcs[g], span, order, m1, m2, ln1, router, tri, sgu_w_in[j],
                                  sgu_b_in[j], sgu_ln_g[j], sgu_ln_b[j], sgu_w_s[j], sgu_b_s[j],
                                  sgu_w_out[j], sgu_b_out[j])
            mixed.append(outs)
            order = outs[5]
        srcs, order = _moe_layer(mixed, m2, ln2, moe_w1, b1, moe_w2, b2, layer)
    out = order
    for g in range(n_groups):
        out = _moe_post(srcs[g], starts[g], nb, out, keep_prev=g > 0)
    return out
```
